```python
import math
import jax, jax.numpy as jnp
from jax import lax
import numpy as np

D_MODEL = 2048
BATCH = 16
SEQ = 256
DEPTH = 1
DEC_BATCH = 4
DEC_SEQ = 2048
PAST_LEN = 256

GRID_W = 64
ATT_HEAD_DIM = 64
V_HEAD_DIM = 2 * ATT_HEAD_DIM
ATT_WIDTH = D_MODEL // 2
N_ATT_HEADS = ATT_WIDTH // V_HEAD_DIM
QK_WIDTH = N_ATT_HEADS * 2 * ATT_HEAD_DIM
SGU_WIDTH = D_MODEL // 2
SGU_GROUP_DIM = 128
N_SGU_GROUPS = SGU_WIDTH // SGU_GROUP_DIM
CHUNK = 128
MIX_WIDTH = ATT_WIDTH + SGU_WIDTH
IN_WIDTH = 2 * QK_WIDTH + ATT_WIDTH + 2 * SGU_WIDTH
D_FF = 4 * D_MODEL
Q_BLOCK = 128
ROPE_BASE = 10000.0
EPS = 1e-6

kernel_name = "hybrid_diffattn_sgu_dit_step"


def rms_norm(x, w):
    xf = x.astype(jnp.float32)
    y = xf * lax.rsqrt(jnp.mean(xf * xf, axis=-1, keepdims=True) + EPS)
    return (y * w.astype(jnp.float32)).astype(x.dtype)


def layer_norm(x, w):
    xf = x.astype(jnp.float32)
    mu = jnp.mean(xf, axis=-1, keepdims=True)
    xc = xf - mu
    y = xc * lax.rsqrt(jnp.mean(xc * xc, axis=-1, keepdims=True) + EPS)
    return (y * w.astype(jnp.float32)).astype(x.dtype)


def grid_rope_tables(n, dtype):
    rows = n // GRID_W
    r, col = jnp.meshgrid(jnp.arange(rows), jnp.arange(GRID_W), indexing="ij")
    r = r.reshape(-1).astype(jnp.float32)
    col = col.reshape(-1).astype(jnp.float32)
    n_freq = ATT_HEAD_DIM // 4
    freqs = ROPE_BASE ** (-jnp.arange(n_freq, dtype=jnp.float32) / n_freq)
    ang_r = r[:, None] * freqs
    ang_c = col[:, None] * freqs
    ang = jnp.concatenate([ang_r, ang_r, ang_c, ang_c], axis=-1)
    return jnp.cos(ang).astype(dtype), jnp.sin(ang).astype(dtype)


def apply_rope(x, cos, sin):
    q = ATT_HEAD_DIM // 4
    a1, a2, b1, b2 = jnp.split(x, [q, 2 * q, 3 * q], axis=-1)
    rot = jnp.concatenate([-a2, a1, -b2, b1], axis=-1)
    c = cos[None, :, None, None, :]
    s = sin[None, :, None, None, :]
    return x * c + rot * s


def diff_attention(q, k, v, lam, subln_w, lam_init):
    b, nq = q.shape[0], q.shape[1]
    nblk = nq // Q_BLOCK
    qb = q.reshape(b, nblk, Q_BLOCK, N_ATT_HEADS, 2, ATT_HEAD_DIM).swapaxes(0, 1)
    scale = ATT_HEAD_DIM ** -0.5
    lam32 = lam.astype(jnp.float32)

    def block(qi):
        s = jnp.einsum("bqhcd,bkhcd->bhcqk", qi, k,
                       preferred_element_type=jnp.float32) * scale
        p = jax.nn.softmax(s, axis=-1)
        a = p[:, :, 0] - lam32 * p[:, :, 1]
        return jnp.einsum("bhqk,bkhe->bqhe", a.astype(v.dtype), v)

    o = lax.map(block, qb)
    o = o.swapaxes(0, 1).reshape(b, nq, N_ATT_HEADS, V_HEAD_DIM)
    o = rms_norm(o, subln_w) * (1.0 - lam_init)
    return o.reshape(b, nq, ATT_WIDTH)


def spatial_gating(u, g, norm_w, w_s, b_s):
    b, n, _ = u.shape
    u = jax.nn.gelu(u)
    g = layer_norm(jax.nn.gelu(g), norm_w)
    gc = g.reshape(b, n // CHUNK, CHUNK, N_SGU_GROUPS, SGU_GROUP_DIM)
    mixed = jnp.einsum("gpq,bcqgd->bcpgd", w_s, gc) + b_s.T[None, None, :, :, None]
    return u * mixed.reshape(b, n, SGU_WIDTH)


def adaln(c, w_ada, b_ada):
    mod = jax.nn.silu(c) @ w_ada + b_ada
    return [m[:, None, :] for m in jnp.split(mod, 6, axis=-1)]


def modulate(x, w, shift, scale):
    return rms_norm(x, w) * (1.0 + scale) + shift


def project_mixers(h, w_in, q_norm_w, k_norm_w):
    b, n, _ = h.shape
    z = h @ w_in
    q, k, v, u, g = jnp.split(
        z, [QK_WIDTH, 2 * QK_WIDTH, 2 * QK_WIDTH + ATT_WIDTH,
            2 * QK_WIDTH + ATT_WIDTH + SGU_WIDTH], axis=-1)
    q = rms_norm(q.reshape(b, n, N_ATT_HEADS, 2, ATT_HEAD_DIM), q_norm_w)
    k = rms_norm(k.reshape(b, n, N_ATT_HEADS, 2, ATT_HEAD_DIM), k_norm_w)
    v = v.reshape(b, n, N_ATT_HEADS, V_HEAD_DIM)
    return q, k, v, u, g


def sq_relu_mlp(h, w1, w2):
    return jnp.square(jax.nn.relu(h @ w1)) @ w2


def setup_inputs(seed: int = 0) -> dict:
    key = jax.random.key(seed)
    ks = jax.random.split(key, 24)
    f32 = jnp.float32
    nrm = lambda k, s, sc: jax.random.normal(k, s, f32) * sc
    L, D = DEPTH, D_MODEL
    return {
        "x_prompt": nrm(ks[0], (BATCH, SEQ, D), 1.0),
        "x_sample": nrm(ks[1], (DEC_BATCH, DEC_SEQ, D), 1.0),
        "cache_k": nrm(ks[2], (DEC_BATCH, L, PAST_LEN, N_ATT_HEADS, 2, ATT_HEAD_DIM), 1.0),
        "cache_v": nrm(ks[3], (DEC_BATCH, L, PAST_LEN, N_ATT_HEADS, V_HEAD_DIM), 1.0),
        "c": nrm(ks[4], (DEC_BATCH, D), 1.0),
        "c_ctx": nrm(ks[5], (D,), 1.0),
        "w_ada": nrm(ks[6], (L, D, 6 * D), 0.5 * D ** -0.5),
        "b_ada": nrm(ks[7], (L, 6 * D), 0.02),
        "norm1_w": 1.0 + nrm(ks[8], (L, D), 0.02),
        "norm2_w": 1.0 + nrm(ks[9], (L, D), 0.02),
        "w_in": nrm(ks[10], (L, D, IN_WIDTH), D ** -0.5),
        "q_norm_w": 1.0 + nrm(ks[11], (L, ATT_HEAD_DIM), 0.02),
        "k_norm_w": 1.0 + nrm(ks[12], (L, ATT_HEAD_DIM), 0.02),
        "lambda_q1": nrm(ks[13], (L, ATT_HEAD_DIM), 0.1),
        "lambda_k1": nrm(ks[14], (L, ATT_HEAD_DIM), 0.1),
        "lambda_q2": nrm(ks[15], (L, ATT_HEAD_DIM), 0.1),
        "lambda_k2": nrm(ks[16], (L, ATT_HEAD_DIM), 0.1),
        "subln_w": 1.0 + nrm(ks[17], (L, V_HEAD_DIM), 0.02),
        "sgu_norm_w": 1.0 + nrm(ks[18], (L, SGU_WIDTH), 0.02),
        "w_s": nrm(ks[19], (L, N_SGU_GROUPS, CHUNK, CHUNK), CHUNK ** -0.5),
        "b_s": 1.0 + nrm(ks[20], (L, N_SGU_GROUPS, CHUNK), 0.01),
        "w_o": nrm(ks[21], (L, MIX_WIDTH, D), MIX_WIDTH ** -0.5),
        "w_ff1": nrm(ks[22], (L, D, D_FF), D ** -0.5),
        "w_ff2": nrm(ks[23], (L, D_FF, D), D_FF ** -0.5),
    }


def reference(x_prompt, x_sample, cache_k, cache_v, c, c_ctx, w_ada, b_ada,
              norm1_w, norm2_w, w_in, q_norm_w, k_norm_w, lambda_q1, lambda_k1,
              lambda_q2, lambda_k2, subln_w, sgu_norm_w, w_s, b_s, w_o,
              w_ff1, w_ff2):
    y_p = x_prompt
    y_s = x_sample
    cos, sin = grid_rope_tables(x_sample.shape[1], x_sample.dtype)
    new_k, new_v = [], []
    for l in range(DEPTH):
        lam_init = 0.8 - 0.6 * math.exp(-0.3 * l)
        lam = (jnp.exp(jnp.sum(lambda_q1[l].astype(jnp.float32) * lambda_k1[l].astype(jnp.float32)))
               - jnp.exp(jnp.sum(lambda_q2[l].astype(jnp.float32) * lambda_k2[l].astype(jnp.float32)))
               + lam_init)

        sh1, sc1, g1, sh2, sc2, g2 = adaln(c_ctx[None, :], w_ada[l], b_ada[l])
        h = modulate(y_p, norm1_w[l], sh1, sc1)
        q, k, v, u, g = project_mixers(h, w_in[l], q_norm_w[l], k_norm_w[l])
        new_k.append(k)
        new_v.append(v)
        att = diff_attention(q, k, v, lam, subln_w[l], lam_init)
        sgu = spatial_gating(u, g, sgu_norm_w[l], w_s[l], b_s[l])
        y_p = y_p + g1 * (jnp.concatenate([att, sgu], axis=-1) @ w_o[l])
        h = modulate(y_p, norm2_w[l], sh2, sc2)
        y_p = y_p + g2 * sq_relu_mlp(h, w_ff1[l], w_ff2[l])

        sh1, sc1, g1, sh2, sc2, g2 = adaln(c, w_ada[l], b_ada[l])
        h = modulate(y_s, norm1_w[l], sh1, sc1)
        q, k, v, u, g = project_mixers(h, w_in[l], q_norm_w[l], k_norm_w[l])
        q = apply_rope(q, cos, sin)
        k = apply_rope(k, cos, sin)
        k_all = jnp.concatenate([cache_k[:, l].astype(k.dtype), k], axis=1)
        v_all = jnp.concatenate([cache_v[:, l].astype(v.dtype), v], axis=1)
        att = diff_attention(q, k_all, v_all, lam, subln_w[l], lam_init)
        sgu = spatial_gating(u, g, sgu_norm_w[l], w_s[l], b_s[l])
        y_s = y_s + g1 * (jnp.concatenate([att, sgu], axis=-1) @ w_o[l])
        h = modulate(y_s, norm2_w[l], sh2, sc2)
        y_s = y_s + g2 * sq_relu_mlp(h, w_ff1[l], w_ff2[l])

    state_k = jnp.stack(new_k, axis=1)
    state_v = jnp.stack(new_v, axis=1)
    return (y_p, y_s, state_k, state_v)
```

```python
import functools
import math

import jax
import jax.numpy as jnp
import numpy as np
from jax import lax
from jax.experimental import pallas as pl
from jax.experimental.pallas import tpu as pltpu

F32 = jnp.float32
BF16 = jnp.bfloat16

HEAD_DIM = 64
V_DIM = 128
CHUNK = 128
GROUP = 128
GRID_W = 64
ROPE_BASE = 10000.0
EPS = 1e-6
N_MOD = 6
MOD_ROWS = 8
LANES = 128
MXU_DIM = 256
VMEM_LIMIT = 60 * 1024 * 1024


def _cparams(n_axes):
    return pltpu.CompilerParams(
        dimension_semantics=("arbitrary",) * n_axes,
        vmem_limit_bytes=VMEM_LIMIT)


def _adaln_kernel(c_ref, w_ref, b_ref, o_ref):
    c = c_ref[...]
    s = c * (1.0 / (1.0 + jnp.exp(-c)))
    o_ref[...] = jnp.dot(s.astype(BF16), w_ref[...].astype(BF16),
                         preferred_element_type=F32) + b_ref[...]


def _adaln(c_rows, w_ada, b_ada, tn=1024):
    d, n = w_ada.shape
    return pl.pallas_call(
        _adaln_kernel,
        grid=(n // tn,),
        in_specs=[pl.BlockSpec((MOD_ROWS, d), lambda j: (0, 0)),
                  pl.BlockSpec((d, tn), lambda j: (0, j)),
                  pl.BlockSpec((1, tn), lambda j: (0, j))],
        out_specs=pl.BlockSpec((MOD_ROWS, tn), lambda j: (0, j)),
        out_shape=jax.ShapeDtypeStruct((MOD_ROWS, n), F32),
        compiler_params=_cparams(1),
        name="adaln",
    )(c_rows, w_ada, b_ada.reshape(1, n))


def _gelu(x):
    return jax.nn.gelu(x, approximate=True)


def _head_norm(z, w_ref, bd_ref):
    outs = []
    for c in range(z.shape[1] // MXU_DIM):
        zc = z[:, c * MXU_DIM:(c + 1) * MXU_DIM]
        ss = jnp.dot((zc * zc).astype(BF16), bd_ref[...], preferred_element_type=F32)
        r = lax.rsqrt(ss * (1.0 / HEAD_DIM) + EPS)
        outs.append(zc * r * w_ref[:, c * MXU_DIM:(c + 1) * MXU_DIM])
    return outs


def _rope128(x, cos, sin_lo, sin_hi):
    quarter = HEAD_DIM // 4
    return (x * cos + pltpu.roll(x, LANES - quarter, 1) * sin_lo
            + pltpu.roll(x, quarter, 1) * sin_hi)


def _inproj_kernel(*refs, rope, emit_state):
    it = iter(refs)
    x_ref, sh_ref, sc_ref, n1_ref, w_ref, qw_ref, kw_ref, bd_ref = (next(it) for _ in range(8))
    if rope:
        cos_ref, slo_ref, shi_ref = (next(it) for _ in range(3))
    sgw_ref, ws_ref, bs_ref = (next(it) for _ in range(3))
    q_ref, k_ref, v_ref, sgu_ref = (next(it) for _ in range(4))
    if emit_state:
        sk_ref, sv_ref = (next(it) for _ in range(2))
    h_scr, gu_scr = (next(it) for _ in range(2))

    j = pl.program_id(1)
    tm = x_ref.shape[0]

    @pl.when(j == 0)
    def _():
        x = x_ref[...]
        ms = jnp.mean(x * x, axis=-1, keepdims=True)
        y = x * lax.rsqrt(ms + EPS) * n1_ref[...]
        h_scr[...] = (y * (1.0 + sc_ref[...]) + sh_ref[...]).astype(BF16)

    def proj():
        return jnp.dot(h_scr[...], w_ref[...], preferred_element_type=F32)

    def qk_epilogue(w_norm_ref, out_ref, scale, state_ref):
        blocks = _head_norm(proj(), w_norm_ref, bd_ref)
        for c, n in enumerate(blocks):
            if state_ref is not None:
                state_ref[:, c * MXU_DIM:(c + 1) * MXU_DIM] = n
            for hh in range(MXU_DIM // LANES):
                xh = n[:, hh * LANES:(hh + 1) * LANES]
                if rope:
                    xh = _rope128(xh, cos_ref[...], slo_ref[...], shi_ref[...])
                if scale != 1.0:
                    xh = xh * scale
                lo = c * MXU_DIM + hh * LANES
                out_ref[:, lo:lo + LANES] = xh.astype(BF16)

    @pl.when(j == 0)
    def _():
        qk_epilogue(qw_ref, q_ref, HEAD_DIM ** -0.5, None)

    @pl.when(j == 1)
    def _():
        qk_epilogue(kw_ref, k_ref, 1.0, sk_ref if emit_state else None)

    @pl.when(j == 2)
    def _():
        z = proj()
        v_ref[...] = z.astype(BF16)
        if emit_state:
            sv_ref[...] = z

    @pl.when(j == 3)
    def _():
        gu_scr[...] = _gelu(proj())

    @pl.when(j == 4)
    def _():
        g = _gelu(proj())
        mu = jnp.mean(g, axis=-1, keepdims=True)
        gc = g - mu
        var = jnp.mean(gc * gc, axis=-1, keepdims=True)
        gn = (gc * lax.rsqrt(var + EPS) * sgw_ref[...]).astype(BF16)
        n_chunks = tm // CHUNK
        for grp in range(gn.shape[1] // GROUP):
            cols = slice(grp * GROUP, (grp + 1) * GROUP)
            rhs = jnp.concatenate(
                [gn[c * CHUNK:(c + 1) * CHUNK, cols] for c in range(n_chunks)], axis=1)
            mixed = jnp.dot(ws_ref[grp], rhs, preferred_element_type=F32) + bs_ref[grp]
            for c in range(n_chunks):
                rows = slice(c * CHUNK, (c + 1) * CHUNK)
                sgu_ref[rows, cols] = (
                    gu_scr[rows, cols] * mixed[:, c * CHUNK:(c + 1) * CHUNK]).astype(BF16)


def _inproj(x, mod, mod_row, norm1_w, w_in, qw, kw, bd, rope_tabs, sgw, ws, bs,
            *, tm, emit_state):
    m, d = x.shape
    n_in = w_in.shape[1]
    width = n_in // 5
    rope = rope_tabs is not None
    row_spec = lambda w: pl.BlockSpec((1, w), lambda i, j: (0, 0))
    mod_spec = lambda which: pl.BlockSpec(
        (None, None, 1, d), lambda i, j: (mod_row(i), which, 0, 0))
    in_specs = [
        pl.BlockSpec((tm, d), lambda i, j: (i, 0)),
        mod_spec(0), mod_spec(1), row_spec(d),
        pl.BlockSpec((d, width), lambda i, j: (0, j)),
        row_spec(width), row_spec(width),
        pl.BlockSpec((MXU_DIM, MXU_DIM), lambda i, j: (0, 0)),
    ]
    args = [x, mod, mod, norm1_w, w_in, qw, kw, bd]
    if rope:
        seq_tiles = rope_tabs[0].shape[0] // tm
        tab_spec = pl.BlockSpec((tm, LANES), lambda i, j: (i % seq_tiles, 0))
        in_specs += [tab_spec] * 3
        args += list(rope_tabs)
    in_specs += [row_spec(width),
                 pl.BlockSpec(ws.shape, lambda i, j: (0, 0, 0)),
                 pl.BlockSpec(bs.shape, lambda i, j: (0, 0, 0))]
    args += [sgw, ws, bs]
    out_spec = pl.BlockSpec((tm, width), lambda i, j: (i, 0))
    out_shape = [jax.ShapeDtypeStruct((m, width), BF16)] * 4
    if emit_state:
        out_shape += [jax.ShapeDtypeStruct((m, width), F32)] * 2
    return pl.pallas_call(
        functools.partial(_inproj_kernel, rope=rope, emit_state=emit_state),
        grid=(m // tm, 5),
        in_specs=in_specs,
        out_specs=[out_spec] * len(out_shape),
        out_shape=out_shape,
        scratch_shapes=[pltpu.VMEM((tm, d), BF16), pltpu.VMEM((tm, width), F32)],
        compiler_params=_cparams(2),
        name="inproj_rope" if rope else "inproj_ctx",
    )(*args)


def _lambda(lq1_ref, lk1_ref, lq2_ref, lk2_ref, lam_init):
    a = jnp.sum(lq1_ref[...] * lk1_ref[...], axis=-1, keepdims=True)
    b = jnp.sum(lq2_ref[...] * lk2_ref[...], axis=-1, keepdims=True)
    return jnp.exp(a) - jnp.exp(b) + lam_init


def _diff_attn_head(q, k, v, lam, subln_w, lam_init):
    tq = q.shape[0]
    lane = lax.broadcasted_iota(jnp.int32, q.shape, 1)
    zero = jnp.zeros_like(q)
    q2 = jnp.concatenate([jnp.where(lane < HEAD_DIM, q, zero),
                          jnp.where(lane >= HEAD_DIM, q, zero)], axis=0)
    s = lax.dot_general(q2, k, (((1,), (1,)), ((), ())), preferred_element_type=F32)
    e = jnp.exp(s - jnp.max(s, axis=-1, keepdims=True))
    inv = 1.0 / jnp.sum(e, axis=-1, keepdims=True)
    a = e[:tq] * inv[:tq] - e[tq:] * (lam * inv[tq:])
    o = jnp.dot(a.astype(BF16), v, preferred_element_type=F32)
    ms = jnp.mean(o * o, axis=-1, keepdims=True)
    return o * lax.rsqrt(ms + EPS) * subln_w * (1.0 - lam_init)


def _attn_cached_kernel(q_ref, kn_ref, vn_ref, kc_ref, vc_ref, lq1, lk1, lq2, lk2, sw_ref,
                        o_ref, k_scr, v_scr, *, lam_init):
    past = kc_ref.shape[0]

    @pl.when(pl.program_id(2) == 0)
    def _():
        k_scr[:past] = kc_ref[...].astype(BF16)
        k_scr[past:] = kn_ref[...]
        v_scr[:past] = vc_ref[...].astype(BF16)
        v_scr[past:] = vn_ref[...]

    lam = _lambda(lq1, lk1, lq2, lk2, lam_init)
    o_ref[...] = _diff_attn_head(q_ref[...], k_scr[...], v_scr[...], lam, sw_ref[...],
                                 lam_init).astype(o_ref.dtype)


def _attn_cached(q, k, v, cache_k, cache_v, lams, subln_w, *, n_batch, lam_init, tq=256):
    m, width = q.shape
    n_heads = width // V_DIM
    seq = m // n_batch
    past = cache_k.shape[0] // n_batch
    nq = seq // tq
    vec = lambda w: pl.BlockSpec((1, w), lambda b, h, t: (0, 0))
    return pl.pallas_call(
        functools.partial(_attn_cached_kernel, lam_init=lam_init),
        grid=(n_batch, n_heads, nq),
        in_specs=[pl.BlockSpec((tq, V_DIM), lambda b, h, t: (b * nq + t, h)),
                  pl.BlockSpec((seq, V_DIM), lambda b, h, t: (b, h)),
                  pl.BlockSpec((seq, V_DIM), lambda b, h, t: (b, h)),
                  pl.BlockSpec((past, V_DIM), lambda b, h, t: (b, h)),
                  pl.BlockSpec((past, V_DIM), lambda b, h, t: (b, h)),
                  vec(HEAD_DIM), vec(HEAD_DIM), vec(HEAD_DIM), vec(HEAD_DIM), vec(V_DIM)],
        out_specs=pl.BlockSpec((tq, V_DIM), lambda b, h, t: (b * nq + t, h)),
        out_shape=jax.ShapeDtypeStruct((m, width), BF16),
        scratch_shapes=[pltpu.VMEM((past + seq, V_DIM), BF16)] * 2,
        compiler_params=_cparams(3),
        name="attn_latent",
    )(q, k, v, cache_k, cache_v, *lams, subln_w)


def _attn_self_kernel(q_ref, k_ref, v_ref, lq1, lk1, lq2, lk2, sw_ref, o_ref, *, lam_init):
    lam = _lambda(lq1, lk1, lq2, lk2, lam_init)
    for h in range(q_ref.shape[1] // V_DIM):
        cols = slice(h * V_DIM, (h + 1) * V_DIM)
        o_ref[:, cols] = _diff_attn_head(q_ref[:, cols], k_ref[:, cols], v_ref[:, cols], lam,
                                         sw_ref[...], lam_init).astype(o_ref.dtype)


def _attn_self(q, k, v, lams, subln_w, *, n_batch, lam_init):
    m, width = q.shape
    seq = m // n_batch
    blk = pl.BlockSpec((seq, width), lambda b: (b, 0))
    vec = lambda w: pl.BlockSpec((1, w), lambda b: (0, 0))
    return pl.pallas_call(
        functools.partial(_attn_self_kernel, lam_init=lam_init),
        grid=(n_batch,),
        in_specs=[blk, blk, blk,
                  vec(HEAD_DIM), vec(HEAD_DIM), vec(HEAD_DIM), vec(HEAD_DIM), vec(V_DIM)],
        out_specs=blk,
        out_shape=jax.ShapeDtypeStruct((m, width), BF16),
        compiler_params=_cparams(1),
        name="attn_ctx",
    )(q, k, v, *lams, subln_w)


def _outproj_kernel(att_ref, sgu_ref, wo_ref, x_ref, g1_ref, sh2_ref, sc2_ref, n2_ref,
                    y_ref, h_ref):
    half = att_ref.shape[1]
    mix = jnp.dot(att_ref[...], wo_ref[:half], preferred_element_type=F32)
    mix = mix + jnp.dot(sgu_ref[...], wo_ref[half:], preferred_element_type=F32)
    y = x_ref[...] + g1_ref[...] * mix
    y_ref[...] = y
    ms = jnp.mean(y * y, axis=-1, keepdims=True)
    n = y * lax.rsqrt(ms + EPS) * n2_ref[...]
    h_ref[...] = (n * (1.0 + sc2_ref[...]) + sh2_ref[...]).astype(BF16)


def _outproj(att, sgu, w_o, x, mod, mod_row, norm2_w, *, tm):
    m, d = x.shape
    half = att.shape[1]
    mod_spec = lambda which: pl.BlockSpec(
        (None, None, 1, d), lambda i: (mod_row(i), which, 0, 0))
    return pl.pallas_call(
        _outproj_kernel,
        grid=(m // tm,),
        in_specs=[pl.BlockSpec((tm, half), lambda i: (i, 0)),
                  pl.BlockSpec((tm, half), lambda i: (i, 0)),
                  pl.BlockSpec(w_o.shape, lambda i: (0, 0)),
                  pl.BlockSpec((tm, d), lambda i: (i, 0)),
                  mod_spec(2), mod_spec(3), mod_spec(4),
                  pl.BlockSpec((1, d), lambda i: (0, 0))],
        out_specs=[pl.BlockSpec((tm, d), lambda i: (i, 0))] * 2,
        out_shape=[jax.ShapeDtypeStruct((m, d), F32), jax.ShapeDtypeStruct((m, d), BF16)],
        compiler_params=_cparams(1),
        name="outproj",
    )(att, sgu, w_o, x, mod, mod, mod, norm2_w)


def _mlp_kernel(h_ref, w1_ref, w2_ref, y1_ref, g2_ref, o_ref, acc_ref):
    f = pl.program_id(1)
    hid = jnp.dot(h_ref[...], w1_ref[...], preferred_element_type=F32)
    hid = jnp.square(jnp.maximum(hid, 0.0)).astype(BF16)
    part = jnp.dot(hid, w2_ref[...], preferred_element_type=F32)

    @pl.when(f == 0)
    def _():
        acc_ref[...] = part

    @pl.when(f > 0)
    def _():
        acc_ref[...] += part

    @pl.when(f == pl.num_programs(1) - 1)
    def _():
        o_ref[...] = y1_ref[...] + g2_ref[...] * acc_ref[...]


def _mlp(h, w1, w2, y1, mod, mod_row, *, tm, tf):
    m, d = h.shape
    d_ff = w1.shape[1]
    return pl.pallas_call(
        _mlp_kernel,
        grid=(m // tm, d_ff // tf),
        in_specs=[pl.BlockSpec((tm, d), lambda i, f: (i, 0)),
                  pl.BlockSpec((d, tf), lambda i, f: (0, f)),
                  pl.BlockSpec((tf, d), lambda i, f: (f, 0)),
                  pl.BlockSpec((tm, d), lambda i, f: (i, 0)),
                  pl.BlockSpec((None, None, 1, d), lambda i, f: (mod_row(i), 5, 0, 0))],
        out_specs=pl.BlockSpec((tm, d), lambda i, f: (i, 0)),
        out_shape=jax.ShapeDtypeStruct((m, d), F32),
        scratch_shapes=[pltpu.VMEM((tm, d), F32)],
        compiler_params=_cparams(2),
        name="mlp",
    )(h, w1, w2, y1, mod)


def _rope_tables(n):
    rows = n // GRID_W
    r, col = jnp.meshgrid(jnp.arange(rows), jnp.arange(GRID_W), indexing="ij")
    r = r.reshape(-1).astype(F32)
    col = col.reshape(-1).astype(F32)
    n_freq = HEAD_DIM // 4
    freqs = ROPE_BASE ** (-jnp.arange(n_freq, dtype=F32) / n_freq)
    ang_r = r[:, None] * freqs
    ang_c = col[:, None] * freqs
    ang = jnp.concatenate([ang_r, ang_r, ang_c, ang_c], axis=-1)
    cos = jnp.tile(jnp.cos(ang), (1, LANES // HEAD_DIM))
    sin = jnp.tile(jnp.sin(ang), (1, LANES // HEAD_DIM))
    first = (jnp.arange(LANES) % (2 * n_freq)) < n_freq
    return cos, jnp.where(first, -sin, 0.0), jnp.where(first, 0.0, sin)


def _block_diag_ones(n, blk):
    idx = np.arange(n) // blk
    return jnp.asarray(idx[:, None] == idx[None, :], dtype=BF16)


def kernel(x_prompt, x_sample, cache_k, cache_v, c, c_ctx, w_ada, b_ada, norm1_w, norm2_w,
           w_in, q_norm_w, k_norm_w, lambda_q1, lambda_k1, lambda_q2, lambda_k2, subln_w,
           sgu_norm_w, w_s, b_s, w_o, w_ff1, w_ff2):
    n_ctx, ctx_len, d = x_prompt.shape
    n_lat, lat_len, _ = x_sample.shape
    depth = w_in.shape[0]
    past = cache_k.shape[2]
    width = w_o.shape[1] // 2
    tm = 512
    assert n_lat + 1 <= MOD_ROWS and lat_len % tm == 0 and ctx_len % CHUNK == 0

    y_p = x_prompt.reshape(n_ctx * ctx_len, d)
    y_s = x_sample.reshape(n_lat * lat_len, d)
    rope_tabs = _rope_tables(lat_len)
    bd = _block_diag_ones(MXU_DIM, HEAD_DIM)
    c_rows = jnp.concatenate(
        [c_ctx[None, :], c, jnp.zeros((MOD_ROWS - 1 - n_lat, d), F32)], axis=0)
    lat_tiles = lat_len // tm
    ctx_row = lambda i: 0
    lat_row = lambda i: 1 + i // lat_tiles

    state_k, state_v = [], []
    for l in range(depth):
        lam_init = 0.8 - 0.6 * math.exp(-0.3 * l)
        w_in_l, w_o_l = w_in[l].astype(BF16), w_o[l].astype(BF16)
        w1_l, w2_l = w_ff1[l].astype(BF16), w_ff2[l].astype(BF16)
        ws_l = w_s[l].astype(BF16)
        bs_l = b_s[l][:, :, None]
        row = lambda a: a[l].reshape(1, -1)
        qw = jnp.tile(row(q_norm_w), (1, width // HEAD_DIM))
        kw = jnp.tile(row(k_norm_w), (1, width // HEAD_DIM))
        lams = (row(lambda_q1), row(lambda_k1), row(lambda_q2), row(lambda_k2))

        mod = _adaln(c_rows, w_ada[l], b_ada[l]).reshape(MOD_ROWS, N_MOD, 1, d)

        def mixers(x, mod_row, tabs, emit_state):
            return _inproj(x, mod, mod_row, row(norm1_w), w_in_l, qw, kw, bd, tabs,
                           row(sgu_norm_w), ws_l, bs_l, tm=tm, emit_state=emit_state)

        def finish(x, att, sgu, mod_row):
            y1, h2 = _outproj(att, sgu, w_o_l, x, mod, mod_row, row(norm2_w), tm=tm)
            return _mlp(h2, w1_l, w2_l, y1, mod, mod_row, tm=tm, tf=512)

        q, k, v, sgu, sk, sv = mixers(y_p, ctx_row, None, True)
        state_k.append(sk.reshape(n_ctx, ctx_len, width // V_DIM, 2, HEAD_DIM))
        state_v.append(sv.reshape(n_ctx, ctx_len, width // V_DIM, V_DIM))
        att = _attn_self(q, k, v, lams, row(subln_w), n_batch=n_ctx, lam_init=lam_init)
        y_p = finish(y_p, att, sgu, ctx_row)

        q, k, v, sgu = mixers(y_s, lat_row, rope_tabs, False)
        att = _attn_cached(q, k, v,
                           cache_k[:, l].reshape(n_lat * past, width),
                           cache_v[:, l].reshape(n_lat * past, width),
                           lams, row(subln_w), n_batch=n_lat, lam_init=lam_init)
        y_s = finish(y_s, att, sgu, lat_row)

    return (y_p.reshape(n_ctx, ctx_len, d), y_s.reshape(n_lat, lat_len, d),
            jnp.stack(state_k, axis=1), jnp.stack(state_v, axis=1))
```

```python
import functools
import math

import jax
import jax.numpy as jnp
import numpy as np
from jax import lax
from jax.experimental import pallas as pl
from jax.experimental.pallas import tpu as pltpu

F32 = jnp.float32
BF16 = jnp.bfloat16

HEAD_DIM = 64
V_DIM = 128
CHUNK = 128
GROUP = 128
GRID_W = 64
ROPE_BASE = 10000.0
EPS = 1e-6
N_MOD = 6
MOD_ROWS = 8
LANES = 128
MXU_DIM = 256
VMEM_LIMIT = 60 * 1024 * 1024
Q_SCALE = HEAD_DIM ** -0.5 * math.log2(math.e)


def _cparams(n_axes):
    return pltpu.CompilerParams(
        dimension_semantics=("arbitrary",) * n_axes,
        vmem_limit_bytes=VMEM_LIMIT)


def _adaln_kernel(c_ref, w_ref, b_ref, o_ref):
    c = c_ref[...]
    s = c * (1.0 / (1.0 + jnp.exp(-c)))
    o_ref[...] = jnp.dot(s.astype(BF16), w_ref[...].astype(BF16),
                         preferred_element_type=F32) + b_ref[...]


def _adaln(c_rows, w_ada, b_ada, tn=1024):
    d, n = w_ada.shape
    return pl.pallas_call(
        _adaln_kernel,
        grid=(n // tn,),
        in_specs=[pl.BlockSpec((MOD_ROWS, d), lambda j: (0, 0)),
                  pl.BlockSpec((d, tn), lambda j: (0, j)),
                  pl.BlockSpec((1, tn), lambda j: (0, j))],
        out_specs=pl.BlockSpec((MOD_ROWS, tn), lambda j: (0, j)),
        out_shape=jax.ShapeDtypeStruct((MOD_ROWS, n), F32),
        compiler_params=_cparams(1),
        name="adaln",
    )(c_rows, w_ada, b_ada.reshape(1, n))


def _gelu(x):
    return jax.nn.gelu(x, approximate=True)


def _head_norm(z, w_ref, bd_ref):
    outs = []
    for c in range(z.shape[1] // MXU_DIM):
        zc = z[:, c * MXU_DIM:(c + 1) * MXU_DIM]
        ss = jnp.dot((zc * zc).astype(BF16), bd_ref[...], preferred_element_type=F32)
        r = lax.rsqrt(ss * (1.0 / HEAD_DIM) + EPS)
        outs.append(zc * r * w_ref[:, c * MXU_DIM:(c + 1) * MXU_DIM])
    return outs


def _rope128(x, cos, sin_lo, sin_hi):
    quarter = HEAD_DIM // 4
    return (x * cos + pltpu.roll(x, LANES - quarter, 1) * sin_lo
            + pltpu.roll(x, quarter, 1) * sin_hi)


def _inproj_kernel(*refs, rope, emit_state):
    it = iter(refs)
    x_ref, sh_ref, sc_ref, n1_ref, w_ref, qw_ref, kw_ref, bd_ref = (next(it) for _ in range(8))
    if rope:
        cos_ref, slo_ref, shi_ref = (next(it) for _ in range(3))
    sgw_ref, ws_ref, bs_ref = (next(it) for _ in range(3))
    q_ref, k_ref, v_ref, sgu_ref = (next(it) for _ in range(4))
    if emit_state:
        sk_ref, sv_ref = (next(it) for _ in range(2))
    h_scr, gu_scr = (next(it) for _ in range(2))

    j = pl.program_id(1)
    tm = x_ref.shape[0]

    @pl.when(j == 0)
    def _():
        x = x_ref[...]
        ms = jnp.mean(x * x, axis=-1, keepdims=True)
        y = x * lax.rsqrt(ms + EPS) * n1_ref[...]
        h_scr[...] = (y * (1.0 + sc_ref[...]) + sh_ref[...]).astype(BF16)

    def proj():
        return jnp.dot(h_scr[...], w_ref[...], preferred_element_type=F32)

    def qk_epilogue(w_norm_ref, out_ref, scale, state_ref):
        blocks = _head_norm(proj(), w_norm_ref, bd_ref)
        for c, n in enumerate(blocks):
            if state_ref is not None:
                seq = state_ref.shape[2]
                for s in range(tm // seq):
                    state_ref[s, c * MXU_DIM:(c + 1) * MXU_DIM, :] = n[s * seq:(s + 1) * seq].T
            for hh in range(MXU_DIM // LANES):
                xh = n[:, hh * LANES:(hh + 1) * LANES]
                if rope:
                    xh = _rope128(xh, cos_ref[...], slo_ref[...], shi_ref[...])
                if scale != 1.0:
                    xh = xh * scale
                lo = c * MXU_DIM + hh * LANES
                out_ref[:, lo:lo + LANES] = xh.astype(BF16)

    @pl.when(j == 0)
    def _():
        qk_epilogue(qw_ref, q_ref, Q_SCALE, None)

    @pl.when(j == 1)
    def _():
        qk_epilogue(kw_ref, k_ref, 1.0, sk_ref if emit_state else None)

    @pl.when(j == 2)
    def _():
        z = proj()
        v_ref[...] = z.astype(BF16)
        if emit_state:
            sv_ref[...] = z

    @pl.when(j == 3)
    def _():
        gu_scr[...] = _gelu(proj())

    @pl.when(j == 4)
    def _():
        g = _gelu(proj())
        mu = jnp.mean(g, axis=-1, keepdims=True)
        gc = g - mu
        var = jnp.mean(gc * gc, axis=-1, keepdims=True)
        gn = (gc * lax.rsqrt(var + EPS) * sgw_ref[...]).astype(BF16)
        n_chunks = tm // CHUNK
        for grp in range(gn.shape[1] // GROUP):
            cols = slice(grp * GROUP, (grp + 1) * GROUP)
            rhs = jnp.concatenate(
                [gn[c * CHUNK:(c + 1) * CHUNK, cols] for c in range(n_chunks)], axis=1)
            mixed = jnp.dot(ws_ref[grp], rhs, preferred_element_type=F32) + bs_ref[grp]
            for c in range(n_chunks):
                rows = slice(c * CHUNK, (c + 1) * CHUNK)
                sgu_ref[rows, cols] = (
                    gu_scr[rows, cols] * mixed[:, c * CHUNK:(c + 1) * CHUNK]).astype(BF16)


def _inproj(x, mod, mod_row, norm1_w, w_in, qw, kw, bd, rope_tabs, sgw, ws, bs,
            *, tm, state_seq):
    m, d = x.shape
    n_in = w_in.shape[1]
    width = n_in // 5
    rope = rope_tabs is not None
    row_spec = lambda w: pl.BlockSpec((1, w), lambda i, j: (0, 0))
    mod_spec = lambda which: pl.BlockSpec(
        (None, None, 1, d), lambda i, j: (mod_row(i), which, 0, 0))
    in_specs = [
        pl.BlockSpec((tm, d), lambda i, j: (i, 0)),
        mod_spec(0), mod_spec(1), row_spec(d),
        pl.BlockSpec((d, width), lambda i, j: (0, j)),
        row_spec(width), row_spec(width),
        pl.BlockSpec((MXU_DIM, MXU_DIM), lambda i, j: (0, 0)),
    ]
    args = [x, mod, mod, norm1_w, w_in, qw, kw, bd]
    if rope:
        seq_tiles = rope_tabs[0].shape[0] // tm
        tab_spec = pl.BlockSpec((tm, LANES), lambda i, j: (i % seq_tiles, 0))
        in_specs += [tab_spec] * 3
        args += list(rope_tabs)
    in_specs += [row_spec(width),
                 pl.BlockSpec(ws.shape, lambda i, j: (0, 0, 0)),
                 pl.BlockSpec(bs.shape, lambda i, j: (0, 0, 0))]
    args += [sgw, ws, bs]
    out_spec = pl.BlockSpec((tm, width), lambda i, j: (i, 0))
    out_shape = [jax.ShapeDtypeStruct((m, width), BF16)] * 4
    out_specs = [out_spec] * 4
    emit_state = state_seq is not None
    if emit_state:
        out_shape += [jax.ShapeDtypeStruct((m // state_seq, width, state_seq), F32),
                      jax.ShapeDtypeStruct((m, width), F32)]
        out_specs += [pl.BlockSpec((tm // state_seq, width, state_seq), lambda i, j: (i, 0, 0)),
                      out_spec]
    return pl.pallas_call(
        functools.partial(_inproj_kernel, rope=rope, emit_state=emit_state),
        grid=(m // tm, 5),
        in_specs=in_specs,
        out_specs=out_specs,
        out_shape=out_shape,
        scratch_shapes=[pltpu.VMEM((tm, d), BF16), pltpu.VMEM((tm, width), F32)],
        compiler_params=_cparams(2),
        name="inproj_rope" if rope else "inproj_ctx",
    )(*args)


def _lambda(lq1_ref, lk1_ref, lq2_ref, lk2_ref, lam_init):
    a = jnp.sum(lq1_ref[...] * lk1_ref[...], axis=-1, keepdims=True)
    b = jnp.sum(lq2_ref[...] * lk2_ref[...], axis=-1, keepdims=True)
    return jnp.exp(a) - jnp.exp(b) + lam_init


def _lane_fold(op, acc, tile):
    for c in range(0, tile.shape[1], LANES):
        blk = tile[:, c:c + LANES]
        acc = blk if acc is None else op(acc, blk)
    return acc


class _ScoreTiles:
    def __init__(self, q, keys):
        lane = lax.broadcasted_iota(jnp.int32, q.shape, 1)
        zero = jnp.zeros_like(q)
        self.q2 = jnp.concatenate([jnp.where(lane < HEAD_DIM, q, zero),
                                   jnp.where(lane >= HEAD_DIM, q, zero)], axis=0)
        self.slices = []
        for k, feature_major in keys:
            n_keys = k.shape[1] if feature_major else k.shape[0]
            self.slices += [(k, feature_major, lo) for lo in range(0, n_keys, MXU_DIM)]
        self.tiles = []
        self.m_lanes = None

    def __len__(self):
        return len(self.slices)

    def compute(self, t):
        k, feature_major, lo = self.slices[t]
        if feature_major:
            st = jnp.dot(self.q2, k[:, lo:lo + MXU_DIM], preferred_element_type=F32)
        else:
            st = lax.dot_general(self.q2, k[lo:lo + MXU_DIM], (((1,), (1,)), ((), ())),
                                 preferred_element_type=F32)
        self.tiles.append(st)
        self.m_lanes = _lane_fold(jnp.maximum, self.m_lanes, st)

    def row_max(self):
        return jnp.max(self.m_lanes, axis=-1, keepdims=True)


def _attn_heads(n_heads, load_q, load_keys, load_values, store, lam, subln_w, lam_init):
    cur = _ScoreTiles(load_q(0), load_keys(0))
    for t in range(len(cur)):
        cur.compute(t)
    for h in range(n_heads):
        nxt = _ScoreTiles(load_q(h + 1), load_keys(h + 1)) if h + 1 < n_heads else None
        tq = cur.q2.shape[0] // 2
        m = cur.row_max()
        e_tiles, l_lanes = [], None
        for t in range(len(cur)):
            if nxt is not None:
                nxt.compute(t)
            et = jnp.exp2(cur.tiles[t] - m)
            e_tiles.append(et)
            l_lanes = _lane_fold(jnp.add, l_lanes, et)
        l = jnp.sum(l_lanes, axis=-1, keepdims=True)
        ratio = lam * l[:tq] / l[tq:]
        a = jnp.concatenate([(et[:tq] - et[tq:] * ratio).astype(BF16) for et in e_tiles], axis=1)
        o, lo = None, 0
        for v in load_values(h):
            part = jnp.dot(a[:, lo:lo + v.shape[0]], v, preferred_element_type=F32)
            o = part if o is None else o + part
            lo += v.shape[0]
        o = o * (1.0 / l[:tq])
        ms = jnp.mean(o * o, axis=-1, keepdims=True)
        store(h, o * lax.rsqrt(ms + EPS) * subln_w * (1.0 - lam_init))
        cur = nxt


def _attn_cached_kernel(q_ref, kn_ref, vn_ref, kct_ref, vc_ref, lq1, lk1, lq2, lk2, sw_ref,
                        o_ref, *, lam_init):
    lam = _lambda(lq1, lk1, lq2, lk2, lam_init)
    cols = lambda h: slice(h * V_DIM, (h + 1) * V_DIM)

    def store(h, o):
        o_ref[:, cols(h)] = o.astype(o_ref.dtype)

    _attn_heads(
        q_ref.shape[1] // V_DIM,
        lambda h: q_ref[:, cols(h)],
        lambda h: [(kct_ref[cols(h), :].astype(BF16), True), (kn_ref[:, cols(h)], False)],
        lambda h: [vc_ref[:, cols(h)].astype(BF16), vn_ref[:, cols(h)]],
        store, lam, sw_ref[...], lam_init)


def _attn_cached(q, k, v, cache_kt, cache_v, lams, subln_w, *, n_batch, lam_init, tq, heads):
    m, width = q.shape
    seq = m // n_batch
    past = cache_kt.shape[2]
    nq = seq // tq
    w = heads * V_DIM
    vec = lambda n: pl.BlockSpec((1, n), lambda b, g, t: (0, 0))
    return pl.pallas_call(
        functools.partial(_attn_cached_kernel, lam_init=lam_init),
        grid=(n_batch, width // w, nq),
        in_specs=[pl.BlockSpec((tq, w), lambda b, g, t: (b * nq + t, g)),
                  pl.BlockSpec((seq, w), lambda b, g, t: (b, g)),
                  pl.BlockSpec((seq, w), lambda b, g, t: (b, g)),
                  pl.BlockSpec((None, w, past), lambda b, g, t: (b, g, 0)),
                  pl.BlockSpec((past, w), lambda b, g, t: (b, g)),
                  vec(HEAD_DIM), vec(HEAD_DIM), vec(HEAD_DIM), vec(HEAD_DIM), vec(V_DIM)],
        out_specs=pl.BlockSpec((tq, w), lambda b, g, t: (b * nq + t, g)),
        out_shape=jax.ShapeDtypeStruct((m, width), BF16),
        compiler_params=_cparams(3),
        name="attn_latent",
    )(q, k, v, cache_kt, cache_v, *lams, subln_w)


def _attn_self_kernel(q_ref, k_ref, v_ref, lq1, lk1, lq2, lk2, sw_ref, o_ref, *, lam_init):
    lam = _lambda(lq1, lk1, lq2, lk2, lam_init)
    cols = lambda h: slice(h * V_DIM, (h + 1) * V_DIM)

    def store(h, o):
        o_ref[:, cols(h)] = o.astype(o_ref.dtype)

    _attn_heads(
        q_ref.shape[1] // V_DIM,
        lambda h: q_ref[:, cols(h)],
        lambda h: [(k_ref[:, cols(h)], False)],
        lambda h: [v_ref[:, cols(h)]],
        store, lam, sw_ref[...], lam_init)


def _attn_self(q, k, v, lams, subln_w, *, n_batch, lam_init):
    m, width = q.shape
    seq = m // n_batch
    blk = pl.BlockSpec((seq, width), lambda b: (b, 0))
    vec = lambda w: pl.BlockSpec((1, w), lambda b: (0, 0))
    return pl.pallas_call(
        functools.partial(_attn_self_kernel, lam_init=lam_init),
        grid=(n_batch,),
        in_specs=[blk, blk, blk,
                  vec(HEAD_DIM), vec(HEAD_DIM), vec(HEAD_DIM), vec(HEAD_DIM), vec(V_DIM)],
        out_specs=blk,
        out_shape=jax.ShapeDtypeStruct((m, width), BF16),
        compiler_params=_cparams(1),
        name="attn_ctx",
    )(q, k, v, *lams, subln_w)


def _outproj_kernel(att_ref, sgu_ref, wo_ref, x_ref, g1_ref, sh2_ref, sc2_ref, n2_ref,
                    y_ref, h_ref):
    half = att_ref.shape[1]
    mix = jnp.dot(att_ref[...], wo_ref[:half], preferred_element_type=F32)
    mix = mix + jnp.dot(sgu_ref[...], wo_ref[half:], preferred_element_type=F32)
    y = x_ref[...] + g1_ref[...] * mix
    y_ref[...] = y
    ms = jnp.mean(y * y, axis=-1, keepdims=True)
    n = y * lax.rsqrt(ms + EPS) * n2_ref[...]
    h_ref[...] = (n * (1.0 + sc2_ref[...]) + sh2_ref[...]).astype(BF16)


def _outproj(att, sgu, w_o, x, mod, mod_row, norm2_w, *, tm):
    m, d = x.shape
    half = att.shape[1]
    mod_spec = lambda which: pl.BlockSpec(
        (None, None, 1, d), lambda i: (mod_row(i), which, 0, 0))
    return pl.pallas_call(
        _outproj_kernel,
        grid=(m // tm,),
        in_specs=[pl.BlockSpec((tm, half), lambda i: (i, 0)),
                  pl.BlockSpec((tm, half), lambda i: (i, 0)),
                  pl.BlockSpec(w_o.shape, lambda i: (0, 0)),
                  pl.BlockSpec((tm, d), lambda i: (i, 0)),
                  mod_spec(2), mod_spec(3), mod_spec(4),
                  pl.BlockSpec((1, d), lambda i: (0, 0))],
        out_specs=[pl.BlockSpec((tm, d), lambda i: (i, 0))] * 2,
        out_shape=[jax.ShapeDtypeStruct((m, d), F32), jax.ShapeDtypeStruct((m, d), BF16)],
        compiler_params=_cparams(1),
        name="outproj",
    )(att, sgu, w_o, x, mod, mod, mod, norm2_w)


def _mlp_kernel(h_ref, w1_ref, w2_ref, y1_ref, g2_ref, o_ref):
    f = pl.program_id(1)

    @pl.when(f == 0)
    def _():
        o_ref[...] = jnp.zeros_like(o_ref)

    hid = jnp.dot(h_ref[...], w1_ref[...], preferred_element_type=F32)
    hid = jnp.square(jnp.maximum(hid, 0.0)).astype(BF16)
    o_ref[...] += jnp.dot(hid, w2_ref[...], preferred_element_type=F32)

    @pl.when(f == pl.num_programs(1) - 1)
    def _():
        o_ref[...] = y1_ref[...] + g2_ref[...] * o_ref[...]


def _mlp(h, w1, w2, y1, mod, mod_row, *, tm, tf):
    m, d = h.shape
    d_ff = w1.shape[1]
    return pl.pallas_call(
        _mlp_kernel,
        grid=(m // tm, d_ff // tf),
        in_specs=[pl.BlockSpec((tm, d), lambda i, f: (i, 0)),
                  pl.BlockSpec((d, tf), lambda i, f: (0, f)),
                  pl.BlockSpec((tf, d), lambda i, f: (f, 0)),
                  pl.BlockSpec((tm, d), lambda i, f: (i, 0)),
                  pl.BlockSpec((None, None, 1, d), lambda i, f: (mod_row(i), 5, 0, 0))],
        out_specs=pl.BlockSpec((tm, d), lambda i, f: (i, 0)),
        out_shape=jax.ShapeDtypeStruct((m, d), F32),
        compiler_params=_cparams(2),
        name="mlp",
    )(h, w1, w2, y1, mod)


def _rope_tables(n):
    rows = n // GRID_W
    r, col = jnp.meshgrid(jnp.arange(rows), jnp.arange(GRID_W), indexing="ij")
    r = r.reshape(-1).astype(F32)
    col = col.reshape(-1).astype(F32)
    n_freq = HEAD_DIM // 4
    freqs = ROPE_BASE ** (-jnp.arange(n_freq, dtype=F32) / n_freq)
    ang_r = r[:, None] * freqs
    ang_c = col[:, None] * freqs
    ang = jnp.concatenate([ang_r, ang_r, ang_c, ang_c], axis=-1)
    cos = jnp.tile(jnp.cos(ang), (1, LANES // HEAD_DIM))
    sin = jnp.tile(jnp.sin(ang), (1, LANES // HEAD_DIM))
    first = (jnp.arange(LANES) % (2 * n_freq)) < n_freq
    return cos, jnp.where(first, -sin, 0.0), jnp.where(first, 0.0, sin)


def _block_diag_ones(n, blk):
    idx = np.arange(n) // blk
    return jnp.asarray(idx[:, None] == idx[None, :], dtype=BF16)


def kernel(x_prompt, x_sample, cache_k, cache_v, c, c_ctx, w_ada, b_ada, norm1_w, norm2_w,
           w_in, q_norm_w, k_norm_w, lambda_q1, lambda_k1, lambda_q2, lambda_k2, subln_w,
           sgu_norm_w, w_s, b_s, w_o, w_ff1, w_ff2):
    n_ctx, ctx_len, d = x_prompt.shape
    n_lat, lat_len, _ = x_sample.shape
    depth = w_in.shape[0]
    past = cache_k.shape[2]
    width = w_o.shape[1] // 2
    n_heads = width // V_DIM
    tm = 512
    assert n_lat + 1 <= MOD_ROWS and lat_len % tm == 0 and ctx_len % CHUNK == 0

    y_p = x_prompt.reshape(n_ctx * ctx_len, d)
    y_s = x_sample.reshape(n_lat * lat_len, d)
    rope_tabs = _rope_tables(lat_len)
    bd = _block_diag_ones(MXU_DIM, HEAD_DIM)
    c_rows = jnp.concatenate(
        [c_ctx[None, :], c, jnp.zeros((MOD_ROWS - 1 - n_lat, d), F32)], axis=0)
    lat_tiles = lat_len // tm
    ctx_row = lambda i: 0
    lat_row = lambda i: 1 + i // lat_tiles

    state_k, state_v = [], []
    for l in range(depth):
        lam_init = 0.8 - 0.6 * math.exp(-0.3 * l)
        w_in_l, w_o_l = w_in[l].astype(BF16), w_o[l].astype(BF16)
        w1_l, w2_l = w_ff1[l].astype(BF16), w_ff2[l].astype(BF16)
        ws_l = w_s[l].astype(BF16)
        bs_l = b_s[l][:, :, None]
        row = lambda a: a[l].reshape(1, -1)
        qw = jnp.tile(row(q_norm_w), (1, width // HEAD_DIM))
        kw = jnp.tile(row(k_norm_w), (1, width // HEAD_DIM))
        lams = (row(lambda_q1), row(lambda_k1), row(lambda_q2), row(lambda_k2))

        mod = _adaln(c_rows, w_ada[l], b_ada[l]).reshape(MOD_ROWS, N_MOD, 1, d)

        def mixers(x, mod_row, tabs, state_seq):
            return _inproj(x, mod, mod_row, row(norm1_w), w_in_l, qw, kw, bd, tabs,
                           row(sgu_norm_w), ws_l, bs_l, tm=tm, state_seq=state_seq)

        def finish(x, att, sgu, mod_row):
            y1, h2 = _outproj(att, sgu, w_o_l, x, mod, mod_row, row(norm2_w), tm=tm)
            return _mlp(h2, w1_l, w2_l, y1, mod, mod_row, tm=tm, tf=512)

        q, k, v, sgu, sk, sv = mixers(y_p, ctx_row, None, ctx_len)
        sk = sk.reshape(n_ctx, n_heads, 2, HEAD_DIM, ctx_len)
        state_k.append(jnp.transpose(sk, (0, 4, 1, 2, 3)))
        state_v.append(sv.reshape(n_ctx, ctx_len, n_heads, V_DIM))
        att = _attn_self(q, k, v, lams, row(subln_w), n_batch=n_ctx, lam_init=lam_init)
        y_p = finish(y_p, att, sgu, ctx_row)

        q, k, v, sgu = mixers(y_s, lat_row, rope_tabs, None)
        cache_kt = jnp.transpose(cache_k[:, l], (0, 2, 3, 4, 1)).reshape(n_lat, width, past)
        att = _attn_cached(q, k, v, cache_kt, cache_v[:, l].reshape(n_lat * past, width),
                           lams, row(subln_w), n_batch=n_lat, lam_init=lam_init,
                           tq=256, heads=n_heads)
        y_s = finish(y_s, att, sgu, lat_row)

    return (y_p.reshape(n_ctx, ctx_len, d), y_s.reshape(n_lat, lat_len, d),
            jnp.stack(state_k, axis=1), jnp.stack(state_v, axis=1))
```

```python
import functools
import math

import jax
import jax.numpy as jnp
import numpy as np
from jax import lax
from jax.experimental import pallas as pl
from jax.experimental.pallas import tpu as pltpu

F32 = jnp.float32
BF16 = jnp.bfloat16

HEAD_DIM = 64
V_DIM = 128
CHUNK = 128
GROUP = 128
GRID_W = 64
ROPE_BASE = 10000.0
EPS = 1e-6
N_MOD = 6
MOD_ROWS = 8
LANES = 128
MXU_DIM = 256
VMEM_LIMIT = 60 * 1024 * 1024
Q_SCALE = HEAD_DIM ** -0.5 * math.log2(math.e)


def _cparams(n_axes):
    return pltpu.CompilerParams(
        dimension_semantics=("arbitrary",) * n_axes,
        vmem_limit_bytes=VMEM_LIMIT)


def _adaln_kernel(c_ref, w_ref, b_ref, o_ref):
    c = c_ref[...]
    s = c * (1.0 / (1.0 + jnp.exp(-c)))
    o_ref[...] = jnp.dot(s.astype(BF16), w_ref[...].astype(BF16),
                         preferred_element_type=F32) + b_ref[...]


def _adaln(c_rows, w_ada, b_ada, tn=1024):
    d, n = w_ada.shape
    return pl.pallas_call(
        _adaln_kernel,
        grid=(n // tn,),
        in_specs=[pl.BlockSpec((MOD_ROWS, d), lambda j: (0, 0)),
                  pl.BlockSpec((d, tn), lambda j: (0, j)),
                  pl.BlockSpec((1, tn), lambda j: (0, j))],
        out_specs=pl.BlockSpec((MOD_ROWS, tn), lambda j: (0, j)),
        out_shape=jax.ShapeDtypeStruct((MOD_ROWS, n), F32),
        compiler_params=_cparams(1),
        name="adaln",
    )(c_rows, w_ada, b_ada.reshape(1, n))


def _gelu(x):
    return jax.nn.gelu(x, approximate=True)


def _head_norm(z, w_ref, bd_ref):
    outs = []
    for c in range(z.shape[1] // MXU_DIM):
        zc = z[:, c * MXU_DIM:(c + 1) * MXU_DIM]
        ss = jnp.dot((zc * zc).astype(BF16), bd_ref[...], preferred_element_type=F32)
        r = lax.rsqrt(ss * (1.0 / HEAD_DIM) + EPS)
        outs.append(zc * r * w_ref[:, c * MXU_DIM:(c + 1) * MXU_DIM])
    return outs


def _rope128(x, cos, sin_lo, sin_hi):
    quarter = HEAD_DIM // 4
    return (x * cos + pltpu.roll(x, LANES - quarter, 1) * sin_lo
            + pltpu.roll(x, quarter, 1) * sin_hi)


def _inproj_kernel(*refs, rope, emit_state):
    it = iter(refs)
    x_ref, sh_ref, sc_ref, n1_ref, w_ref, qw_ref, kw_ref, bd_ref = (next(it) for _ in range(8))
    if rope:
        cos_ref, slo_ref, shi_ref = (next(it) for _ in range(3))
    sgw_ref, ws_ref, bs_ref = (next(it) for _ in range(3))
    q_ref, k_ref, v_ref, sgu_ref = (next(it) for _ in range(4))
    if emit_state:
        sk_ref, sv_ref = (next(it) for _ in range(2))
    tm = x_ref.shape[0]
    width = q_ref.shape[1]

    x = x_ref[...]
    ms = jnp.mean(x * x, axis=-1, keepdims=True)
    h = (x * lax.rsqrt(ms + EPS) * (n1_ref[...] * (1.0 + sc_ref[...])) + sh_ref[...]).astype(BF16)

    def proj(j):
        return jnp.dot(h, w_ref[:, j * width:(j + 1) * width], preferred_element_type=F32)

    def qk_epilogue(z, w_norm_ref, out_ref, scale, state_ref):
        blocks = _head_norm(z, w_norm_ref, bd_ref)
        for c, n in enumerate(blocks):
            if state_ref is not None:
                seq = state_ref.shape[2]
                for s in range(tm // seq):
                    state_ref[s, c * MXU_DIM:(c + 1) * MXU_DIM, :] = n[s * seq:(s + 1) * seq].T
            for hh in range(MXU_DIM // LANES):
                xh = n[:, hh * LANES:(hh + 1) * LANES]
                if rope:
                    xh = _rope128(xh, cos_ref[...], slo_ref[...], shi_ref[...])
                if scale != 1.0:
                    xh = xh * scale
                lo = c * MXU_DIM + hh * LANES
                out_ref[:, lo:lo + LANES] = xh.astype(BF16)

    zq = proj(0)
    zk = proj(1)
    qk_epilogue(zq, qw_ref, q_ref, Q_SCALE, None)
    zv = proj(2)
    qk_epilogue(zk, kw_ref, k_ref, 1.0, sk_ref if emit_state else None)
    zu = proj(3)
    v_ref[...] = zv.astype(BF16)
    if emit_state:
        sv_ref[...] = zv
    zg = proj(4)
    gu = _gelu(zu)
    g = _gelu(zg)
    mu = jnp.mean(g, axis=-1, keepdims=True)
    gc = g - mu
    var = jnp.mean(gc * gc, axis=-1, keepdims=True)
    gn = (gc * lax.rsqrt(var + EPS) * sgw_ref[...]).astype(BF16)
    n_chunks = tm // CHUNK
    for grp in range(width // GROUP):
        cols = slice(grp * GROUP, (grp + 1) * GROUP)
        rhs = jnp.concatenate(
            [gn[c * CHUNK:(c + 1) * CHUNK, cols] for c in range(n_chunks)], axis=1)
        mixed = jnp.dot(ws_ref[grp], rhs, preferred_element_type=F32) + bs_ref[grp]
        for c in range(n_chunks):
            rows = slice(c * CHUNK, (c + 1) * CHUNK)
            sgu_ref[rows, cols] = (
                gu[rows, cols] * mixed[:, c * CHUNK:(c + 1) * CHUNK]).astype(BF16)


def _inproj(x, mod, mod_row, norm1_w, w_in, qw, kw, bd, rope_tabs, sgw, ws, bs,
            *, tm, state_seq):
    m, d = x.shape
    n_in = w_in.shape[1]
    width = n_in // 5
    rope = rope_tabs is not None
    const = lambda shape: pl.BlockSpec(shape, lambda i: (0,) * len(shape),
                                       pipeline_mode=pl.Buffered(1))
    mod_spec = lambda which: pl.BlockSpec(
        (None, None, 1, d), lambda i: (mod_row(i), which, 0, 0))
    in_specs = [
        pl.BlockSpec((tm, d), lambda i: (i, 0)),
        mod_spec(0), mod_spec(1), const((1, d)),
        const((d, n_in)),
        const((1, width)), const((1, width)),
        const((MXU_DIM, MXU_DIM)),
    ]
    args = [x, mod, mod, norm1_w, w_in, qw, kw, bd]
    if rope:
        seq_tiles = rope_tabs[0].shape[0] // tm
        tab_spec = pl.BlockSpec((tm, LANES), lambda i: (i % seq_tiles, 0))
        in_specs += [tab_spec] * 3
        args += list(rope_tabs)
    in_specs += [const((1, width)), const(ws.shape), const(bs.shape)]
    args += [sgw, ws, bs]
    out_spec = pl.BlockSpec((tm, width), lambda i: (i, 0))
    out_shape = [jax.ShapeDtypeStruct((m, width), BF16)] * 4
    out_specs = [out_spec] * 4
    emit_state = state_seq is not None
    if emit_state:
        out_shape += [jax.ShapeDtypeStruct((m // state_seq, width, state_seq), F32),
                      jax.ShapeDtypeStruct((m, width), F32)]
        out_specs += [pl.BlockSpec((tm // state_seq, width, state_seq), lambda i: (i, 0, 0)),
                      out_spec]
    return pl.pallas_call(
        functools.partial(_inproj_kernel, rope=rope, emit_state=emit_state),
        grid=(m // tm,),
        in_specs=in_specs,
        out_specs=out_specs,
        out_shape=out_shape,
        compiler_params=_cparams(1),
        name="inproj_rope" if rope else "inproj_ctx",
    )(*args)


def _lambda(lq1_ref, lk1_ref, lq2_ref, lk2_ref, lam_init):
    a = jnp.sum(lq1_ref[...] * lk1_ref[...], axis=-1, keepdims=True)
    b = jnp.sum(lq2_ref[...] * lk2_ref[...], axis=-1, keepdims=True)
    return jnp.exp(a) - jnp.exp(b) + lam_init


def _lane_fold(op, acc, tile):
    for c in range(0, tile.shape[1], LANES):
        blk = tile[:, c:c + LANES]
        acc = blk if acc is None else op(acc, blk)
    return acc


class _ScoreTiles:
    def __init__(self, q, keys):
        lane = lax.broadcasted_iota(jnp.int32, q.shape, 1)
        zero = jnp.zeros_like(q)
        self.q2 = jnp.concatenate([jnp.where(lane < HEAD_DIM, q, zero),
                                   jnp.where(lane >= HEAD_DIM, q, zero)], axis=0)
        self.slices = []
        for k, feature_major in keys:
            n_keys = k.shape[1] if feature_major else k.shape[0]
            self.slices += [(k, feature_major, lo) for lo in range(0, n_keys, MXU_DIM)]
        self.tiles = []
        self.m_lanes = None

    def __len__(self):
        return len(self.slices)

    def compute(self, t):
        k, feature_major, lo = self.slices[t]
        if feature_major:
            st = jnp.dot(self.q2, k[:, lo:lo + MXU_DIM], preferred_element_type=F32)
        else:
            st = lax.dot_general(self.q2, k[lo:lo + MXU_DIM], (((1,), (1,)), ((), ())),
                                 preferred_element_type=F32)
        self.tiles.append(st)
        self.m_lanes = _lane_fold(jnp.maximum, self.m_lanes, st)

    def row_max(self):
        return jnp.max(self.m_lanes, axis=-1, keepdims=True)


def _attn_units(q_ref, o_ref, sub, load_keys, load_values, lam, subln_w, lam_init):
    units = [(slice(r * sub, (r + 1) * sub), slice(h * V_DIM, (h + 1) * V_DIM))
             for r in range(q_ref.shape[0] // sub) for h in range(q_ref.shape[1] // V_DIM)]

    def scores(unit):
        rows, cols = unit
        tiles = _ScoreTiles(q_ref[rows, cols], load_keys(cols))
        for t in range(len(tiles)):
            tiles.compute(t)
        return tiles

    nxt = scores(units[0])
    for i, (rows, cols) in enumerate(units):
        cur, nxt = nxt, (scores(units[i + 1]) if i + 1 < len(units) else None)
        m = cur.row_max()
        e = jnp.concatenate([jnp.exp2(st - m).astype(BF16) for st in cur.tiles], axis=1)
        v = jnp.concatenate(load_values(cols), axis=0)
        v1 = jnp.concatenate([v, jnp.ones_like(v)], axis=1)
        ol = jnp.dot(e, v1, preferred_element_type=F32)
        l1, l2 = ol[:sub, V_DIM:V_DIM + 1], ol[sub:, V_DIM:V_DIM + 1]
        o = ol[:sub, :V_DIM] * (1.0 / l1) - ol[sub:, :V_DIM] * (lam / l2)
        ms = jnp.mean(o * o, axis=-1, keepdims=True)
        o_ref[rows, cols] = (o * lax.rsqrt(ms + EPS) * subln_w
                             * (1.0 - lam_init)).astype(o_ref.dtype)


def _attn_cached_kernel(q_ref, kn_ref, vn_ref, kct_ref, vc_ref, lq1, lk1, lq2, lk2, sw_ref,
                        o_ref, *, lam_init, sub):
    lam = _lambda(lq1, lk1, lq2, lk2, lam_init)
    _attn_units(
        q_ref, o_ref, sub,
        lambda cols: [(kct_ref[cols, :].astype(BF16), True), (kn_ref[:, cols], False)],
        lambda cols: [vc_ref[:, cols].astype(BF16), vn_ref[:, cols]],
        lam, sw_ref[...], lam_init)


def _attn_cached(q, k, v, cache_kt, cache_v, lams, subln_w, *, n_batch, lam_init, tq, sub,
                 heads):
    m, width = q.shape
    seq = m // n_batch
    past = cache_kt.shape[2]
    nq = seq // tq
    w = heads * V_DIM
    vec = lambda n: pl.BlockSpec((1, n), lambda b, g, t: (0, 0))
    return pl.pallas_call(
        functools.partial(_attn_cached_kernel, lam_init=lam_init, sub=sub),
        grid=(n_batch, width // w, nq),
        in_specs=[pl.BlockSpec((tq, w), lambda b, g, t: (b * nq + t, g)),
                  pl.BlockSpec((seq, w), lambda b, g, t: (b, g)),
                  pl.BlockSpec((seq, w), lambda b, g, t: (b, g)),
                  pl.BlockSpec((None, w, past), lambda b, g, t: (b, g, 0)),
                  pl.BlockSpec((past, w), lambda b, g, t: (b, g)),
                  vec(HEAD_DIM), vec(HEAD_DIM), vec(HEAD_DIM), vec(HEAD_DIM), vec(V_DIM)],
        out_specs=pl.BlockSpec((tq, w), lambda b, g, t: (b * nq + t, g)),
        out_shape=jax.ShapeDtypeStruct((m, width), BF16),
        compiler_params=_cparams(3),
        name="attn_latent",
    )(q, k, v, cache_kt, cache_v, *lams, subln_w)


def _attn_self_kernel(q_ref, k_ref, v_ref, lq1, lk1, lq2, lk2, sw_ref, o_ref, *, lam_init):
    lam = _lambda(lq1, lk1, lq2, lk2, lam_init)
    _attn_units(q_ref, o_ref, q_ref.shape[0],
                lambda cols: [(k_ref[:, cols], False)],
                lambda cols: [v_ref[:, cols]],
                lam, sw_ref[...], lam_init)


def _attn_self(q, k, v, lams, subln_w, *, n_batch, lam_init):
    m, width = q.shape
    seq = m // n_batch
    blk = pl.BlockSpec((seq, width), lambda b: (b, 0))
    vec = lambda w: pl.BlockSpec((1, w), lambda b: (0, 0))
    return pl.pallas_call(
        functools.partial(_attn_self_kernel, lam_init=lam_init),
        grid=(n_batch,),
        in_specs=[blk, blk, blk,
                  vec(HEAD_DIM), vec(HEAD_DIM), vec(HEAD_DIM), vec(HEAD_DIM), vec(V_DIM)],
        out_specs=blk,
        out_shape=jax.ShapeDtypeStruct((m, width), BF16),
        compiler_params=_cparams(1),
        name="attn_ctx",
    )(q, k, v, *lams, subln_w)


def _outproj_kernel(att_ref, sgu_ref, wo_ref, x_ref, g1_ref, sh2_ref, sc2_ref, n2_ref,
                    y_ref, h_ref):
    half = att_ref.shape[1]
    mix = jnp.dot(att_ref[...], wo_ref[:half], preferred_element_type=F32)
    mix = mix + jnp.dot(sgu_ref[...], wo_ref[half:], preferred_element_type=F32)
    y = x_ref[...] + g1_ref[...] * mix
    y_ref[...] = y
    ms = jnp.mean(y * y, axis=-1, keepdims=True)
    n = y * lax.rsqrt(ms + EPS) * n2_ref[...]
    h_ref[...] = (n * (1.0 + sc2_ref[...]) + sh2_ref[...]).astype(BF16)


def _outproj(att, sgu, w_o, x, mod, mod_row, norm2_w, *, tm):
    m, d = x.shape
    half = att.shape[1]
    mod_spec = lambda which: pl.BlockSpec(
        (None, None, 1, d), lambda i: (mod_row(i), which, 0, 0))
    return pl.pallas_call(
        _outproj_kernel,
        grid=(m // tm,),
        in_specs=[pl.BlockSpec((tm, half), lambda i: (i, 0)),
                  pl.BlockSpec((tm, half), lambda i: (i, 0)),
                  pl.BlockSpec(w_o.shape, lambda i: (0, 0)),
                  pl.BlockSpec((tm, d), lambda i: (i, 0)),
                  mod_spec(2), mod_spec(3), mod_spec(4),
                  pl.BlockSpec((1, d), lambda i: (0, 0))],
        out_specs=[pl.BlockSpec((tm, d), lambda i: (i, 0))] * 2,
        out_shape=[jax.ShapeDtypeStruct((m, d), F32), jax.ShapeDtypeStruct((m, d), BF16)],
        compiler_params=_cparams(1),
        name="outproj",
    )(att, sgu, w_o, x, mod, mod, mod, norm2_w)


def _mlp_kernel(h_ref, w1_ref, w2_ref, y1_ref, g2_ref, o_ref):
    f = pl.program_id(1)

    @pl.when(f == 0)
    def _():
        o_ref[...] = jnp.zeros_like(o_ref)

    hid = jnp.dot(h_ref[...], w1_ref[...], preferred_element_type=F32)
    hid = jnp.square(jnp.maximum(hid, 0.0)).astype(BF16)
    o_ref[...] += jnp.dot(hid, w2_ref[...], preferred_element_type=F32)

    @pl.when(f == pl.num_programs(1) - 1)
    def _():
        o_ref[...] = y1_ref[...] + g2_ref[...] * o_ref[...]


def _mlp(h, w1, w2, y1, mod, mod_row, *, tm, tf):
    m, d = h.shape
    d_ff = w1.shape[1]
    return pl.pallas_call(
        _mlp_kernel,
        grid=(m // tm, d_ff // tf),
        in_specs=[pl.BlockSpec((tm, d), lambda i, f: (i, 0)),
                  pl.BlockSpec((d, tf), lambda i, f: (0, f)),
                  pl.BlockSpec((tf, d), lambda i, f: (f, 0)),
                  pl.BlockSpec((tm, d), lambda i, f: (i, 0)),
                  pl.BlockSpec((None, None, 1, d), lambda i, f: (mod_row(i), 5, 0, 0))],
        out_specs=pl.BlockSpec((tm, d), lambda i, f: (i, 0)),
        out_shape=jax.ShapeDtypeStruct((m, d), F32),
        compiler_params=_cparams(2),
        name="mlp",
    )(h, w1, w2, y1, mod)


def _rope_tables(n):
    rows = n // GRID_W
    r, col = jnp.meshgrid(jnp.arange(rows), jnp.arange(GRID_W), indexing="ij")
    r = r.reshape(-1).astype(F32)
    col = col.reshape(-1).astype(F32)
    n_freq = HEAD_DIM // 4
    freqs = ROPE_BASE ** (-jnp.arange(n_freq, dtype=F32) / n_freq)
    ang_r = r[:, None] * freqs
    ang_c = col[:, None] * freqs
    ang = jnp.concatenate([ang_r, ang_r, ang_c, ang_c], axis=-1)
    cos = jnp.tile(jnp.cos(ang), (1, LANES // HEAD_DIM))
    sin = jnp.tile(jnp.sin(ang), (1, LANES // HEAD_DIM))
    first = (jnp.arange(LANES) % (2 * n_freq)) < n_freq
    return cos, jnp.where(first, -sin, 0.0), jnp.where(first, 0.0, sin)


def _block_diag_ones(n, blk):
    idx = np.arange(n) // blk
    return jnp.asarray(idx[:, None] == idx[None, :], dtype=BF16)


def kernel(x_prompt, x_sample, cache_k, cache_v, c, c_ctx, w_ada, b_ada, norm1_w, norm2_w,
           w_in, q_norm_w, k_norm_w, lambda_q1, lambda_k1, lambda_q2, lambda_k2, subln_w,
           sgu_norm_w, w_s, b_s, w_o, w_ff1, w_ff2):
    n_ctx, ctx_len, d = x_prompt.shape
    n_lat, lat_len, _ = x_sample.shape
    depth = w_in.shape[0]
    past = cache_k.shape[2]
    width = w_o.shape[1] // 2
    n_heads = width // V_DIM
    tm = 512
    assert n_lat + 1 <= MOD_ROWS and lat_len % tm == 0 and ctx_len % CHUNK == 0

    y_p = x_prompt.reshape(n_ctx * ctx_len, d)
    y_s = x_sample.reshape(n_lat * lat_len, d)
    rope_tabs = _rope_tables(lat_len)
    bd = _block_diag_ones(MXU_DIM, HEAD_DIM)
    c_rows = jnp.concatenate(
        [c_ctx[None, :], c, jnp.zeros((MOD_ROWS - 1 - n_lat, d), F32)], axis=0)
    lat_tiles = lat_len // tm
    ctx_row = lambda i: 0
    lat_row = lambda i: 1 + i // lat_tiles

    state_k, state_v = [], []
    for l in range(depth):
        lam_init = 0.8 - 0.6 * math.exp(-0.3 * l)
        w_in_l, w_o_l = w_in[l].astype(BF16), w_o[l].astype(BF16)
        w1_l, w2_l = w_ff1[l].astype(BF16), w_ff2[l].astype(BF16)
        ws_l = w_s[l].astype(BF16)
        bs_l = b_s[l][:, :, None]
        row = lambda a: a[l].reshape(1, -1)
        qw = jnp.tile(row(q_norm_w), (1, width // HEAD_DIM))
        kw = jnp.tile(row(k_norm_w), (1, width // HEAD_DIM))
        lams = (row(lambda_q1), row(lambda_k1), row(lambda_q2), row(lambda_k2))

        mod = _adaln(c_rows, w_ada[l], b_ada[l]).reshape(MOD_ROWS, N_MOD, 1, d)

        def mixers(x, mod_row, tabs, state_seq):
            return _inproj(x, mod, mod_row, row(norm1_w), w_in_l, qw, kw, bd, tabs,
                           row(sgu_norm_w), ws_l, bs_l, tm=tm, state_seq=state_seq)

        def finish(x, att, sgu, mod_row):
            y1, h2 = _outproj(att, sgu, w_o_l, x, mod, mod_row, row(norm2_w), tm=tm)
            return _mlp(h2, w1_l, w2_l, y1, mod, mod_row, tm=tm, tf=1024)

        q, k, v, sgu, sk, sv = mixers(y_p, ctx_row, None, ctx_len)
        sk = sk.reshape(n_ctx, n_heads, 2, HEAD_DIM, ctx_len)
        state_k.append(jnp.transpose(sk, (0, 4, 1, 2, 3)))
        state_v.append(sv.reshape(n_ctx, ctx_len, n_heads, V_DIM))
        att = _attn_self(q, k, v, lams, row(subln_w), n_batch=n_ctx, lam_init=lam_init)
        y_p = finish(y_p, att, sgu, ctx_row)

        q, k, v, sgu = mixers(y_s, lat_row, rope_tabs, None)
        cache_kt = jnp.transpose(cache_k[:, l], (0, 2, 3, 4, 1)).reshape(n_lat, width, past)
        att = _attn_cached(q, k, v, cache_kt, cache_v[:, l].reshape(n_lat * past, width),
                           lams, row(subln_w), n_batch=n_lat, lam_init=lam_init,
                           tq=512, sub=256, heads=n_heads)
        y_s = finish(y_s, att, sgu, lat_row)

    return (y_p.reshape(n_ctx, ctx_len, d), y_s.reshape(n_lat, lat_len, d),
            jnp.stack(state_k, axis=1), jnp.stack(state_v, axis=1))
```

```python
import functools
import math
from typing import NamedTuple

import jax
import jax.numpy as jnp
import numpy as np
from jax import lax
from jax.experimental import pallas as pl
from jax.experimental.pallas import tpu as pltpu

F32 = jnp.float32
BF16 = jnp.bfloat16

HEAD_DIM = 64
V_DIM = 128
CHUNK = 128
GROUP = 128
GRID_W = 64
ROPE_BASE = 10000.0
EPS = 1e-6
N_MOD = 6
MOD_ROWS = 8
LANES = 128
MXU_DIM = 256
VMEM_LIMIT = 60 * 1024 * 1024
Q_SCALE = HEAD_DIM ** -0.5 * math.log2(math.e)


class _Tiles(NamedTuple):
    tm: int
    tm_mlp: int
    tf: int
    tq: int
    tq_sub: int


_TILES = _Tiles(tm=512, tm_mlp=1024, tf=512, tq=512, tq_sub=256)


def _cparams(n_axes):
    return pltpu.CompilerParams(
        dimension_semantics=("arbitrary",) * n_axes,
        vmem_limit_bytes=VMEM_LIMIT)


def _adaln_kernel(c_ref, w_ref, b_ref, o_ref):
    c = c_ref[...]
    s = c * (1.0 / (1.0 + jnp.exp(-c)))
    o_ref[...] = jnp.dot(s.astype(BF16), w_ref[...].astype(BF16),
                         preferred_element_type=F32) + b_ref[...]


def _adaln(c_rows, w_ada, b_ada, tn=1024):
    d, n = w_ada.shape
    return pl.pallas_call(
        _adaln_kernel,
        grid=(n // tn,),
        in_specs=[pl.BlockSpec((MOD_ROWS, d), lambda j: (0, 0)),
                  pl.BlockSpec((d, tn), lambda j: (0, j)),
                  pl.BlockSpec((1, tn), lambda j: (0, j))],
        out_specs=pl.BlockSpec((MOD_ROWS, tn), lambda j: (0, j)),
        out_shape=jax.ShapeDtypeStruct((MOD_ROWS, n), F32),
        compiler_params=_cparams(1),
        name="adaln",
    )(c_rows, w_ada, b_ada.reshape(1, n))


def _gelu(x):
    return jax.nn.gelu(x, approximate=True)


def _head_norm(z, w_ref, bd_ref):
    outs = []
    for c in range(z.shape[1] // MXU_DIM):
        zc = z[:, c * MXU_DIM:(c + 1) * MXU_DIM]
        ss = jnp.dot((zc * zc).astype(BF16), bd_ref[...], preferred_element_type=F32)
        r = lax.rsqrt(ss * (1.0 / HEAD_DIM) + EPS)
        outs.append(zc * r * w_ref[:, c * MXU_DIM:(c + 1) * MXU_DIM])
    return outs


def _rope128(x, cos, sin_lo, sin_hi):
    quarter = HEAD_DIM // 4
    return (x * cos + pltpu.roll(x, LANES - quarter, 1) * sin_lo
            + pltpu.roll(x, quarter, 1) * sin_hi)


def _inproj_kernel(*refs, rope, emit_state):
    it = iter(refs)
    x_ref, sh_ref, sc_ref, n1_ref, w_ref, qw_ref, kw_ref, bd_ref = (next(it) for _ in range(8))
    if rope:
        cos_ref, slo_ref, shi_ref = (next(it) for _ in range(3))
    sgw_ref, ws_ref, bs_ref = (next(it) for _ in range(3))
    q_ref, k_ref, v_ref, sgu_ref = (next(it) for _ in range(4))
    if emit_state:
        sk_ref, sv_ref = (next(it) for _ in range(2))
    tm = x_ref.shape[0]
    width = q_ref.shape[1]

    x = x_ref[...]
    ms = jnp.mean(x * x, axis=-1, keepdims=True)
    h = (x * lax.rsqrt(ms + EPS) * (n1_ref[...] * (1.0 + sc_ref[...])) + sh_ref[...]).astype(BF16)

    def proj(j):
        return jnp.dot(h, w_ref[:, j * width:(j + 1) * width], preferred_element_type=F32)

    def qk_epilogue(z, w_norm_ref, out_ref, scale, state_ref):
        blocks = _head_norm(z, w_norm_ref, bd_ref)
        for c, n in enumerate(blocks):
            if state_ref is not None:
                seq = state_ref.shape[2]
                for s in range(tm // seq):
                    state_ref[s, c * MXU_DIM:(c + 1) * MXU_DIM, :] = n[s * seq:(s + 1) * seq].T
            for hh in range(MXU_DIM // LANES):
                xh = n[:, hh * LANES:(hh + 1) * LANES]
                if rope:
                    xh = _rope128(xh, cos_ref[...], slo_ref[...], shi_ref[...])
                if scale != 1.0:
                    xh = xh * scale
                lo = c * MXU_DIM + hh * LANES
                out_ref[:, lo:lo + LANES] = xh.astype(BF16)

    zq = proj(0)
    zk = proj(1)
    qk_epilogue(zq, qw_ref, q_ref, Q_SCALE, None)
    zv = proj(2)
    qk_epilogue(zk, kw_ref, k_ref, 1.0, sk_ref if emit_state else None)
    zu = proj(3)
    v_ref[...] = zv.astype(BF16)
    if emit_state:
        sv_ref[...] = zv
    zg = proj(4)
    gu = _gelu(zu)
    g = _gelu(zg)
    mu = jnp.mean(g, axis=-1, keepdims=True)
    gc = g - mu
    var = jnp.mean(gc * gc, axis=-1, keepdims=True)
    gn = (gc * lax.rsqrt(var + EPS) * sgw_ref[...]).astype(BF16)
    n_chunks = tm // CHUNK
    for grp in range(width // GROUP):
        cols = slice(grp * GROUP, (grp + 1) * GROUP)
        rhs = jnp.concatenate(
            [gn[c * CHUNK:(c + 1) * CHUNK, cols] for c in range(n_chunks)], axis=1)
        mixed = jnp.dot(ws_ref[grp], rhs, preferred_element_type=F32) + bs_ref[grp]
        for c in range(n_chunks):
            rows = slice(c * CHUNK, (c + 1) * CHUNK)
            sgu_ref[rows, cols] = (
                gu[rows, cols] * mixed[:, c * CHUNK:(c + 1) * CHUNK]).astype(BF16)


def _inproj(x, mod, mod_row, norm1_w, w_in, qw, kw, bd, rope_tabs, sgw, ws, bs,
            *, tm, state_seq):
    m, d = x.shape
    n_in = w_in.shape[1]
    width = n_in // 5
    rope = rope_tabs is not None
    const = lambda shape: pl.BlockSpec(shape, lambda i: (0,) * len(shape),
                                       pipeline_mode=pl.Buffered(1))
    mod_spec = lambda which: pl.BlockSpec(
        (None, None, 1, d), lambda i: (mod_row(i), which, 0, 0))
    in_specs = [
        pl.BlockSpec((tm, d), lambda i: (i, 0)),
        mod_spec(0), mod_spec(1), const((1, d)),
        const((d, n_in)),
        const((1, width)), const((1, width)),
        const((MXU_DIM, MXU_DIM)),
    ]
    args = [x, mod, mod, norm1_w, w_in, qw, kw, bd]
    if rope:
        seq_tiles = rope_tabs[0].shape[0] // tm
        tab_spec = pl.BlockSpec((tm, LANES), lambda i: (i % seq_tiles, 0))
        in_specs += [tab_spec] * 3
        args += list(rope_tabs)
    in_specs += [const((1, width)), const(ws.shape), const(bs.shape)]
    args += [sgw, ws, bs]
    out_spec = pl.BlockSpec((tm, width), lambda i: (i, 0))
    out_shape = [jax.ShapeDtypeStruct((m, width), BF16)] * 4
    out_specs = [out_spec] * 4
    emit_state = state_seq is not None
    if emit_state:
        out_shape += [jax.ShapeDtypeStruct((m // state_seq, width, state_seq), F32),
                      jax.ShapeDtypeStruct((m, width), F32)]
        out_specs += [pl.BlockSpec((tm // state_seq, width, state_seq), lambda i: (i, 0, 0)),
                      out_spec]
    return pl.pallas_call(
        functools.partial(_inproj_kernel, rope=rope, emit_state=emit_state),
        grid=(m // tm,),
        in_specs=in_specs,
        out_specs=out_specs,
        out_shape=out_shape,
        compiler_params=_cparams(1),
        name="inproj_rope" if rope else "inproj_ctx",
    )(*args)


def _lambda(lq1_ref, lk1_ref, lq2_ref, lk2_ref, lam_init):
    a = jnp.sum(lq1_ref[...] * lk1_ref[...], axis=-1, keepdims=True)
    b = jnp.sum(lq2_ref[...] * lk2_ref[...], axis=-1, keepdims=True)
    return jnp.exp(a) - jnp.exp(b) + lam_init


def _lane_fold(op, acc, tile):
    for c in range(0, tile.shape[1], LANES):
        blk = tile[:, c:c + LANES]
        acc = blk if acc is None else op(acc, blk)
    return acc


class _ScoreTiles:
    def __init__(self, q, keys):
        lane = lax.broadcasted_iota(jnp.int32, q.shape, 1)
        zero = jnp.zeros_like(q)
        self.q2 = jnp.concatenate([jnp.where(lane < HEAD_DIM, q, zero),
                                   jnp.where(lane >= HEAD_DIM, q, zero)], axis=0)
        self.slices = []
        for k, feature_major in keys:
            n_keys = k.shape[1] if feature_major else k.shape[0]
            self.slices += [(k, feature_major, lo) for lo in range(0, n_keys, MXU_DIM)]
        self.tiles = []
        self.m_lanes = None

    def __len__(self):
        return len(self.slices)

    def compute(self, t):
        k, feature_major, lo = self.slices[t]
        if feature_major:
            st = jnp.dot(self.q2, k[:, lo:lo + MXU_DIM], preferred_element_type=F32)
        else:
            st = lax.dot_general(self.q2, k[lo:lo + MXU_DIM], (((1,), (1,)), ((), ())),
                                 preferred_element_type=F32)
        self.tiles.append(st)
        self.m_lanes = _lane_fold(jnp.maximum, self.m_lanes, st)

    def row_max(self):
        return jnp.max(self.m_lanes, axis=-1, keepdims=True)


def _attn_units(q_ref, o_ref, sub, lockstep, load_keys, load_values, lam, subln_w, lam_init):
    units = [(slice(r * sub, (r + 1) * sub), slice(h * V_DIM, (h + 1) * V_DIM))
             for r in range(q_ref.shape[0] // sub) for h in range(q_ref.shape[1] // V_DIM)]
    groups = [units[i:i + lockstep] for i in range(0, len(units), lockstep)]

    def scores(group):
        out = []
        for rows, cols in group:
            tiles = _ScoreTiles(q_ref[rows, cols], load_keys(cols))
            for t in range(len(tiles)):
                tiles.compute(t)
            out.append(tiles)
        return out

    nxt = scores(groups[0])
    for i, group in enumerate(groups):
        cur, nxt = nxt, (scores(groups[i + 1]) if i + 1 < len(groups) else None)
        ms = [tiles.row_max() for tiles in cur]
        es = [jnp.concatenate([jnp.exp2(st - m).astype(BF16) for st in tiles.tiles], axis=1)
              for tiles, m in zip(cur, ms)]
        ols = []
        for e, (rows, cols) in zip(es, group):
            v = jnp.concatenate(load_values(cols), axis=0)
            v1 = jnp.concatenate([v, jnp.ones_like(v)], axis=1)
            ols.append(jnp.dot(e, v1, preferred_element_type=F32))
        outs = []
        for ol in ols:
            l1, l2 = ol[:sub, V_DIM:V_DIM + 1], ol[sub:, V_DIM:V_DIM + 1]
            outs.append(ol[:sub, :V_DIM] * (1.0 / l1) - ol[sub:, :V_DIM] * (lam / l2))
        sq = [jnp.mean(o * o, axis=-1, keepdims=True) for o in outs]
        for o, s2, (rows, cols) in zip(outs, sq, group):
            o_ref[rows, cols] = (o * lax.rsqrt(s2 + EPS) * subln_w
                                 * (1.0 - lam_init)).astype(o_ref.dtype)


def _attn_cached_kernel(q_ref, kn_ref, vn_ref, kct_ref, vc_ref, lq1, lk1, lq2, lk2, sw_ref,
                        o_ref, *, lam_init, sub):
    lam = _lambda(lq1, lk1, lq2, lk2, lam_init)
    _attn_units(
        q_ref, o_ref, sub, 1,
        lambda cols: [(kct_ref[cols, :].astype(BF16), True), (kn_ref[:, cols], False)],
        lambda cols: [vc_ref[:, cols].astype(BF16), vn_ref[:, cols]],
        lam, sw_ref[...], lam_init)


def _attn_cached(q, k, v, cache_kt, cache_v, lams, subln_w, *, n_batch, lam_init, tq, sub,
                 heads):
    m, width = q.shape
    seq = m // n_batch
    past = cache_kt.shape[2]
    nq = seq // tq
    w = heads * V_DIM
    vec = lambda n: pl.BlockSpec((1, n), lambda b, g, t: (0, 0))
    return pl.pallas_call(
        functools.partial(_attn_cached_kernel, lam_init=lam_init, sub=sub),
        grid=(n_batch, width // w, nq),
        in_specs=[pl.BlockSpec((tq, w), lambda b, g, t: (b * nq + t, g)),
                  pl.BlockSpec((seq, w), lambda b, g, t: (b, g)),
                  pl.BlockSpec((seq, w), lambda b, g, t: (b, g)),
                  pl.BlockSpec((None, w, past), lambda b, g, t: (b, g, 0)),
                  pl.BlockSpec((past, w), lambda b, g, t: (b, g)),
                  vec(HEAD_DIM), vec(HEAD_DIM), vec(HEAD_DIM), vec(HEAD_DIM), vec(V_DIM)],
        out_specs=pl.BlockSpec((tq, w), lambda b, g, t: (b * nq + t, g)),
        out_shape=jax.ShapeDtypeStruct((m, width), BF16),
        compiler_params=_cparams(3),
        name="attn_latent",
    )(q, k, v, cache_kt, cache_v, *lams, subln_w)


def _attn_self_kernel(q_ref, k_ref, v_ref, lq1, lk1, lq2, lk2, sw_ref, o_ref, *, lam_init):
    lam = _lambda(lq1, lk1, lq2, lk2, lam_init)
    _attn_units(q_ref, o_ref, q_ref.shape[0], q_ref.shape[1] // V_DIM,
                lambda cols: [(k_ref[:, cols], False)],
                lambda cols: [v_ref[:, cols]],
                lam, sw_ref[...], lam_init)


def _attn_self(q, k, v, lams, subln_w, *, n_batch, lam_init):
    m, width = q.shape
    seq = m // n_batch
    blk = pl.BlockSpec((seq, width), lambda b: (b, 0))
    vec = lambda w: pl.BlockSpec((1, w), lambda b: (0, 0))
    return pl.pallas_call(
        functools.partial(_attn_self_kernel, lam_init=lam_init),
        grid=(n_batch,),
        in_specs=[blk, blk, blk,
                  vec(HEAD_DIM), vec(HEAD_DIM), vec(HEAD_DIM), vec(HEAD_DIM), vec(V_DIM)],
        out_specs=blk,
        out_shape=jax.ShapeDtypeStruct((m, width), BF16),
        compiler_params=_cparams(1),
        name="attn_ctx",
    )(q, k, v, *lams, subln_w)


def _outproj_kernel(att_ref, sgu_ref, wo_ref, x_ref, g1_ref, sh2_ref, sc2_ref, n2_ref,
                    y_ref, h_ref):
    half = att_ref.shape[1]
    mix = jnp.dot(att_ref[...], wo_ref[:half], preferred_element_type=F32)
    mix = mix + jnp.dot(sgu_ref[...], wo_ref[half:], preferred_element_type=F32)
    y = x_ref[...] + g1_ref[...] * mix
    y_ref[...] = y
    ms = jnp.mean(y * y, axis=-1, keepdims=True)
    n = y * lax.rsqrt(ms + EPS) * n2_ref[...]
    h_ref[...] = (n * (1.0 + sc2_ref[...]) + sh2_ref[...]).astype(BF16)


def _outproj(att, sgu, w_o, x, mod, mod_row, norm2_w, *, tm):
    m, d = x.shape
    half = att.shape[1]
    mod_spec = lambda which: pl.BlockSpec(
        (None, None, 1, d), lambda i: (mod_row(i), which, 0, 0))
    return pl.pallas_call(
        _outproj_kernel,
        grid=(m // tm,),
        in_specs=[pl.BlockSpec((tm, half), lambda i: (i, 0)),
                  pl.BlockSpec((tm, half), lambda i: (i, 0)),
                  pl.BlockSpec(w_o.shape, lambda i: (0, 0)),
                  pl.BlockSpec((tm, d), lambda i: (i, 0)),
                  mod_spec(2), mod_spec(3), mod_spec(4),
                  pl.BlockSpec((1, d), lambda i: (0, 0))],
        out_specs=[pl.BlockSpec((tm, d), lambda i: (i, 0))] * 2,
        out_shape=[jax.ShapeDtypeStruct((m, d), F32), jax.ShapeDtypeStruct((m, d), BF16)],
        compiler_params=_cparams(1),
        name="outproj",
    )(att, sgu, w_o, x, mod, mod, mod, norm2_w)


def _mlp_kernel(h_ref, w1_ref, w2_ref, y1_ref, g2_ref, o_ref):
    f = pl.program_id(1)

    @pl.when(f == 0)
    def _():
        o_ref[...] = jnp.zeros_like(o_ref)

    hid = jnp.dot(h_ref[...], w1_ref[...], preferred_element_type=F32)
    hid = jnp.square(jnp.maximum(hid, 0.0)).astype(BF16)
    o_ref[...] += jnp.dot(hid, w2_ref[...], preferred_element_type=F32)

    @pl.when(f == pl.num_programs(1) - 1)
    def _():
        o_ref[...] = y1_ref[...] + g2_ref[...] * o_ref[...]


def _mlp(h, w1, w2, y1, mod, mod_row, *, tm, tf):
    m, d = h.shape
    d_ff = w1.shape[1]
    return pl.pallas_call(
        _mlp_kernel,
        grid=(m // tm, d_ff // tf),
        in_specs=[pl.BlockSpec((tm, d), lambda i, f: (i, 0)),
                  pl.BlockSpec((d, tf), lambda i, f: (0, f)),
                  pl.BlockSpec((tf, d), lambda i, f: (f, 0)),
                  pl.BlockSpec((tm, d), lambda i, f: (i, 0)),
                  pl.BlockSpec((None, None, 1, d), lambda i, f: (mod_row(i), 5, 0, 0))],
        out_specs=pl.BlockSpec((tm, d), lambda i, f: (i, 0)),
        out_shape=jax.ShapeDtypeStruct((m, d), F32),
        compiler_params=_cparams(2),
        name="mlp",
    )(h, w1, w2, y1, mod)


def _rope_tables(n):
    rows = n // GRID_W
    r, col = jnp.meshgrid(jnp.arange(rows), jnp.arange(GRID_W), indexing="ij")
    r = r.reshape(-1).astype(F32)
    col = col.reshape(-1).astype(F32)
    n_freq = HEAD_DIM // 4
    freqs = ROPE_BASE ** (-jnp.arange(n_freq, dtype=F32) / n_freq)
    ang_r = r[:, None] * freqs
    ang_c = col[:, None] * freqs
    ang = jnp.concatenate([ang_r, ang_r, ang_c, ang_c], axis=-1)
    cos = jnp.tile(jnp.cos(ang), (1, LANES // HEAD_DIM))
    sin = jnp.tile(jnp.sin(ang), (1, LANES // HEAD_DIM))
    first = (jnp.arange(LANES) % (2 * n_freq)) < n_freq
    return cos, jnp.where(first, -sin, 0.0), jnp.where(first, 0.0, sin)


def _block_diag_ones(n, blk):
    idx = np.arange(n) // blk
    return jnp.asarray(idx[:, None] == idx[None, :], dtype=BF16)


def kernel(x_prompt, x_sample, cache_k, cache_v, c, c_ctx, w_ada, b_ada, norm1_w, norm2_w,
           w_in, q_norm_w, k_norm_w, lambda_q1, lambda_k1, lambda_q2, lambda_k2, subln_w,
           sgu_norm_w, w_s, b_s, w_o, w_ff1, w_ff2):
    n_ctx, ctx_len, d = x_prompt.shape
    n_lat, lat_len, _ = x_sample.shape
    depth = w_in.shape[0]
    past = cache_k.shape[2]
    width = w_o.shape[1] // 2
    n_heads = width // V_DIM
    tiles = _TILES
    assert n_lat + 1 <= MOD_ROWS and ctx_len % CHUNK == 0
    assert lat_len % tiles.tm == 0 and lat_len % tiles.tm_mlp == 0

    y_p = x_prompt.reshape(n_ctx * ctx_len, d)
    y_s = x_sample.reshape(n_lat * lat_len, d)
    rope_tabs = _rope_tables(lat_len)
    bd = _block_diag_ones(MXU_DIM, HEAD_DIM)
    c_rows = jnp.concatenate(
        [c_ctx[None, :], c, jnp.zeros((MOD_ROWS - 1 - n_lat, d), F32)], axis=0)
    ctx_row = lambda tm: (lambda i: 0)
    lat_row = lambda tm: (lambda i: 1 + i // (lat_len // tm))

    state_k, state_v = [], []
    for l in range(depth):
        lam_init = 0.8 - 0.6 * math.exp(-0.3 * l)
        w_in_l, w_o_l = w_in[l].astype(BF16), w_o[l].astype(BF16)
        w1_l, w2_l = w_ff1[l].astype(BF16), w_ff2[l].astype(BF16)
        ws_l = w_s[l].astype(BF16)
        bs_l = b_s[l][:, :, None]
        row = lambda a: a[l].reshape(1, -1)
        qw = jnp.tile(row(q_norm_w), (1, width // HEAD_DIM))
        kw = jnp.tile(row(k_norm_w), (1, width // HEAD_DIM))
        lams = (row(lambda_q1), row(lambda_k1), row(lambda_q2), row(lambda_k2))

        mod = _adaln(c_rows, w_ada[l], b_ada[l]).reshape(MOD_ROWS, N_MOD, 1, d)

        def mixers(x, mod_row, tabs, state_seq):
            return _inproj(x, mod, mod_row(tiles.tm), row(norm1_w), w_in_l, qw, kw, bd, tabs,
                           row(sgu_norm_w), ws_l, bs_l, tm=tiles.tm, state_seq=state_seq)

        def finish(x, att, sgu, mod_row):
            y1, h2 = _outproj(att, sgu, w_o_l, x, mod, mod_row(tiles.tm), row(norm2_w),
                              tm=tiles.tm)
            return _mlp(h2, w1_l, w2_l, y1, mod, mod_row(tiles.tm_mlp),
                        tm=tiles.tm_mlp, tf=tiles.tf)

        q, k, v, sgu, sk, sv = mixers(y_p, ctx_row, None, ctx_len)
        sk = sk.reshape(n_ctx, n_heads, 2, HEAD_DIM, ctx_len)
        state_k.append(jnp.transpose(sk, (0, 4, 1, 2, 3)))
        state_v.append(sv.reshape(n_ctx, ctx_len, n_heads, V_DIM))
        att = _attn_self(q, k, v, lams, row(subln_w), n_batch=n_ctx, lam_init=lam_init)
        y_p = finish(y_p, att, sgu, ctx_row)

        q, k, v, sgu = mixers(y_s, lat_row, rope_tabs, None)
        cache_kt = jnp.transpose(cache_k[:, l], (0, 2, 3, 4, 1)).reshape(n_lat, width, past)
        att = _attn_cached(q, k, v, cache_kt, cache_v[:, l].reshape(n_lat * past, width),
                           lams, row(subln_w), n_batch=n_lat, lam_init=lam_init,
                           tq=tiles.tq, sub=tiles.tq_sub, heads=n_heads)
        y_s = finish(y_s, att, sgu, lat_row)

    return (y_p.reshape(n_ctx, ctx_len, d), y_s.reshape(n_lat, lat_len, d),
            jnp.stack(state_k, axis=1), jnp.stack(state_v, axis=1))
```

```python
import functools
import math
from typing import NamedTuple

import jax
import jax.numpy as jnp
import numpy as np
from jax import lax
from jax.experimental import pallas as pl
from jax.experimental.pallas import tpu as pltpu

F32 = jnp.float32
BF16 = jnp.bfloat16

HEAD_DIM = 64
V_DIM = 128
CHUNK = 128
GROUP = 128
GRID_W = 64
ROPE_BASE = 10000.0
EPS = 1e-6
N_MOD = 6
MOD_ROWS = 8
LANES = 128
MXU_DIM = 256
VMEM_LIMIT = 60 * 1024 * 1024
Q_SCALE = HEAD_DIM ** -0.5 * math.log2(math.e)


class _Tiles(NamedTuple):
    tm: int
    tm_mlp: int
    tf: int
    tq: int
    tq_sub: int


_TILES = _Tiles(tm=512, tm_mlp=512, tf=1024, tq=512, tq_sub=256)


def _cparams(n_axes):
    return pltpu.CompilerParams(
        dimension_semantics=("arbitrary",) * n_axes,
        vmem_limit_bytes=VMEM_LIMIT)


def _adaln_kernel(c_ref, w_ref, b_ref, o_ref):
    c = c_ref[...]
    s = c * (1.0 / (1.0 + jnp.exp(-c)))
    o_ref[...] = jnp.dot(s.astype(BF16), w_ref[...].astype(BF16),
                         preferred_element_type=F32) + b_ref[...]


def _adaln(c_rows, w_ada, b_ada, tn=1024):
    d, n = w_ada.shape
    return pl.pallas_call(
        _adaln_kernel,
        grid=(n // tn,),
        in_specs=[pl.BlockSpec((MOD_ROWS, d), lambda j: (0, 0)),
                  pl.BlockSpec((d, tn), lambda j: (0, j)),
                  pl.BlockSpec((1, tn), lambda j: (0, j))],
        out_specs=pl.BlockSpec((MOD_ROWS, tn), lambda j: (0, j)),
        out_shape=jax.ShapeDtypeStruct((MOD_ROWS, n), F32),
        compiler_params=_cparams(1),
        name="adaln",
    )(c_rows, w_ada, b_ada.reshape(1, n))


def _gelu(x):
    return jax.nn.gelu(x, approximate=True)


def _head_norm(z, w_ref, bd_ref):
    outs = []
    for c in range(z.shape[1] // MXU_DIM):
        zc = z[:, c * MXU_DIM:(c + 1) * MXU_DIM]
        ss = jnp.dot((zc * zc).astype(BF16), bd_ref[...], preferred_element_type=F32)
        r = lax.rsqrt(ss * (1.0 / HEAD_DIM) + EPS)
        outs.append(zc * r * w_ref[:, c * MXU_DIM:(c + 1) * MXU_DIM])
    return outs


def _rope128(x, cos, sin_lo, sin_hi):
    quarter = HEAD_DIM // 4
    return (x * cos + pltpu.roll(x, LANES - quarter, 1) * sin_lo
            + pltpu.roll(x, quarter, 1) * sin_hi)


def _inproj_kernel(*refs, rope, emit_state):
    it = iter(refs)
    x_ref, sh_ref, sc_ref, n1_ref, w_ref, qw_ref, kw_ref, bd_ref = (next(it) for _ in range(8))
    if rope:
        cos_ref, slo_ref, shi_ref = (next(it) for _ in range(3))
    sgw_ref, ws_ref, bs_ref = (next(it) for _ in range(3))
    q_ref, k_ref, v_ref, sgu_ref = (next(it) for _ in range(4))
    if emit_state:
        sk_ref, sv_ref = (next(it) for _ in range(2))
    tm = x_ref.shape[0]
    width = q_ref.shape[1]

    x = x_ref[...]
    ms = jnp.mean(x * x, axis=-1, keepdims=True)
    h = (x * lax.rsqrt(ms + EPS) * (n1_ref[...] * (1.0 + sc_ref[...])) + sh_ref[...]).astype(BF16)

    def proj(j):
        return jnp.dot(h, w_ref[:, j * width:(j + 1) * width], preferred_element_type=F32)

    def qk_epilogue(z, w_norm_ref, out_ref, scale, state_ref):
        blocks = _head_norm(z, w_norm_ref, bd_ref)
        for c, n in enumerate(blocks):
            if state_ref is not None:
                seq = state_ref.shape[2]
                for s in range(tm // seq):
                    state_ref[s, c * MXU_DIM:(c + 1) * MXU_DIM, :] = n[s * seq:(s + 1) * seq].T
            for hh in range(MXU_DIM // LANES):
                xh = n[:, hh * LANES:(hh + 1) * LANES]
                if rope:
                    xh = _rope128(xh, cos_ref[...], slo_ref[...], shi_ref[...])
                if scale != 1.0:
                    xh = xh * scale
                lo = c * MXU_DIM + hh * LANES
                out_ref[:, lo:lo + LANES] = xh.astype(BF16)

    zq = proj(0)
    zk = proj(1)
    qk_epilogue(zq, qw_ref, q_ref, Q_SCALE, None)
    zv = proj(2)
    qk_epilogue(zk, kw_ref, k_ref, 1.0, sk_ref if emit_state else None)
    zu = proj(3)
    v_ref[...] = zv.astype(BF16)
    if emit_state:
        sv_ref[...] = zv
    zg = proj(4)
    gu = _gelu(zu)
    g = _gelu(zg)
    mu = jnp.mean(g, axis=-1, keepdims=True)
    gc = g - mu
    var = jnp.mean(gc * gc, axis=-1, keepdims=True)
    gn = (gc * lax.rsqrt(var + EPS) * sgw_ref[...]).astype(BF16)
    n_chunks = tm // CHUNK
    for grp in range(width // GROUP):
        cols = slice(grp * GROUP, (grp + 1) * GROUP)
        rhs = jnp.concatenate(
            [gn[c * CHUNK:(c + 1) * CHUNK, cols] for c in range(n_chunks)], axis=1)
        mixed = jnp.dot(ws_ref[grp], rhs, preferred_element_type=F32) + bs_ref[grp]
        for c in range(n_chunks):
            rows = slice(c * CHUNK, (c + 1) * CHUNK)
            sgu_ref[rows, cols] = (
                gu[rows, cols] * mixed[:, c * CHUNK:(c + 1) * CHUNK]).astype(BF16)


def _inproj(x, mod, mod_row, norm1_w, w_in, qw, kw, bd, rope_tabs, sgw, ws, bs,
            *, tm, state_seq):
    m, d = x.shape
    n_in = w_in.shape[1]
    width = n_in // 5
    rope = rope_tabs is not None
    const = lambda shape: pl.BlockSpec(shape, lambda i: (0,) * len(shape),
                                       pipeline_mode=pl.Buffered(1))
    mod_spec = lambda which: pl.BlockSpec(
        (None, None, 1, d), lambda i: (mod_row(i), which, 0, 0))
    in_specs = [
        pl.BlockSpec((tm, d), lambda i: (i, 0)),
        mod_spec(0), mod_spec(1), const((1, d)),
        const((d, n_in)),
        const((1, width)), const((1, width)),
        const((MXU_DIM, MXU_DIM)),
    ]
    args = [x, mod, mod, norm1_w, w_in, qw, kw, bd]
    if rope:
        seq_tiles = rope_tabs[0].shape[0] // tm
        tab_spec = pl.BlockSpec((tm, LANES), lambda i: (i % seq_tiles, 0))
        in_specs += [tab_spec] * 3
        args += list(rope_tabs)
    in_specs += [const((1, width)), const(ws.shape), const(bs.shape)]
    args += [sgw, ws, bs]
    out_spec = pl.BlockSpec((tm, width), lambda i: (i, 0))
    out_shape = [jax.ShapeDtypeStruct((m, width), BF16)] * 4
    out_specs = [out_spec] * 4
    emit_state = state_seq is not None
    if emit_state:
        out_shape += [jax.ShapeDtypeStruct((m // state_seq, width, state_seq), F32),
                      jax.ShapeDtypeStruct((m, width), F32)]
        out_specs += [pl.BlockSpec((tm // state_seq, width, state_seq), lambda i: (i, 0, 0)),
                      out_spec]
    return pl.pallas_call(
        functools.partial(_inproj_kernel, rope=rope, emit_state=emit_state),
        grid=(m // tm,),
        in_specs=in_specs,
        out_specs=out_specs,
        out_shape=out_shape,
        compiler_params=_cparams(1),
        name="inproj_rope" if rope else "inproj_ctx",
    )(*args)


def _lambda(lq1_ref, lk1_ref, lq2_ref, lk2_ref, lam_init):
    a = jnp.sum(lq1_ref[...] * lk1_ref[...], axis=-1, keepdims=True)
    b = jnp.sum(lq2_ref[...] * lk2_ref[...], axis=-1, keepdims=True)
    return jnp.exp(a) - jnp.exp(b) + lam_init


def _lane_fold(op, acc, tile):
    for c in range(0, tile.shape[1], LANES):
        blk = tile[:, c:c + LANES]
        acc = blk if acc is None else op(acc, blk)
    return acc


class _ScoreTiles:
    def __init__(self, q, keys):
        lane = lax.broadcasted_iota(jnp.int32, q.shape, 1)
        zero = jnp.zeros_like(q)
        self.q2 = jnp.concatenate([jnp.where(lane < HEAD_DIM, q, zero),
                                   jnp.where(lane >= HEAD_DIM, q, zero)], axis=0)
        self.slices = []
        for k, feature_major in keys:
            n_keys = k.shape[1] if feature_major else k.shape[0]
            self.slices += [(k, feature_major, lo) for lo in range(0, n_keys, MXU_DIM)]
        self.tiles = []
        self.m_lanes = None

    def __len__(self):
        return len(self.slices)

    def compute(self, t):
        k, feature_major, lo = self.slices[t]
        if feature_major:
            st = jnp.dot(self.q2, k[:, lo:lo + MXU_DIM], preferred_element_type=F32)
        else:
            st = lax.dot_general(self.q2, k[lo:lo + MXU_DIM], (((1,), (1,)), ((), ())),
                                 preferred_element_type=F32)
        self.tiles.append(st)
        self.m_lanes = _lane_fold(jnp.maximum, self.m_lanes, st)

    def row_max(self):
        return jnp.max(self.m_lanes, axis=-1, keepdims=True)


def _attn_units(q_ref, o_ref, sub, lockstep, load_keys, load_values, lam, subln_w, lam_init):
    units = [(slice(r * sub, (r + 1) * sub), slice(h * V_DIM, (h + 1) * V_DIM))
             for r in range(q_ref.shape[0] // sub) for h in range(q_ref.shape[1] // V_DIM)]
    groups = [units[i:i + lockstep] for i in range(0, len(units), lockstep)]

    def scores(group):
        out = []
        for rows, cols in group:
            tiles = _ScoreTiles(q_ref[rows, cols], load_keys(cols))
            for t in range(len(tiles)):
                tiles.compute(t)
            out.append(tiles)
        return out

    nxt = scores(groups[0])
    for i, group in enumerate(groups):
        cur, nxt = nxt, (scores(groups[i + 1]) if i + 1 < len(groups) else None)
        ms = [tiles.row_max() for tiles in cur]
        es = [jnp.concatenate([jnp.exp2(st - m).astype(BF16) for st in tiles.tiles], axis=1)
              for tiles, m in zip(cur, ms)]
        ols = []
        for e, (rows, cols) in zip(es, group):
            v = jnp.concatenate(load_values(cols), axis=0)
            v1 = jnp.concatenate([v, jnp.ones_like(v)], axis=1)
            ols.append(jnp.dot(e, v1, preferred_element_type=F32))
        outs = []
        for ol in ols:
            l1, l2 = ol[:sub, V_DIM:V_DIM + 1], ol[sub:, V_DIM:V_DIM + 1]
            outs.append(ol[:sub, :V_DIM] * (1.0 / l1) - ol[sub:, :V_DIM] * (lam / l2))
        sq = [jnp.mean(o * o, axis=-1, keepdims=True) for o in outs]
        for o, s2, (rows, cols) in zip(outs, sq, group):
            o_ref[rows, cols] = (o * lax.rsqrt(s2 + EPS) * subln_w
                                 * (1.0 - lam_init)).astype(o_ref.dtype)


def _cast_riders(riders, n_steps, step_of):
    in_specs, out_specs, out_shape = [], [], []
    for w, axis in riders:
        blk = tuple(n // n_steps if a == axis else n for a, n in enumerate(w.shape))
        idx = lambda *g, axis=axis, nd=w.ndim: tuple(
            step_of(*g) if a == axis else 0 for a in range(nd))
        in_specs.append(pl.BlockSpec(blk, idx))
        out_specs.append(pl.BlockSpec(blk, idx))
        out_shape.append(jax.ShapeDtypeStruct(w.shape, BF16))
    return in_specs, out_specs, out_shape


def _cast_rider_refs(refs, n):
    for src, dst in zip(refs[:n], refs[n:]):
        dst[...] = src[...].astype(dst.dtype)


def _attn_cached_kernel(q_ref, kn_ref, vn_ref, kct_ref, vc_ref, lq1, lk1, lq2, lk2, sw_ref,
                        *rest, lam_init, sub, n_riders):
    o_ref = rest[n_riders]
    _cast_rider_refs(rest[:n_riders] + rest[n_riders + 1:], n_riders)
    lam = _lambda(lq1, lk1, lq2, lk2, lam_init)
    _attn_units(
        q_ref, o_ref, sub, 1,
        lambda cols: [(kct_ref[cols, :].astype(BF16), True), (kn_ref[:, cols], False)],
        lambda cols: [vc_ref[:, cols].astype(BF16), vn_ref[:, cols]],
        lam, sw_ref[...], lam_init)


def _attn_cached(q, k, v, cache_kt, cache_v, lams, subln_w, riders, *, n_batch, lam_init, tq,
                 sub, heads):
    m, width = q.shape
    seq = m // n_batch
    past = cache_kt.shape[2]
    nq = seq // tq
    w = heads * V_DIM
    n_groups = width // w
    vec = lambda n: pl.BlockSpec((1, n), lambda b, g, t: (0, 0))
    r_in, r_out, r_shape = _cast_riders(
        riders, n_batch * n_groups * nq, lambda b, g, t: (b * n_groups + g) * nq + t)
    return pl.pallas_call(
        functools.partial(_attn_cached_kernel, lam_init=lam_init, sub=sub,
                          n_riders=len(riders)),
        grid=(n_batch, n_groups, nq),
        in_specs=[pl.BlockSpec((tq, w), lambda b, g, t: (b * nq + t, g)),
                  pl.BlockSpec((seq, w), lambda b, g, t: (b, g)),
                  pl.BlockSpec((seq, w), lambda b, g, t: (b, g)),
                  pl.BlockSpec((None, w, past), lambda b, g, t: (b, g, 0)),
                  pl.BlockSpec((past, w), lambda b, g, t: (b, g)),
                  vec(HEAD_DIM), vec(HEAD_DIM), vec(HEAD_DIM), vec(HEAD_DIM), vec(V_DIM)] + r_in,
        out_specs=[pl.BlockSpec((tq, w), lambda b, g, t: (b * nq + t, g))] + r_out,
        out_shape=[jax.ShapeDtypeStruct((m, width), BF16)] + r_shape,
        compiler_params=_cparams(3),
        name="attn_latent",
    )(q, k, v, cache_kt, cache_v, *lams, subln_w, *[w for w, _ in riders])


def _attn_self_kernel(q_ref, k_ref, v_ref, lq1, lk1, lq2, lk2, sw_ref, *rest, lam_init,
                      n_riders):
    o_ref = rest[n_riders]
    _cast_rider_refs(rest[:n_riders] + rest[n_riders + 1:], n_riders)
    lam = _lambda(lq1, lk1, lq2, lk2, lam_init)
    _attn_units(q_ref, o_ref, q_ref.shape[0], q_ref.shape[1] // V_DIM,
                lambda cols: [(k_ref[:, cols], False)],
                lambda cols: [v_ref[:, cols]],
                lam, sw_ref[...], lam_init)


def _attn_self(q, k, v, lams, subln_w, riders, *, n_batch, lam_init):
    m, width = q.shape
    seq = m // n_batch
    blk = pl.BlockSpec((seq, width), lambda b: (b, 0))
    vec = lambda w: pl.BlockSpec((1, w), lambda b: (0, 0))
    r_in, r_out, r_shape = _cast_riders(riders, n_batch, lambda b: b)
    return pl.pallas_call(
        functools.partial(_attn_self_kernel, lam_init=lam_init, n_riders=len(riders)),
        grid=(n_batch,),
        in_specs=[blk, blk, blk,
                  vec(HEAD_DIM), vec(HEAD_DIM), vec(HEAD_DIM), vec(HEAD_DIM), vec(V_DIM)] + r_in,
        out_specs=[blk] + r_out,
        out_shape=[jax.ShapeDtypeStruct((m, width), BF16)] + r_shape,
        compiler_params=_cparams(1),
        name="attn_ctx",
    )(q, k, v, *lams, subln_w, *[w for w, _ in riders])


def _outproj_kernel(att_ref, sgu_ref, wo_ref, x_ref, g1_ref, sh2_ref, sc2_ref, n2_ref,
                    y_ref, h_ref):
    half = att_ref.shape[1]
    mix = jnp.dot(att_ref[...], wo_ref[:half], preferred_element_type=F32)
    mix = mix + jnp.dot(sgu_ref[...], wo_ref[half:], preferred_element_type=F32)
    y = x_ref[...] + g1_ref[...] * mix
    y_ref[...] = y
    ms = jnp.mean(y * y, axis=-1, keepdims=True)
    n = y * lax.rsqrt(ms + EPS) * n2_ref[...]
    h_ref[...] = (n * (1.0 + sc2_ref[...]) + sh2_ref[...]).astype(BF16)


def _outproj(att, sgu, w_o, x, mod, mod_row, norm2_w, *, tm):
    m, d = x.shape
    half = att.shape[1]
    mod_spec = lambda which: pl.BlockSpec(
        (None, None, 1, d), lambda i: (mod_row(i), which, 0, 0))
    return pl.pallas_call(
        _outproj_kernel,
        grid=(m // tm,),
        in_specs=[pl.BlockSpec((tm, half), lambda i: (i, 0)),
                  pl.BlockSpec((tm, half), lambda i: (i, 0)),
                  pl.BlockSpec(w_o.shape, lambda i: (0, 0)),
                  pl.BlockSpec((tm, d), lambda i: (i, 0)),
                  mod_spec(2), mod_spec(3), mod_spec(4),
                  pl.BlockSpec((1, d), lambda i: (0, 0))],
        out_specs=[pl.BlockSpec((tm, d), lambda i: (i, 0))] * 2,
        out_shape=[jax.ShapeDtypeStruct((m, d), F32), jax.ShapeDtypeStruct((m, d), BF16)],
        compiler_params=_cparams(1),
        name="outproj",
    )(att, sgu, w_o, x, mod, mod, mod, norm2_w)


def _mlp_kernel(h_ref, w1_ref, w2_ref, y1_ref, g2_ref, o_ref):
    f = pl.program_id(1)

    @pl.when(f == 0)
    def _():
        o_ref[...] = jnp.zeros_like(o_ref)

    hid = jnp.dot(h_ref[...], w1_ref[...], preferred_element_type=F32)
    hid = jnp.square(jnp.maximum(hid, 0.0)).astype(BF16)
    o_ref[...] += jnp.dot(hid, w2_ref[...], preferred_element_type=F32)

    @pl.when(f == pl.num_programs(1) - 1)
    def _():
        o_ref[...] = y1_ref[...] + g2_ref[...] * o_ref[...]


def _mlp(h, w1, w2, y1, mod, mod_row, *, tm, tf):
    m, d = h.shape
    d_ff = w1.shape[1]
    return pl.pallas_call(
        _mlp_kernel,
        grid=(m // tm, d_ff // tf),
        in_specs=[pl.BlockSpec((tm, d), lambda i, f: (i, 0)),
                  pl.BlockSpec((d, tf), lambda i, f: (0, f)),
                  pl.BlockSpec((tf, d), lambda i, f: (f, 0)),
                  pl.BlockSpec((tm, d), lambda i, f: (i, 0)),
                  pl.BlockSpec((None, None, 1, d), lambda i, f: (mod_row(i), 5, 0, 0))],
        out_specs=pl.BlockSpec((tm, d), lambda i, f: (i, 0)),
        out_shape=jax.ShapeDtypeStruct((m, d), F32),
        compiler_params=_cparams(2),
        name="mlp",
    )(h, w1, w2, y1, mod)


def _rope_tables(n):
    rows = n // GRID_W
    r, col = jnp.meshgrid(jnp.arange(rows), jnp.arange(GRID_W), indexing="ij")
    r = r.reshape(-1).astype(F32)
    col = col.reshape(-1).astype(F32)
    n_freq = HEAD_DIM // 4
    freqs = ROPE_BASE ** (-jnp.arange(n_freq, dtype=F32) / n_freq)
    ang_r = r[:, None] * freqs
    ang_c = col[:, None] * freqs
    ang = jnp.concatenate([ang_r, ang_r, ang_c, ang_c], axis=-1)
    cos = jnp.tile(jnp.cos(ang), (1, LANES // HEAD_DIM))
    sin = jnp.tile(jnp.sin(ang), (1, LANES // HEAD_DIM))
    first = (jnp.arange(LANES) % (2 * n_freq)) < n_freq
    return cos, jnp.where(first, -sin, 0.0), jnp.where(first, 0.0, sin)


def _block_diag_ones(n, blk):
    idx = np.arange(n) // blk
    return jnp.asarray(idx[:, None] == idx[None, :], dtype=BF16)


def kernel(x_prompt, x_sample, cache_k, cache_v, c, c_ctx, w_ada, b_ada, norm1_w, norm2_w,
           w_in, q_norm_w, k_norm_w, lambda_q1, lambda_k1, lambda_q2, lambda_k2, subln_w,
           sgu_norm_w, w_s, b_s, w_o, w_ff1, w_ff2):
    n_ctx, ctx_len, d = x_prompt.shape
    n_lat, lat_len, _ = x_sample.shape
    depth = w_in.shape[0]
    past = cache_k.shape[2]
    width = w_o.shape[1] // 2
    n_heads = width // V_DIM
    tiles = _TILES
    assert n_lat + 1 <= MOD_ROWS and ctx_len % CHUNK == 0
    assert lat_len % tiles.tm == 0 and lat_len % tiles.tm_mlp == 0

    y_p = x_prompt.reshape(n_ctx * ctx_len, d)
    y_s = x_sample.reshape(n_lat * lat_len, d)
    rope_tabs = _rope_tables(lat_len)
    bd = _block_diag_ones(MXU_DIM, HEAD_DIM)
    c_rows = jnp.concatenate(
        [c_ctx[None, :], c, jnp.zeros((MOD_ROWS - 1 - n_lat, d), F32)], axis=0)
    ctx_row = lambda tm: (lambda i: 0)
    lat_row = lambda tm: (lambda i: 1 + i // (lat_len // tm))

    state_k, state_v = [], []
    for l in range(depth):
        lam_init = 0.8 - 0.6 * math.exp(-0.3 * l)
        w_in_l = w_in[l].astype(BF16)
        ws_l = w_s[l].astype(BF16)
        bs_l = b_s[l][:, :, None]
        row = lambda a: a[l].reshape(1, -1)
        qw = jnp.tile(row(q_norm_w), (1, width // HEAD_DIM))
        kw = jnp.tile(row(k_norm_w), (1, width // HEAD_DIM))
        lams = (row(lambda_q1), row(lambda_k1), row(lambda_q2), row(lambda_k2))

        mod = _adaln(c_rows, w_ada[l], b_ada[l]).reshape(MOD_ROWS, N_MOD, 1, d)

        def mixers(x, mod_row, tabs, state_seq):
            return _inproj(x, mod, mod_row(tiles.tm), row(norm1_w), w_in_l, qw, kw, bd, tabs,
                           row(sgu_norm_w), ws_l, bs_l, tm=tiles.tm, state_seq=state_seq)

        def finish(x, att, sgu, mod_row):
            y1, h2 = _outproj(att, sgu, w_o_l, x, mod, mod_row(tiles.tm), row(norm2_w),
                              tm=tiles.tm)
            return _mlp(h2, w1_l, w2_l, y1, mod, mod_row(tiles.tm_mlp),
                        tm=tiles.tm_mlp, tf=tiles.tf)

        q_p, k_p, v_p, sgu_p, sk, sv = mixers(y_p, ctx_row, None, ctx_len)
        sk = sk.reshape(n_ctx, n_heads, 2, HEAD_DIM, ctx_len)
        state_k.append(jnp.transpose(sk, (0, 4, 1, 2, 3)))
        state_v.append(sv.reshape(n_ctx, ctx_len, n_heads, V_DIM))
        q_s, k_s, v_s, sgu_s = mixers(y_s, lat_row, rope_tabs, None)

        att_p, w2_l, w_o_l = _attn_self(
            q_p, k_p, v_p, lams, row(subln_w), [(w_ff2[l], 0), (w_o[l], 0)],
            n_batch=n_ctx, lam_init=lam_init)
        cache_kt = jnp.transpose(cache_k[:, l], (0, 2, 3, 4, 1)).reshape(n_lat, width, past)
        att_s, w1_l = _attn_cached(
            q_s, k_s, v_s, cache_kt, cache_v[:, l].reshape(n_lat * past, width),
            lams, row(subln_w), [(w_ff1[l], 1)], n_batch=n_lat, lam_init=lam_init,
            tq=tiles.tq, sub=tiles.tq_sub, heads=n_heads)

        y_p = finish(y_p, att_p, sgu_p, ctx_row)
        y_s = finish(y_s, att_s, sgu_s, lat_row)

    return (y_p.reshape(n_ctx, ctx_len, d), y_s.reshape(n_lat, lat_len, d),
            jnp.stack(state_k, axis=1), jnp.stack(state_v, axis=1))
```

```python
import functools
import math
from typing import NamedTuple

import jax
import jax.numpy as jnp
import numpy as np
from jax import lax
from jax.experimental import pallas as pl
from jax.experimental.pallas import tpu as pltpu

F32 = jnp.float32
BF16 = jnp.bfloat16

HEAD_DIM = 64
V_DIM = 128
CHUNK = 128
GROUP = 128
GRID_W = 64
ROPE_BASE = 10000.0
EPS = 1e-6
N_MOD = 6
MOD_ROWS = 8
LANES = 128
MXU_DIM = 256
BF16_ROWS = 16
VMEM_LIMIT = 60 * 1024 * 1024
Q_SCALE = HEAD_DIM ** -0.5 * math.log2(math.e)


class _Tiles(NamedTuple):
    tm: int
    tm_mlp: int
    tf: int
    tq: int
    tq_sub: int


_TILES = _Tiles(tm=512, tm_mlp=512, tf=1024, tq=512, tq_sub=256)


def _cparams(n_axes):
    return pltpu.CompilerParams(
        dimension_semantics=("arbitrary",) * n_axes,
        vmem_limit_bytes=VMEM_LIMIT)


def _adaln_kernel(c_ref, w_ref, b_ref, o_ref):
    c = c_ref[...]
    s = c * (1.0 / (1.0 + jnp.exp(-c)))
    o_ref[...] = jnp.dot(s.astype(BF16), w_ref[...].astype(BF16),
                         preferred_element_type=F32) + b_ref[...]


def _adaln(c_rows, w_ada, b_ada, tn=1024):
    d, n = w_ada.shape
    return pl.pallas_call(
        _adaln_kernel,
        grid=(n // tn,),
        in_specs=[pl.BlockSpec((MOD_ROWS, d), lambda j: (0, 0)),
                  pl.BlockSpec((d, tn), lambda j: (0, j)),
                  pl.BlockSpec((1, tn), lambda j: (0, j))],
        out_specs=pl.BlockSpec((MOD_ROWS, tn), lambda j: (0, j)),
        out_shape=jax.ShapeDtypeStruct((MOD_ROWS, n), F32),
        compiler_params=_cparams(1),
        name="adaln",
    )(c_rows, w_ada, b_ada.reshape(1, n))


def _gelu(x):
    return jax.nn.gelu(x, approximate=True)


def _head_norm(z, w_ref, bd_ref):
    outs = []
    for c in range(z.shape[1] // MXU_DIM):
        zc = z[:, c * MXU_DIM:(c + 1) * MXU_DIM]
        ss = jnp.dot((zc * zc).astype(BF16), bd_ref[...], preferred_element_type=F32)
        r = lax.rsqrt(ss * (1.0 / HEAD_DIM) + EPS)
        outs.append(zc * r * w_ref[:, c * MXU_DIM:(c + 1) * MXU_DIM])
    return outs


def _rope128(x, cos, sin_lo, sin_hi):
    quarter = HEAD_DIM // 4
    return (x * cos + pltpu.roll(x, LANES - quarter, 1) * sin_lo
            + pltpu.roll(x, quarter, 1) * sin_hi)


def _inproj_kernel(*refs, rope, emit_state):
    it = iter(refs)
    x_ref, sh_ref, sc_ref, n1_ref, w_ref, qw_ref, kw_ref, bd_ref = (next(it) for _ in range(8))
    if rope:
        cos_ref, slo_ref, shi_ref = (next(it) for _ in range(3))
    sgw_ref, ws_ref, bs_ref = (next(it) for _ in range(3))
    q_ref, k_ref, v_ref, sgu_ref = (next(it) for _ in range(4))
    if emit_state:
        sk_ref, sv_ref = (next(it) for _ in range(2))
    tm = x_ref.shape[0]
    width = q_ref.shape[1]

    x = x_ref[...]
    ms = jnp.mean(x * x, axis=-1, keepdims=True)
    h = (x * lax.rsqrt(ms + EPS) * (n1_ref[...] * (1.0 + sc_ref[...])) + sh_ref[...]).astype(BF16)

    def proj(j):
        return jnp.dot(h, w_ref[:, j * width:(j + 1) * width], preferred_element_type=F32)

    def qk_epilogue(z, w_norm_ref, out_ref, scale, state_ref):
        blocks = _head_norm(z, w_norm_ref, bd_ref)
        for c, n in enumerate(blocks):
            if state_ref is not None:
                seq = state_ref.shape[2]
                for s in range(tm // seq):
                    state_ref[s, c * MXU_DIM:(c + 1) * MXU_DIM, :] = n[s * seq:(s + 1) * seq].T
            for hh in range(MXU_DIM // LANES):
                xh = n[:, hh * LANES:(hh + 1) * LANES]
                if rope:
                    xh = _rope128(xh, cos_ref[...], slo_ref[...], shi_ref[...])
                if scale != 1.0:
                    xh = xh * scale
                lo = c * MXU_DIM + hh * LANES
                out_ref[:, lo:lo + LANES] = xh.astype(BF16)

    zq = proj(0)
    zk = proj(1)
    qk_epilogue(zq, qw_ref, q_ref, Q_SCALE, None)
    zv = proj(2)
    qk_epilogue(zk, kw_ref, k_ref, 1.0, sk_ref if emit_state else None)
    zu = proj(3)
    v_ref[...] = zv.astype(BF16)
    if emit_state:
        sv_ref[...] = zv
    zg = proj(4)
    gu = _gelu(zu)
    g = _gelu(zg)
    mu = jnp.mean(g, axis=-1, keepdims=True)
    gc = g - mu
    var = jnp.mean(gc * gc, axis=-1, keepdims=True)
    gn = (gc * lax.rsqrt(var + EPS) * sgw_ref[...]).astype(BF16)
    n_chunks = tm // CHUNK
    for grp in range(width // GROUP):
        cols = slice(grp * GROUP, (grp + 1) * GROUP)
        rhs = jnp.concatenate(
            [gn[c * CHUNK:(c + 1) * CHUNK, cols] for c in range(n_chunks)], axis=1)
        mixed = jnp.dot(ws_ref[grp], rhs, preferred_element_type=F32) + bs_ref[grp]
        for c in range(n_chunks):
            rows = slice(c * CHUNK, (c + 1) * CHUNK)
            sgu_ref[rows, cols] = (
                gu[rows, cols] * mixed[:, c * CHUNK:(c + 1) * CHUNK]).astype(BF16)


def _inproj(x, mod, mod_row, norm1_w, w_in, qw, kw, bd, rope_tabs, sgw, ws, bs,
            *, tm, state_seq):
    m, d = x.shape
    n_in = w_in.shape[1]
    width = n_in // 5
    rope = rope_tabs is not None
    const = lambda shape: pl.BlockSpec(shape, lambda i: (0,) * len(shape),
                                       pipeline_mode=pl.Buffered(1))
    mod_spec = lambda which: pl.BlockSpec(
        (None, None, 1, d), lambda i: (mod_row(i), which, 0, 0))
    in_specs = [
        pl.BlockSpec((tm, d), lambda i: (i, 0)),
        mod_spec(0), mod_spec(1), const((1, d)),
        const((d, n_in)),
        const((1, width)), const((1, width)),
        const((MXU_DIM, MXU_DIM)),
    ]
    args = [x, mod, mod, norm1_w, w_in, qw, kw, bd]
    if rope:
        seq_tiles = rope_tabs[0].shape[0] // tm
        tab_spec = pl.BlockSpec((tm, LANES), lambda i: (i % seq_tiles, 0))
        in_specs += [tab_spec] * 3
        args += list(rope_tabs)
    in_specs += [const((1, width)), const(ws.shape), const(bs.shape)]
    args += [sgw, ws, bs]
    out_spec = pl.BlockSpec((tm, width), lambda i: (i, 0))
    out_shape = [jax.ShapeDtypeStruct((m, width), BF16)] * 4
    out_specs = [out_spec] * 4
    emit_state = state_seq is not None
    if emit_state:
        out_shape += [jax.ShapeDtypeStruct((m // state_seq, width, state_seq), F32),
                      jax.ShapeDtypeStruct((m, width), F32)]
        out_specs += [pl.BlockSpec((tm // state_seq, width, state_seq), lambda i: (i, 0, 0)),
                      out_spec]
    return pl.pallas_call(
        functools.partial(_inproj_kernel, rope=rope, emit_state=emit_state),
        grid=(m // tm,),
        in_specs=in_specs,
        out_specs=out_specs,
        out_shape=out_shape,
        compiler_params=_cparams(1),
        name="inproj_rope" if rope else "inproj_ctx",
    )(*args)


def _lambda(lq1_ref, lk1_ref, lq2_ref, lk2_ref, lam_init):
    a = jnp.sum(lq1_ref[...] * lk1_ref[...], axis=-1, keepdims=True)
    b = jnp.sum(lq2_ref[...] * lk2_ref[...], axis=-1, keepdims=True)
    return jnp.exp(a) - jnp.exp(b) + lam_init


def _lane_fold(op, acc, tile):
    for c in range(0, tile.shape[1], LANES):
        blk = tile[:, c:c + LANES]
        acc = blk if acc is None else op(acc, blk)
    return acc


class _ScoreTiles:
    def __init__(self, q, keys):
        lane = lax.broadcasted_iota(jnp.int32, q.shape, 1)
        zero = jnp.zeros_like(q)
        self.q2 = jnp.concatenate([jnp.where(lane < HEAD_DIM, q, zero),
                                   jnp.where(lane >= HEAD_DIM, q, zero)], axis=0)
        self.slices = []
        for k, feature_major in keys:
            n_keys = k.shape[1] if feature_major else k.shape[0]
            self.slices += [(k, feature_major, lo) for lo in range(0, n_keys, MXU_DIM)]
        self.tiles = []
        self.m_lanes = None

    def __len__(self):
        return len(self.slices)

    def compute(self, t):
        k, feature_major, lo = self.slices[t]
        if feature_major:
            st = jnp.dot(self.q2, k[:, lo:lo + MXU_DIM], preferred_element_type=F32)
        else:
            st = lax.dot_general(self.q2, k[lo:lo + MXU_DIM], (((1,), (1,)), ((), ())),
                                 preferred_element_type=F32)
        self.tiles.append(st)
        self.m_lanes = _lane_fold(jnp.maximum, self.m_lanes, st)

    def row_max(self):
        return jnp.max(self.m_lanes, axis=-1, keepdims=True)


def _attn_units(q_ref, o_ref, sub, lockstep, load_keys, load_values, lam, subln_w, lam_init,
                fillers):
    units = [(slice(r * sub, (r + 1) * sub), slice(h * V_DIM, (h + 1) * V_DIM))
             for r in range(q_ref.shape[0] // sub) for h in range(q_ref.shape[1] // V_DIM)]
    groups = [units[i:i + lockstep] for i in range(0, len(units), lockstep)]
    n_slots = 3 * len(groups)
    pending = list(fillers)

    def fill(slot):
        done = len(fillers) - len(pending)
        for _ in range(len(fillers) * (slot + 1) // n_slots - done):
            pending.pop(0)()

    def scores(group):
        out = []
        for rows, cols in group:
            tiles = _ScoreTiles(q_ref[rows, cols], load_keys(cols))
            for t in range(len(tiles)):
                tiles.compute(t)
            out.append(tiles)
        return out

    nxt = scores(groups[0])
    for i, group in enumerate(groups):
        cur, nxt = nxt, (scores(groups[i + 1]) if i + 1 < len(groups) else None)
        fill(3 * i)
        ms = [tiles.row_max() for tiles in cur]
        es = [[jnp.exp2(st - m).astype(BF16) for st in tiles.tiles]
              for tiles, m in zip(cur, ms)]
        fill(3 * i + 1)
        ols = []
        for e_tiles, (rows, cols) in zip(es, group):
            v = jnp.concatenate(load_values(cols), axis=0)
            v1 = jnp.concatenate([v, jnp.ones_like(v)], axis=1)
            ol = None
            for t, et in enumerate(e_tiles):
                part = jnp.dot(et, v1[t * MXU_DIM:(t + 1) * MXU_DIM],
                               preferred_element_type=F32)
                ol = part if ol is None else ol + part
            ols.append(ol)
        fill(3 * i + 2)
        outs = []
        for ol in ols:
            l1, l2 = ol[:sub, V_DIM:V_DIM + 1], ol[sub:, V_DIM:V_DIM + 1]
            outs.append(ol[:sub, :V_DIM] * (1.0 / l1) - ol[sub:, :V_DIM] * (lam / l2))
        sq = [jnp.mean(o * o, axis=-1, keepdims=True) for o in outs]
        for o, s2, (rows, cols) in zip(outs, sq, group):
            o_ref[rows, cols] = (o * lax.rsqrt(s2 + EPS) * subln_w
                                 * (1.0 - lam_init)).astype(o_ref.dtype)


def _cast_riders(riders, n_steps, step_of):
    in_specs, out_specs, out_shape = [], [], []
    for w, axis in riders:
        blk = tuple(n // n_steps if a == axis else n for a, n in enumerate(w.shape))
        idx = lambda *g, axis=axis, nd=w.ndim: tuple(
            step_of(*g) if a == axis else 0 for a in range(nd))
        in_specs.append(pl.BlockSpec(blk, idx))
        out_specs.append(pl.BlockSpec(blk, idx))
        out_shape.append(jax.ShapeDtypeStruct(w.shape, BF16))
    return in_specs, out_specs, out_shape


def _cast_rider_fillers(refs, n, n_chunks):
    def chunk(src, dst, rows):
        def cast():
            dst[rows, :] = src[rows, :].astype(dst.dtype)
        return cast

    fillers = []
    for src, dst in zip(refs[:n], refs[n:]):
        k = min(n_chunks, src.shape[0] // BF16_ROWS)
        step = src.shape[0] // k
        fillers += [chunk(src, dst, slice(c * step, (c + 1) * step)) for c in range(k)]
    return fillers


def _attn_cached_kernel(q_ref, kn_ref, vn_ref, kct_ref, vc_ref, lq1, lk1, lq2, lk2, sw_ref,
                        *rest, lam_init, sub, n_riders):
    o_ref = rest[n_riders]
    n_units = (q_ref.shape[0] // sub) * (q_ref.shape[1] // V_DIM)
    fillers = _cast_rider_fillers(rest[:n_riders] + rest[n_riders + 1:], n_riders, n_units)
    lam = _lambda(lq1, lk1, lq2, lk2, lam_init)
    _attn_units(
        q_ref, o_ref, sub, 1,
        lambda cols: [(kct_ref[cols, :].astype(BF16), True), (kn_ref[:, cols], False)],
        lambda cols: [vc_ref[:, cols].astype(BF16), vn_ref[:, cols]],
        lam, sw_ref[...], lam_init, fillers)


def _attn_cached(q, k, v, cache_kt, cache_v, lams, subln_w, riders, *, n_batch, lam_init, tq,
                 sub, heads):
    m, width = q.shape
    seq = m // n_batch
    past = cache_kt.shape[2]
    nq = seq // tq
    w = heads * V_DIM
    n_groups = width // w
    vec = lambda n: pl.BlockSpec((1, n), lambda b, g, t: (0, 0))
    r_in, r_out, r_shape = _cast_riders(
        riders, n_batch * n_groups * nq, lambda b, g, t: (b * n_groups + g) * nq + t)
    return pl.pallas_call(
        functools.partial(_attn_cached_kernel, lam_init=lam_init, sub=sub,
                          n_riders=len(riders)),
        grid=(n_batch, n_groups, nq),
        in_specs=[pl.BlockSpec((tq, w), lambda b, g, t: (b * nq + t, g)),
                  pl.BlockSpec((seq, w), lambda b, g, t: (b, g)),
                  pl.BlockSpec((seq, w), lambda b, g, t: (b, g)),
                  pl.BlockSpec((None, w, past), lambda b, g, t: (b, g, 0)),
                  pl.BlockSpec((past, w), lambda b, g, t: (b, g)),
                  vec(HEAD_DIM), vec(HEAD_DIM), vec(HEAD_DIM), vec(HEAD_DIM), vec(V_DIM)] + r_in,
        out_specs=[pl.BlockSpec((tq, w), lambda b, g, t: (b * nq + t, g))] + r_out,
        out_shape=[jax.ShapeDtypeStruct((m, width), BF16)] + r_shape,
        compiler_params=_cparams(3),
        name="attn_latent",
    )(q, k, v, cache_kt, cache_v, *lams, subln_w, *[w for w, _ in riders])


def _attn_self_kernel(q_ref, k_ref, v_ref, lq1, lk1, lq2, lk2, sw_ref, *rest, lam_init,
                      n_riders):
    o_ref = rest[n_riders]
    fillers = _cast_rider_fillers(rest[:n_riders] + rest[n_riders + 1:], n_riders, 4)
    lam = _lambda(lq1, lk1, lq2, lk2, lam_init)
    _attn_units(q_ref, o_ref, q_ref.shape[0], q_ref.shape[1] // V_DIM,
                lambda cols: [(k_ref[:, cols], False)],
                lambda cols: [v_ref[:, cols]],
                lam, sw_ref[...], lam_init, fillers)


def _attn_self(q, k, v, lams, subln_w, riders, *, n_batch, lam_init):
    m, width = q.shape
    seq = m // n_batch
    blk = pl.BlockSpec((seq, width), lambda b: (b, 0))
    vec = lambda w: pl.BlockSpec((1, w), lambda b: (0, 0))
    r_in, r_out, r_shape = _cast_riders(riders, n_batch, lambda b: b)
    return pl.pallas_call(
        functools.partial(_attn_self_kernel, lam_init=lam_init, n_riders=len(riders)),
        grid=(n_batch,),
        in_specs=[blk, blk, blk,
                  vec(HEAD_DIM), vec(HEAD_DIM), vec(HEAD_DIM), vec(HEAD_DIM), vec(V_DIM)] + r_in,
        out_specs=[blk] + r_out,
        out_shape=[jax.ShapeDtypeStruct((m, width), BF16)] + r_shape,
        compiler_params=_cparams(1),
        name="attn_ctx",
    )(q, k, v, *lams, subln_w, *[w for w, _ in riders])


def _outproj_kernel(att_ref, sgu_ref, wo_ref, x_ref, g1_ref, sh2_ref, sc2_ref, n2_ref,
                    y_ref, h_ref):
    half = att_ref.shape[1]
    mix = jnp.dot(att_ref[...], wo_ref[:half], preferred_element_type=F32)
    mix = mix + jnp.dot(sgu_ref[...], wo_ref[half:], preferred_element_type=F32)
    y = x_ref[...] + g1_ref[...] * mix
    y_ref[...] = y
    ms = jnp.mean(y * y, axis=-1, keepdims=True)
    n = y * lax.rsqrt(ms + EPS) * n2_ref[...]
    h_ref[...] = (n * (1.0 + sc2_ref[...]) + sh2_ref[...]).astype(BF16)


def _outproj(att, sgu, w_o, x, mod, mod_row, norm2_w, *, tm):
    m, d = x.shape
    half = att.shape[1]
    mod_spec = lambda which: pl.BlockSpec(
        (None, None, 1, d), lambda i: (mod_row(i), which, 0, 0))
    return pl.pallas_call(
        _outproj_kernel,
        grid=(m // tm,),
        in_specs=[pl.BlockSpec((tm, half), lambda i: (i, 0)),
                  pl.BlockSpec((tm, half), lambda i: (i, 0)),
                  pl.BlockSpec(w_o.shape, lambda i: (0, 0)),
                  pl.BlockSpec((tm, d), lambda i: (i, 0)),
                  mod_spec(2), mod_spec(3), mod_spec(4),
                  pl.BlockSpec((1, d), lambda i: (0, 0))],
        out_specs=[pl.BlockSpec((tm, d), lambda i: (i, 0))] * 2,
        out_shape=[jax.ShapeDtypeStruct((m, d), F32), jax.ShapeDtypeStruct((m, d), BF16)],
        compiler_params=_cparams(1),
        name="outproj",
    )(att, sgu, w_o, x, mod, mod, mod, norm2_w)


def _mlp_kernel(h_ref, w1_ref, w2_ref, y1_ref, g2_ref, o_ref):
    f = pl.program_id(1)

    @pl.when(f == 0)
    def _():
        o_ref[...] = jnp.zeros_like(o_ref)

    hid = jnp.dot(h_ref[...], w1_ref[...], preferred_element_type=F32)
    hid = jnp.square(jnp.maximum(hid, 0.0)).astype(BF16)
    o_ref[...] += jnp.dot(hid, w2_ref[...], preferred_element_type=F32)

    @pl.when(f == pl.num_programs(1) - 1)
    def _():
        o_ref[...] = y1_ref[...] + g2_ref[...] * o_ref[...]


def _mlp(h, w1, w2, y1, mod, mod_row, *, tm, tf):
    m, d = h.shape
    d_ff = w1.shape[1]
    return pl.pallas_call(
        _mlp_kernel,
        grid=(m // tm, d_ff // tf),
        in_specs=[pl.BlockSpec((tm, d), lambda i, f: (i, 0)),
                  pl.BlockSpec((d, tf), lambda i, f: (0, f)),
                  pl.BlockSpec((tf, d), lambda i, f: (f, 0)),
                  pl.BlockSpec((tm, d), lambda i, f: (i, 0)),
                  pl.BlockSpec((None, None, 1, d), lambda i, f: (mod_row(i), 5, 0, 0))],
        out_specs=pl.BlockSpec((tm, d), lambda i, f: (i, 0)),
        out_shape=jax.ShapeDtypeStruct((m, d), F32),
        compiler_params=_cparams(2),
        name="mlp",
    )(h, w1, w2, y1, mod)


def _rope_tables(n):
    rows = n // GRID_W
    r, col = jnp.meshgrid(jnp.arange(rows), jnp.arange(GRID_W), indexing="ij")
    r = r.reshape(-1).astype(F32)
    col = col.reshape(-1).astype(F32)
    n_freq = HEAD_DIM // 4
    freqs = ROPE_BASE ** (-jnp.arange(n_freq, dtype=F32) / n_freq)
    ang_r = r[:, None] * freqs
    ang_c = col[:, None] * freqs
    ang = jnp.concatenate([ang_r, ang_r, ang_c, ang_c], axis=-1)
    cos = jnp.tile(jnp.cos(ang), (1, LANES // HEAD_DIM))
    sin = jnp.tile(jnp.sin(ang), (1, LANES // HEAD_DIM))
    first = (jnp.arange(LANES) % (2 * n_freq)) < n_freq
    return cos, jnp.where(first, -sin, 0.0), jnp.where(first, 0.0, sin)


def _block_diag_ones(n, blk):
    idx = np.arange(n) // blk
    return jnp.asarray(idx[:, None] == idx[None, :], dtype=BF16)


def kernel(x_prompt, x_sample, cache_k, cache_v, c, c_ctx, w_ada, b_ada, norm1_w, norm2_w,
           w_in, q_norm_w, k_norm_w, lambda_q1, lambda_k1, lambda_q2, lambda_k2, subln_w,
           sgu_norm_w, w_s, b_s, w_o, w_ff1, w_ff2):
    n_ctx, ctx_len, d = x_prompt.shape
    n_lat, lat_len, _ = x_sample.shape
    depth = w_in.shape[0]
    past = cache_k.shape[2]
    width = w_o.shape[1] // 2
    n_heads = width // V_DIM
    tiles = _TILES
    assert n_lat + 1 <= MOD_ROWS and ctx_len % CHUNK == 0
    assert lat_len % tiles.tm == 0 and lat_len % tiles.tm_mlp == 0

    y_p = x_prompt.reshape(n_ctx * ctx_len, d)
    y_s = x_sample.reshape(n_lat * lat_len, d)
    rope_tabs = _rope_tables(lat_len)
    bd = _block_diag_ones(MXU_DIM, HEAD_DIM)
    c_rows = jnp.concatenate(
        [c_ctx[None, :], c, jnp.zeros((MOD_ROWS - 1 - n_lat, d), F32)], axis=0)
    ctx_row = lambda tm: (lambda i: 0)
    lat_row = lambda tm: (lambda i: 1 + i // (lat_len // tm))

    state_k, state_v = [], []
    for l in range(depth):
        lam_init = 0.8 - 0.6 * math.exp(-0.3 * l)
        w_in_l = w_in[l].astype(BF16)
        ws_l = w_s[l].astype(BF16)
        bs_l = b_s[l][:, :, None]
        row = lambda a: a[l].reshape(1, -1)
        qw = jnp.tile(row(q_norm_w), (1, width // HEAD_DIM))
        kw = jnp.tile(row(k_norm_w), (1, width // HEAD_DIM))
        lams = (row(lambda_q1), row(lambda_k1), row(lambda_q2), row(lambda_k2))

        mod = _adaln(c_rows, w_ada[l], b_ada[l]).reshape(MOD_ROWS, N_MOD, 1, d)

        def mixers(x, mod_row, tabs, state_seq):
            return _inproj(x, mod, mod_row(tiles.tm), row(norm1_w), w_in_l, qw, kw, bd, tabs,
                           row(sgu_norm_w), ws_l, bs_l, tm=tiles.tm, state_seq=state_seq)

        def finish(x, att, sgu, mod_row):
            y1, h2 = _outproj(att, sgu, w_o_l, x, mod, mod_row(tiles.tm), row(norm2_w),
                              tm=tiles.tm)
            return _mlp(h2, w1_l, w2_l, y1, mod, mod_row(tiles.tm_mlp),
                        tm=tiles.tm_mlp, tf=tiles.tf)

        q_p, k_p, v_p, sgu_p, sk, sv = mixers(y_p, ctx_row, None, ctx_len)
        sk = sk.reshape(n_ctx, n_heads, 2, HEAD_DIM, ctx_len)
        state_k.append(jnp.transpose(sk, (0, 4, 1, 2, 3)))
        state_v.append(sv.reshape(n_ctx, ctx_len, n_heads, V_DIM))
        q_s, k_s, v_s, sgu_s = mixers(y_s, lat_row, rope_tabs, None)

        att_p, w2_l = _attn_self(
            q_p, k_p, v_p, lams, row(subln_w), [(w_ff2[l], 0)],
            n_batch=n_ctx, lam_init=lam_init)
        cache_kt = jnp.transpose(cache_k[:, l], (0, 2, 3, 4, 1)).reshape(n_lat, width, past)
        att_s, w1_l, w_o_l = _attn_cached(
            q_s, k_s, v_s, cache_kt, cache_v[:, l].reshape(n_lat * past, width),
            lams, row(subln_w), [(w_ff1[l], 1), (w_o[l], 0)], n_batch=n_lat, lam_init=lam_init,
            tq=tiles.tq, sub=tiles.tq_sub, heads=n_heads)

        y_p = finish(y_p, att_p, sgu_p, ctx_row)
        y_s = finish(y_s, att_s, sgu_s, lat_row)

    return (y_p.reshape(n_ctx, ctx_len, d), y_s.reshape(n_lat, lat_len, d),
            jnp.stack(state_k, axis=1), jnp.stack(state_v, axis=1))
```

```python
import functools
import math
from typing import NamedTuple

import jax
import jax.numpy as jnp
import numpy as np
from jax import lax
from jax.experimental import pallas as pl
from jax.experimental.pallas import tpu as pltpu

F32 = jnp.float32
BF16 = jnp.bfloat16

HEAD_DIM = 64
V_DIM = 128
CHUNK = 128
GROUP = 128
GRID_W = 64
ROPE_BASE = 10000.0
EPS = 1e-6
N_MOD = 6
MOD_ROWS = 8
LANES = 128
MXU_DIM = 256
BF16_ROWS = 16
VMEM_LIMIT = 60 * 1024 * 1024
Q_SCALE = HEAD_DIM ** -0.5 * math.log2(math.e)


class _Tiles(NamedTuple):
    tm: int
    tm_mlp: int
    tf: int
    tq: int
    tq_sub: int


_TILES = _Tiles(tm=512, tm_mlp=1024, tf=1024, tq=512, tq_sub=256)


def _cparams(n_axes):
    return pltpu.CompilerParams(
        dimension_semantics=("arbitrary",) * n_axes,
        vmem_limit_bytes=VMEM_LIMIT)


def _adaln_kernel(c_ref, w_ref, b_ref, o_ref):
    c = c_ref[...]
    s = c * (1.0 / (1.0 + jnp.exp(-c)))
    o_ref[...] = jnp.dot(s.astype(BF16), w_ref[...].astype(BF16),
                         preferred_element_type=F32) + b_ref[...]


def _adaln(c_rows, w_ada, b_ada, tn=1024):
    d, n = w_ada.shape
    return pl.pallas_call(
        _adaln_kernel,
        grid=(n // tn,),
        in_specs=[pl.BlockSpec((MOD_ROWS, d), lambda j: (0, 0)),
                  pl.BlockSpec((d, tn), lambda j: (0, j)),
                  pl.BlockSpec((1, tn), lambda j: (0, j))],
        out_specs=pl.BlockSpec((MOD_ROWS, tn), lambda j: (0, j)),
        out_shape=jax.ShapeDtypeStruct((MOD_ROWS, n), F32),
        compiler_params=_cparams(1),
        name="adaln",
    )(c_rows, w_ada, b_ada.reshape(1, n))


def _gelu(x):
    return jax.nn.gelu(x, approximate=True)


def _head_norm(z, w_ref, bd_ref):
    outs = []
    for c in range(z.shape[1] // MXU_DIM):
        zc = z[:, c * MXU_DIM:(c + 1) * MXU_DIM]
        ss = jnp.dot((zc * zc).astype(BF16), bd_ref[...], preferred_element_type=F32)
        r = lax.rsqrt(ss * (1.0 / HEAD_DIM) + EPS)
        outs.append(zc * r * w_ref[:, c * MXU_DIM:(c + 1) * MXU_DIM])
    return outs


def _rope128(x, cos, sin_lo, sin_hi):
    quarter = HEAD_DIM // 4
    return (x * cos + pltpu.roll(x, LANES - quarter, 1) * sin_lo
            + pltpu.roll(x, quarter, 1) * sin_hi)


def _inproj_kernel(*refs, rope, emit_state):
    it = iter(refs)
    x_ref, sh_ref, sc_ref, n1_ref, w_ref, qw_ref, kw_ref, bd_ref = (next(it) for _ in range(8))
    if rope:
        cos_ref, slo_ref, shi_ref = (next(it) for _ in range(3))
    sgw_ref, ws_ref, bs_ref = (next(it) for _ in range(3))
    q_ref, k_ref, v_ref, sgu_ref = (next(it) for _ in range(4))
    if emit_state:
        sk_ref, sv_ref = (next(it) for _ in range(2))
    tm = x_ref.shape[0]
    width = q_ref.shape[1]

    x = x_ref[...]
    ms = jnp.mean(x * x, axis=-1, keepdims=True)
    h = (x * lax.rsqrt(ms + EPS) * (n1_ref[...] * (1.0 + sc_ref[...])) + sh_ref[...]).astype(BF16)

    def proj(j):
        return jnp.dot(h, w_ref[:, j * width:(j + 1) * width], preferred_element_type=F32)

    def qk_epilogue(z, w_norm_ref, out_ref, scale, state_ref):
        blocks = _head_norm(z, w_norm_ref, bd_ref)
        for c, n in enumerate(blocks):
            if state_ref is not None:
                seq = state_ref.shape[2]
                for s in range(tm // seq):
                    state_ref[s, c * MXU_DIM:(c + 1) * MXU_DIM, :] = n[s * seq:(s + 1) * seq].T
            for hh in range(MXU_DIM // LANES):
                xh = n[:, hh * LANES:(hh + 1) * LANES]
                if rope:
                    xh = _rope128(xh, cos_ref[...], slo_ref[...], shi_ref[...])
                if scale != 1.0:
                    xh = xh * scale
                lo = c * MXU_DIM + hh * LANES
                out_ref[:, lo:lo + LANES] = xh.astype(BF16)

    zq = proj(0)
    zk = proj(1)
    qk_epilogue(zq, qw_ref, q_ref, Q_SCALE, None)
    zv = proj(2)
    qk_epilogue(zk, kw_ref, k_ref, 1.0, sk_ref if emit_state else None)
    zu = proj(3)
    v_ref[...] = zv.astype(BF16)
    if emit_state:
        sv_ref[...] = zv
    zg = proj(4)
    gu = _gelu(zu)
    g = _gelu(zg)
    mu = jnp.mean(g, axis=-1, keepdims=True)
    gc = g - mu
    var = jnp.mean(gc * gc, axis=-1, keepdims=True)
    gn = (gc * lax.rsqrt(var + EPS) * sgw_ref[...]).astype(BF16)
    n_chunks = tm // CHUNK
    for grp in range(width // GROUP):
        cols = slice(grp * GROUP, (grp + 1) * GROUP)
        rhs = jnp.concatenate(
            [gn[c * CHUNK:(c + 1) * CHUNK, cols] for c in range(n_chunks)], axis=1)
        mixed = jnp.dot(ws_ref[grp], rhs, preferred_element_type=F32) + bs_ref[grp]
        for c in range(n_chunks):
            rows = slice(c * CHUNK, (c + 1) * CHUNK)
            sgu_ref[rows, cols] = (
                gu[rows, cols] * mixed[:, c * CHUNK:(c + 1) * CHUNK]).astype(BF16)


def _inproj(x, mod, mod_row, norm1_w, w_in, qw, kw, bd, rope_tabs, sgw, ws, bs,
            *, tm, state_seq):
    m, d = x.shape
    n_in = w_in.shape[1]
    width = n_in // 5
    rope = rope_tabs is not None
    const = lambda shape: pl.BlockSpec(shape, lambda i: (0,) * len(shape),
                                       pipeline_mode=pl.Buffered(1))
    mod_spec = lambda which: pl.BlockSpec(
        (None, None, 1, d), lambda i: (mod_row(i), which, 0, 0))
    in_specs = [
        pl.BlockSpec((tm, d), lambda i: (i, 0)),
        mod_spec(0), mod_spec(1), const((1, d)),
        const((d, n_in)),
        const((1, width)), const((1, width)),
        const((MXU_DIM, MXU_DIM)),
    ]
    args = [x, mod, mod, norm1_w, w_in, qw, kw, bd]
    if rope:
        seq_tiles = rope_tabs[0].shape[0] // tm
        tab_spec = pl.BlockSpec((tm, LANES), lambda i: (i % seq_tiles, 0))
        in_specs += [tab_spec] * 3
        args += list(rope_tabs)
    in_specs += [const((1, width)), const(ws.shape), const(bs.shape)]
    args += [sgw, ws, bs]
    out_spec = pl.BlockSpec((tm, width), lambda i: (i, 0))
    out_shape = [jax.ShapeDtypeStruct((m, width), BF16)] * 4
    out_specs = [out_spec] * 4
    emit_state = state_seq is not None
    if emit_state:
        out_shape += [jax.ShapeDtypeStruct((m // state_seq, width, state_seq), F32),
                      jax.ShapeDtypeStruct((m, width), F32)]
        out_specs += [pl.BlockSpec((tm // state_seq, width, state_seq), lambda i: (i, 0, 0)),
                      out_spec]
    return pl.pallas_call(
        functools.partial(_inproj_kernel, rope=rope, emit_state=emit_state),
        grid=(m // tm,),
        in_specs=in_specs,
        out_specs=out_specs,
        out_shape=out_shape,
        compiler_params=_cparams(1),
        name="inproj_rope" if rope else "inproj_ctx",
    )(*args)


def _lambda(lq1_ref, lk1_ref, lq2_ref, lk2_ref, lam_init):
    a = jnp.sum(lq1_ref[...] * lk1_ref[...], axis=-1, keepdims=True)
    b = jnp.sum(lq2_ref[...] * lk2_ref[...], axis=-1, keepdims=True)
    return jnp.exp(a) - jnp.exp(b) + lam_init


def _lane_fold(op, acc, tile):
    for c in range(0, tile.shape[1], LANES):
        blk = tile[:, c:c + LANES]
        acc = blk if acc is None else op(acc, blk)
    return acc


class _ScoreTiles:
    def __init__(self, q, keys):
        lane = lax.broadcasted_iota(jnp.int32, q.shape, 1)
        zero = jnp.zeros_like(q)
        self.q2 = jnp.concatenate([jnp.where(lane < HEAD_DIM, q, zero),
                                   jnp.where(lane >= HEAD_DIM, q, zero)], axis=0)
        self.slices = []
        for k, feature_major in keys:
            n_keys = k.shape[1] if feature_major else k.shape[0]
            self.slices += [(k, feature_major, lo) for lo in range(0, n_keys, MXU_DIM)]
        self.tiles = []
        self.m_lanes = None

    def __len__(self):
        return len(self.slices)

    def compute(self, t):
        k, feature_major, lo = self.slices[t]
        if feature_major:
            st = jnp.dot(self.q2, k[:, lo:lo + MXU_DIM], preferred_element_type=F32)
        else:
            st = lax.dot_general(self.q2, k[lo:lo + MXU_DIM], (((1,), (1,)), ((), ())),
                                 preferred_element_type=F32)
        self.tiles.append(st)
        self.m_lanes = _lane_fold(jnp.maximum, self.m_lanes, st)

    def row_max(self):
        return jnp.max(self.m_lanes, axis=-1, keepdims=True)


def _attn_units(q_ref, o_ref, sub, lockstep, load_keys, load_values, lam, subln_w, lam_init,
                fillers):
    units = [(slice(r * sub, (r + 1) * sub), slice(h * V_DIM, (h + 1) * V_DIM))
             for r in range(q_ref.shape[0] // sub) for h in range(q_ref.shape[1] // V_DIM)]
    groups = [units[i:i + lockstep] for i in range(0, len(units), lockstep)]
    n_slots = 3 * len(groups)
    pending = list(fillers)

    def fill(slot):
        done = len(fillers) - len(pending)
        for _ in range(len(fillers) * (slot + 1) // n_slots - done):
            pending.pop(0)()

    def scores(group):
        out = []
        for rows, cols in group:
            tiles = _ScoreTiles(q_ref[rows, cols], load_keys(cols))
            for t in range(len(tiles)):
                tiles.compute(t)
            out.append(tiles)
        return out

    nxt = scores(groups[0])
    for i, group in enumerate(groups):
        cur, nxt = nxt, (scores(groups[i + 1]) if i + 1 < len(groups) else None)
        fill(3 * i)
        ms = [tiles.row_max() for tiles in cur]
        es = [jnp.concatenate([jnp.exp2(st - m).astype(BF16) for st in tiles.tiles], axis=1)
              for tiles, m in zip(cur, ms)]
        fill(3 * i + 1)
        ols = []
        for e, (rows, cols) in zip(es, group):
            v = jnp.concatenate(load_values(cols), axis=0)
            v1 = jnp.concatenate([v, jnp.ones_like(v)], axis=1)
            ols.append(jnp.dot(e, v1, preferred_element_type=F32))
        fill(3 * i + 2)
        outs = []
        for ol in ols:
            l1, l2 = ol[:sub, V_DIM:V_DIM + 1], ol[sub:, V_DIM:V_DIM + 1]
            outs.append(ol[:sub, :V_DIM] * (1.0 / l1) - ol[sub:, :V_DIM] * (lam / l2))
        sq = [jnp.mean(o * o, axis=-1, keepdims=True) for o in outs]
        for o, s2, (rows, cols) in zip(outs, sq, group):
            o_ref[rows, cols] = (o * lax.rsqrt(s2 + EPS) * subln_w
                                 * (1.0 - lam_init)).astype(o_ref.dtype)


def _cast_riders(riders, n_steps, step_of):
    in_specs, out_specs, out_shape = [], [], []
    for w, axis in riders:
        blk = tuple(n // n_steps if a == axis else n for a, n in enumerate(w.shape))
        idx = lambda *g, axis=axis, nd=w.ndim: tuple(
            step_of(*g) if a == axis else 0 for a in range(nd))
        in_specs.append(pl.BlockSpec(blk, idx))
        out_specs.append(pl.BlockSpec(blk, idx))
        out_shape.append(jax.ShapeDtypeStruct(w.shape, BF16))
    return in_specs, out_specs, out_shape


def _cast_rider_fillers(refs, n, n_chunks):
    def chunk(src, dst, rows):
        def cast():
            dst[rows, :] = src[rows, :].astype(dst.dtype)
        return cast

    fillers = []
    for src, dst in zip(refs[:n], refs[n:]):
        k = min(n_chunks, src.shape[0] // BF16_ROWS)
        step = src.shape[0] // k
        fillers += [chunk(src, dst, slice(c * step, (c + 1) * step)) for c in range(k)]
    return fillers


def _attn_cached_kernel(q_ref, kn_ref, vn_ref, kct_ref, vc_ref, lq1, lk1, lq2, lk2, sw_ref,
                        *rest, lam_init, sub, n_riders):
    o_ref = rest[n_riders]
    n_units = (q_ref.shape[0] // sub) * (q_ref.shape[1] // V_DIM)
    fillers = _cast_rider_fillers(rest[:n_riders] + rest[n_riders + 1:], n_riders, n_units)
    lam = _lambda(lq1, lk1, lq2, lk2, lam_init)
    _attn_units(
        q_ref, o_ref, sub, 1,
        lambda cols: [(kct_ref[cols, :].astype(BF16), True), (kn_ref[:, cols], False)],
        lambda cols: [vc_ref[:, cols].astype(BF16), vn_ref[:, cols]],
        lam, sw_ref[...], lam_init, fillers)


def _attn_cached(q, k, v, cache_kt, cache_v, lams, subln_w, riders, *, n_batch, lam_init, tq,
                 sub, heads):
    m, width = q.shape
    seq = m // n_batch
    past = cache_kt.shape[2]
    nq = seq // tq
    w = heads * V_DIM
    n_groups = width // w
    vec = lambda n: pl.BlockSpec((1, n), lambda b, g, t: (0, 0))
    r_in, r_out, r_shape = _cast_riders(
        riders, n_batch * n_groups * nq, lambda b, g, t: (b * n_groups + g) * nq + t)
    return pl.pallas_call(
        functools.partial(_attn_cached_kernel, lam_init=lam_init, sub=sub,
                          n_riders=len(riders)),
        grid=(n_batch, n_groups, nq),
        in_specs=[pl.BlockSpec((tq, w), lambda b, g, t: (b * nq + t, g)),
                  pl.BlockSpec((seq, w), lambda b, g, t: (b, g)),
                  pl.BlockSpec((seq, w), lambda b, g, t: (b, g)),
                  pl.BlockSpec((None, w, past), lambda b, g, t: (b, g, 0)),
                  pl.BlockSpec((past, w), lambda b, g, t: (b, g)),
                  vec(HEAD_DIM), vec(HEAD_DIM), vec(HEAD_DIM), vec(HEAD_DIM), vec(V_DIM)] + r_in,
        out_specs=[pl.BlockSpec((tq, w), lambda b, g, t: (b * nq + t, g))] + r_out,
        out_shape=[jax.ShapeDtypeStruct((m, width), BF16)] + r_shape,
        compiler_params=_cparams(3),
        name="attn_latent",
    )(q, k, v, cache_kt, cache_v, *lams, subln_w, *[w for w, _ in riders])


def _attn_self_kernel(q_ref, k_ref, v_ref, lq1, lk1, lq2, lk2, sw_ref, *rest, lam_init,
                      n_riders):
    o_ref = rest[n_riders]
    fillers = _cast_rider_fillers(rest[:n_riders] + rest[n_riders + 1:], n_riders, 4)
    lam = _lambda(lq1, lk1, lq2, lk2, lam_init)
    _attn_units(q_ref, o_ref, q_ref.shape[0], q_ref.shape[1] // V_DIM,
                lambda cols: [(k_ref[:, cols], False)],
                lambda cols: [v_ref[:, cols]],
                lam, sw_ref[...], lam_init, fillers)


def _attn_self(q, k, v, lams, subln_w, riders, *, n_batch, lam_init):
    m, width = q.shape
    seq = m // n_batch
    blk = pl.BlockSpec((seq, width), lambda b: (b, 0))
    vec = lambda w: pl.BlockSpec((1, w), lambda b: (0, 0))
    r_in, r_out, r_shape = _cast_riders(riders, n_batch, lambda b: b)
    return pl.pallas_call(
        functools.partial(_attn_self_kernel, lam_init=lam_init, n_riders=len(riders)),
        grid=(n_batch,),
        in_specs=[blk, blk, blk,
                  vec(HEAD_DIM), vec(HEAD_DIM), vec(HEAD_DIM), vec(HEAD_DIM), vec(V_DIM)] + r_in,
        out_specs=[blk] + r_out,
        out_shape=[jax.ShapeDtypeStruct((m, width), BF16)] + r_shape,
        compiler_params=_cparams(1),
        name="attn_ctx",
    )(q, k, v, *lams, subln_w, *[w for w, _ in riders])


def _outproj_kernel(att_ref, sgu_ref, wo_ref, x_ref, g1_ref, sh2_ref, sc2_ref, n2_ref,
                    y_ref, h_ref):
    half = att_ref.shape[1]
    mix = jnp.dot(att_ref[...], wo_ref[:half], preferred_element_type=F32)
    mix = mix + jnp.dot(sgu_ref[...], wo_ref[half:], preferred_element_type=F32)
    y = x_ref[...] + g1_ref[...] * mix
    y_ref[...] = y
    ms = jnp.mean(y * y, axis=-1, keepdims=True)
    n = y * lax.rsqrt(ms + EPS) * n2_ref[...]
    h_ref[...] = (n * (1.0 + sc2_ref[...]) + sh2_ref[...]).astype(BF16)


def _outproj(att, sgu, w_o, x, mod, mod_row, norm2_w, *, tm):
    m, d = x.shape
    half = att.shape[1]
    mod_spec = lambda which: pl.BlockSpec(
        (None, None, 1, d), lambda i: (mod_row(i), which, 0, 0))
    return pl.pallas_call(
        _outproj_kernel,
        grid=(m // tm,),
        in_specs=[pl.BlockSpec((tm, half), lambda i: (i, 0)),
                  pl.BlockSpec((tm, half), lambda i: (i, 0)),
                  pl.BlockSpec(w_o.shape, lambda i: (0, 0)),
                  pl.BlockSpec((tm, d), lambda i: (i, 0)),
                  mod_spec(2), mod_spec(3), mod_spec(4),
                  pl.BlockSpec((1, d), lambda i: (0, 0))],
        out_specs=[pl.BlockSpec((tm, d), lambda i: (i, 0))] * 2,
        out_shape=[jax.ShapeDtypeStruct((m, d), F32), jax.ShapeDtypeStruct((m, d), BF16)],
        compiler_params=_cparams(1),
        name="outproj",
    )(att, sgu, w_o, x, mod, mod, mod, norm2_w)


def _mlp_kernel(h_ref, w1_ref, w2_ref, y1_hbm, g2_ref, o_ref, y1_buf, y1_sem):
    i, f = pl.program_id(0), pl.program_id(1)
    tm = o_ref.shape[0]

    def y1_copy():
        rows = pl.ds(pl.multiple_of(i * tm, tm), tm)
        return pltpu.make_async_copy(y1_hbm.at[rows, :], y1_buf, y1_sem)

    @pl.when(f == 0)
    def _():
        o_ref[...] = jnp.zeros_like(o_ref)
        y1_copy().start()

    hid = jnp.dot(h_ref[...], w1_ref[...], preferred_element_type=F32)
    hid = jnp.square(jnp.maximum(hid, 0.0)).astype(BF16)
    o_ref[...] += jnp.dot(hid, w2_ref[...], preferred_element_type=F32)

    @pl.when(f == pl.num_programs(1) - 1)
    def _():
        y1_copy().wait()
        o_ref[...] = y1_buf[...] + g2_ref[...] * o_ref[...]


def _mlp(h, w1, w2, y1, mod, mod_row, *, tm, tf):
    m, d = h.shape
    d_ff = w1.shape[1]
    return pl.pallas_call(
        _mlp_kernel,
        grid=(m // tm, d_ff // tf),
        in_specs=[pl.BlockSpec((tm, d), lambda i, f: (i, 0)),
                  pl.BlockSpec((d, tf), lambda i, f: (0, f)),
                  pl.BlockSpec((tf, d), lambda i, f: (f, 0)),
                  pl.BlockSpec(memory_space=pl.ANY),
                  pl.BlockSpec((None, None, 1, d), lambda i, f: (mod_row(i), 5, 0, 0))],
        out_specs=pl.BlockSpec((tm, d), lambda i, f: (i, 0)),
        out_shape=jax.ShapeDtypeStruct((m, d), F32),
        scratch_shapes=[pltpu.VMEM((tm, d), F32), pltpu.SemaphoreType.DMA(())],
        compiler_params=_cparams(2),
        name="mlp",
    )(h, w1, w2, y1, mod)


def _rope_tables(n):
    rows = n // GRID_W
    r, col = jnp.meshgrid(jnp.arange(rows), jnp.arange(GRID_W), indexing="ij")
    r = r.reshape(-1).astype(F32)
    col = col.reshape(-1).astype(F32)
    n_freq = HEAD_DIM // 4
    freqs = ROPE_BASE ** (-jnp.arange(n_freq, dtype=F32) / n_freq)
    ang_r = r[:, None] * freqs
    ang_c = col[:, None] * freqs
    ang = jnp.concatenate([ang_r, ang_r, ang_c, ang_c], axis=-1)
    cos = jnp.tile(jnp.cos(ang), (1, LANES // HEAD_DIM))
    sin = jnp.tile(jnp.sin(ang), (1, LANES // HEAD_DIM))
    first = (jnp.arange(LANES) % (2 * n_freq)) < n_freq
    return cos, jnp.where(first, -sin, 0.0), jnp.where(first, 0.0, sin)


def _block_diag_ones(n, blk):
    idx = np.arange(n) // blk
    return jnp.asarray(idx[:, None] == idx[None, :], dtype=BF16)


def kernel(x_prompt, x_sample, cache_k, cache_v, c, c_ctx, w_ada, b_ada, norm1_w, norm2_w,
           w_in, q_norm_w, k_norm_w, lambda_q1, lambda_k1, lambda_q2, lambda_k2, subln_w,
           sgu_norm_w, w_s, b_s, w_o, w_ff1, w_ff2):
    n_ctx, ctx_len, d = x_prompt.shape
    n_lat, lat_len, _ = x_sample.shape
    depth = w_in.shape[0]
    past = cache_k.shape[2]
    width = w_o.shape[1] // 2
    n_heads = width // V_DIM
    tiles = _TILES
    assert n_lat + 1 <= MOD_ROWS and ctx_len % CHUNK == 0
    assert lat_len % tiles.tm == 0 and lat_len % tiles.tm_mlp == 0

    y_p = x_prompt.reshape(n_ctx * ctx_len, d)
    y_s = x_sample.reshape(n_lat * lat_len, d)
    rope_tabs = _rope_tables(lat_len)
    bd = _block_diag_ones(MXU_DIM, HEAD_DIM)
    c_rows = jnp.concatenate(
        [c_ctx[None, :], c, jnp.zeros((MOD_ROWS - 1 - n_lat, d), F32)], axis=0)
    ctx_row = lambda tm: (lambda i: 0)
    lat_row = lambda tm: (lambda i: 1 + i // (lat_len // tm))

    state_k, state_v = [], []
    for l in range(depth):
        lam_init = 0.8 - 0.6 * math.exp(-0.3 * l)
        w_in_l = w_in[l].astype(BF16)
        ws_l = w_s[l].astype(BF16)
        bs_l = b_s[l][:, :, None]
        row = lambda a: a[l].reshape(1, -1)
        qw = jnp.tile(row(q_norm_w), (1, width // HEAD_DIM))
        kw = jnp.tile(row(k_norm_w), (1, width // HEAD_DIM))
        lams = (row(lambda_q1), row(lambda_k1), row(lambda_q2), row(lambda_k2))

        mod = _adaln(c_rows, w_ada[l], b_ada[l]).reshape(MOD_ROWS, N_MOD, 1, d)

        def mixers(x, mod_row, tabs, state_seq):
            return _inproj(x, mod, mod_row(tiles.tm), row(norm1_w), w_in_l, qw, kw, bd, tabs,
                           row(sgu_norm_w), ws_l, bs_l, tm=tiles.tm, state_seq=state_seq)

        def finish(x, att, sgu, mod_row):
            y1, h2 = _outproj(att, sgu, w_o_l, x, mod, mod_row(tiles.tm), row(norm2_w),
                              tm=tiles.tm)
            return _mlp(h2, w1_l, w2_l, y1, mod, mod_row(tiles.tm_mlp),
                        tm=tiles.tm_mlp, tf=tiles.tf)

        q_p, k_p, v_p, sgu_p, sk, sv = mixers(y_p, ctx_row, None, ctx_len)
        sk = sk.reshape(n_ctx, n_heads, 2, HEAD_DIM, ctx_len)
        state_k.append(jnp.transpose(sk, (0, 4, 1, 2, 3)))
        state_v.append(sv.reshape(n_ctx, ctx_len, n_heads, V_DIM))
        q_s, k_s, v_s, sgu_s = mixers(y_s, lat_row, rope_tabs, None)

        att_p, w2_l = _attn_self(
            q_p, k_p, v_p, lams, row(subln_w), [(w_ff2[l], 0)],
            n_batch=n_ctx, lam_init=lam_init)
        cache_kt = jnp.transpose(cache_k[:, l], (0, 2, 3, 4, 1)).reshape(n_lat, width, past)
        att_s, w1_l, w_o_l = _attn_cached(
            q_s, k_s, v_s, cache_kt, cache_v[:, l].reshape(n_lat * past, width),
            lams, row(subln_w), [(w_ff1[l], 1), (w_o[l], 0)], n_batch=n_lat, lam_init=lam_init,
            tq=tiles.tq, sub=tiles.tq_sub, heads=n_heads)

        y_p = finish(y_p, att_p, sgu_p, ctx_row)
        y_s = finish(y_s, att_s, sgu_s, lat_row)

    return (y_p.reshape(n_ctx, ctx_len, d), y_s.reshape(n_lat, lat_len, d),
            jnp.stack(state_k, axis=1), jnp.stack(state_v, axis=1))
```

```python
import functools
import math
from typing import NamedTuple

import jax
import jax.numpy as jnp
import numpy as np
from jax import lax
from jax.experimental import pallas as pl
from jax.experimental.pallas import tpu as pltpu

F32 = jnp.float32
BF16 = jnp.bfloat16

HEAD_DIM = 64
V_DIM = 128
CHUNK = 128
GROUP = 128
GRID_W = 64
ROPE_BASE = 10000.0
EPS = 1e-6
N_MOD = 6
MOD_ROWS = 8
LANES = 128
MXU_DIM = 256
BF16_ROWS = 16
VMEM_LIMIT = 60 * 1024 * 1024
Q_SCALE = HEAD_DIM ** -0.5 * math.log2(math.e)


class _Tiles(NamedTuple):
    tm: int
    tm_mlp: int
    tf: int
    tq: int
    tq_sub: int


_TILES = _Tiles(tm=512, tm_mlp=1024, tf=1024, tq=512, tq_sub=256)


def _cparams(n_axes):
    return pltpu.CompilerParams(
        dimension_semantics=("arbitrary",) * n_axes,
        vmem_limit_bytes=VMEM_LIMIT)


def _adaln_kernel(c_ref, w_ref, b_ref, o_ref):
    c = c_ref[...]
    s = c * (1.0 / (1.0 + jnp.exp(-c)))
    o_ref[...] = jnp.dot(s.astype(BF16), w_ref[...].astype(BF16),
                         preferred_element_type=F32) + b_ref[...]


def _adaln(c_rows, w_ada, b_ada, tn=1024):
    d, n = w_ada.shape
    return pl.pallas_call(
        _adaln_kernel,
        grid=(n // tn,),
        in_specs=[pl.BlockSpec((MOD_ROWS, d), lambda j: (0, 0)),
                  pl.BlockSpec((d, tn), lambda j: (0, j)),
                  pl.BlockSpec((1, tn), lambda j: (0, j))],
        out_specs=pl.BlockSpec((MOD_ROWS, tn), lambda j: (0, j)),
        out_shape=jax.ShapeDtypeStruct((MOD_ROWS, n), F32),
        compiler_params=_cparams(1),
        name="adaln",
    )(c_rows, w_ada, b_ada.reshape(1, n))


def _gelu(x):
    return jax.nn.gelu(x, approximate=True)


def _head_norm(z, w_ref, bd_ref):
    outs = []
    for c in range(z.shape[1] // MXU_DIM):
        zc = z[:, c * MXU_DIM:(c + 1) * MXU_DIM]
        ss = jnp.dot((zc * zc).astype(BF16), bd_ref[...], preferred_element_type=F32)
        r = lax.rsqrt(ss * (1.0 / HEAD_DIM) + EPS)
        outs.append(zc * r * w_ref[:, c * MXU_DIM:(c + 1) * MXU_DIM])
    return outs


def _rope128(x, cos, sin_lo, sin_hi):
    quarter = HEAD_DIM // 4
    return (x * cos + pltpu.roll(x, LANES - quarter, 1) * sin_lo
            + pltpu.roll(x, quarter, 1) * sin_hi)


def _inproj_kernel(*refs, rope, emit_state):
    it = iter(refs)
    x_ref, sh_ref, sc_ref, n1_ref, w_ref, qw_ref, kw_ref, bd_ref = (next(it) for _ in range(8))
    if rope:
        cos_ref, slo_ref, shi_ref = (next(it) for _ in range(3))
    sgw_ref, ws_ref, bs_ref = (next(it) for _ in range(3))
    q_ref, k_ref, v_ref, sgu_ref = (next(it) for _ in range(4))
    if emit_state:
        sk_ref, sv_ref = (next(it) for _ in range(2))
    tm = x_ref.shape[0]
    width = q_ref.shape[1]

    x = x_ref[...]
    ms = jnp.mean(x * x, axis=-1, keepdims=True)
    h = (x * lax.rsqrt(ms + EPS) * (n1_ref[...] * (1.0 + sc_ref[...])) + sh_ref[...]).astype(BF16)

    def proj(j):
        return jnp.dot(h, w_ref[:, j * width:(j + 1) * width], preferred_element_type=F32)

    def qk_epilogue(z, w_norm_ref, out_ref, scale, state_ref):
        blocks = _head_norm(z, w_norm_ref, bd_ref)
        for c, n in enumerate(blocks):
            if state_ref is not None:
                seq = state_ref.shape[2]
                for s in range(tm // seq):
                    state_ref[s, c * MXU_DIM:(c + 1) * MXU_DIM, :] = n[s * seq:(s + 1) * seq].T
            for hh in range(MXU_DIM // LANES):
                xh = n[:, hh * LANES:(hh + 1) * LANES]
                if rope:
                    xh = _rope128(xh, cos_ref[...], slo_ref[...], shi_ref[...])
                if scale != 1.0:
                    xh = xh * scale
                lo = c * MXU_DIM + hh * LANES
                out_ref[:, lo:lo + LANES] = xh.astype(BF16)

    zq = proj(0)
    zk = proj(1)
    qk_epilogue(zq, qw_ref, q_ref, Q_SCALE, None)
    zv = proj(2)
    qk_epilogue(zk, kw_ref, k_ref, 1.0, sk_ref if emit_state else None)
    zu = proj(3)
    v_ref[...] = zv.astype(BF16)
    if emit_state:
        sv_ref[...] = zv
    zg = proj(4)
    gu = _gelu(zu)
    g = _gelu(zg)
    mu = jnp.mean(g, axis=-1, keepdims=True)
    gc = g - mu
    var = jnp.mean(gc * gc, axis=-1, keepdims=True)
    gn = (gc * lax.rsqrt(var + EPS) * sgw_ref[...]).astype(BF16)
    n_chunks = tm // CHUNK
    for grp in range(width // GROUP):
        cols = slice(grp * GROUP, (grp + 1) * GROUP)
        rhs = jnp.concatenate(
            [gn[c * CHUNK:(c + 1) * CHUNK, cols] for c in range(n_chunks)], axis=1)
        mixed = jnp.dot(ws_ref[grp], rhs, preferred_element_type=F32) + bs_ref[grp]
        for c in range(n_chunks):
            rows = slice(c * CHUNK, (c + 1) * CHUNK)
            sgu_ref[rows, cols] = (
                gu[rows, cols] * mixed[:, c * CHUNK:(c + 1) * CHUNK]).astype(BF16)


def _inproj(x, mod, mod_row, norm1_w, w_in, qw, kw, bd, rope_tabs, sgw, ws, bs,
            *, tm, state_seq):
    m, d = x.shape
    n_in = w_in.shape[1]
    width = n_in // 5
    rope = rope_tabs is not None
    const = lambda shape: pl.BlockSpec(shape, lambda i: (0,) * len(shape),
                                       pipeline_mode=pl.Buffered(1))
    mod_spec = lambda which: pl.BlockSpec(
        (None, None, 1, d), lambda i: (mod_row(i), which, 0, 0))
    in_specs = [
        pl.BlockSpec((tm, d), lambda i: (i, 0)),
        mod_spec(0), mod_spec(1), const((1, d)),
        const((d, n_in)),
        const((1, width)), const((1, width)),
        const((MXU_DIM, MXU_DIM)),
    ]
    args = [x, mod, mod, norm1_w, w_in, qw, kw, bd]
    if rope:
        seq_tiles = rope_tabs[0].shape[0] // tm
        tab_spec = pl.BlockSpec((tm, LANES), lambda i: (i % seq_tiles, 0))
        in_specs += [tab_spec] * 3
        args += list(rope_tabs)
    in_specs += [const((1, width)), const(ws.shape), const(bs.shape)]
    args += [sgw, ws, bs]
    out_spec = pl.BlockSpec((tm, width), lambda i: (i, 0))
    out_shape = [jax.ShapeDtypeStruct((m, width), BF16)] * 4
    out_specs = [out_spec] * 4
    emit_state = state_seq is not None
    if emit_state:
        out_shape += [jax.ShapeDtypeStruct((m // state_seq, width, state_seq), F32),
                      jax.ShapeDtypeStruct((m, width), F32)]
        out_specs += [pl.BlockSpec((tm // state_seq, width, state_seq), lambda i: (i, 0, 0)),
                      out_spec]
    return pl.pallas_call(
        functools.partial(_inproj_kernel, rope=rope, emit_state=emit_state),
        grid=(m // tm,),
        in_specs=in_specs,
        out_specs=out_specs,
        out_shape=out_shape,
        compiler_params=_cparams(1),
        name="inproj_rope" if rope else "inproj_ctx",
    )(*args)


def _lambda(lq1_ref, lk1_ref, lq2_ref, lk2_ref, lam_init):
    a = jnp.sum(lq1_ref[...] * lk1_ref[...], axis=-1, keepdims=True)
    b = jnp.sum(lq2_ref[...] * lk2_ref[...], axis=-1, keepdims=True)
    return jnp.exp(a) - jnp.exp(b) + lam_init


def _lane_fold(op, acc, tile):
    for c in range(0, tile.shape[1], LANES):
        blk = tile[:, c:c + LANES]
        acc = blk if acc is None else op(acc, blk)
    return acc


class _ScoreTiles:
    def __init__(self, q, keys):
        lane = lax.broadcasted_iota(jnp.int32, q.shape, 1)
        zero = jnp.zeros_like(q)
        self.q2 = jnp.concatenate([jnp.where(lane < HEAD_DIM, q, zero),
                                   jnp.where(lane >= HEAD_DIM, q, zero)], axis=0)
        self.slices = []
        for k, feature_major in keys:
            n_keys = k.shape[1] if feature_major else k.shape[0]
            self.slices += [(k, feature_major, lo) for lo in range(0, n_keys, MXU_DIM)]
        self.tiles = []
        self.m_lanes = None

    def __len__(self):
        return len(self.slices)

    def compute(self, t):
        k, feature_major, lo = self.slices[t]
        if feature_major:
            st = jnp.dot(self.q2, k[:, lo:lo + MXU_DIM], preferred_element_type=F32)
        else:
            st = lax.dot_general(self.q2, k[lo:lo + MXU_DIM], (((1,), (1,)), ((), ())),
                                 preferred_element_type=F32)
        self.tiles.append(st)
        self.m_lanes = _lane_fold(jnp.maximum, self.m_lanes, st)

    def row_max(self):
        return jnp.max(self.m_lanes, axis=-1, keepdims=True)


def _attn_units(q_ref, o_ref, sub, lockstep, load_keys, load_values, lam, subln_w, lam_init,
                fillers):
    units = [(slice(r * sub, (r + 1) * sub), slice(h * V_DIM, (h + 1) * V_DIM))
             for r in range(q_ref.shape[0] // sub) for h in range(q_ref.shape[1] // V_DIM)]
    groups = [units[i:i + lockstep] for i in range(0, len(units), lockstep)]
    n_slots = 3 * len(groups)
    pending = list(fillers)

    def fill(slot):
        done = len(fillers) - len(pending)
        for _ in range(len(fillers) * (slot + 1) // n_slots - done):
            pending.pop(0)()

    def scores(group):
        out = []
        for rows, cols in group:
            tiles = _ScoreTiles(q_ref[rows, cols], load_keys(cols))
            for t in range(len(tiles)):
                tiles.compute(t)
            out.append(tiles)
        return out

    nxt = scores(groups[0])
    for i, group in enumerate(groups):
        cur, nxt = nxt, (scores(groups[i + 1]) if i + 1 < len(groups) else None)
        fill(3 * i)
        ms = [tiles.row_max() for tiles in cur]
        es = [jnp.concatenate([jnp.exp2(st - m).astype(BF16) for st in tiles.tiles], axis=1)
              for tiles, m in zip(cur, ms)]
        fill(3 * i + 1)
        ols = []
        for e, (rows, cols) in zip(es, group):
            v = jnp.concatenate(load_values(cols), axis=0)
            v1 = jnp.concatenate([v, jnp.ones_like(v)], axis=1)
            ols.append(jnp.dot(e, v1, preferred_element_type=F32))
        fill(3 * i + 2)
        outs = []
        for ol in ols:
            l1, l2 = ol[:sub, V_DIM:V_DIM + 1], ol[sub:, V_DIM:V_DIM + 1]
            outs.append(ol[:sub, :V_DIM] * (1.0 / l1) - ol[sub:, :V_DIM] * (lam / l2))
        sq = [jnp.mean(o * o, axis=-1, keepdims=True) for o in outs]
        for o, s2, (rows, cols) in zip(outs, sq, group):
            o_ref[rows, cols] = (o * lax.rsqrt(s2 + EPS) * subln_w
                                 * (1.0 - lam_init)).astype(o_ref.dtype)


def _cast_riders(riders, n_steps, step_of):
    in_specs, out_specs, out_shape = [], [], []
    for w, axis in riders:
        blk = tuple(n // n_steps if a == axis else n for a, n in enumerate(w.shape))
        idx = lambda *g, axis=axis, nd=w.ndim: tuple(
            step_of(*g) if a == axis else 0 for a in range(nd))
        in_specs.append(pl.BlockSpec(blk, idx))
        out_specs.append(pl.BlockSpec(blk, idx))
        out_shape.append(jax.ShapeDtypeStruct(w.shape, BF16))
    return in_specs, out_specs, out_shape


def _cast_rider_fillers(refs, n, n_chunks):
    def chunk(src, dst, rows):
        def cast():
            dst[rows, :] = src[rows, :].astype(dst.dtype)
        return cast

    fillers = []
    for src, dst in zip(refs[:n], refs[n:]):
        k = min(n_chunks, src.shape[0] // BF16_ROWS)
        step = src.shape[0] // k
        fillers += [chunk(src, dst, slice(c * step, (c + 1) * step)) for c in range(k)]
    return fillers


def _attn_cached_kernel(q_ref, kn_ref, vn_ref, kct_ref, vc_ref, lq1, lk1, lq2, lk2, sw_ref,
                        *rest, lam_init, sub, n_riders):
    o_ref = rest[n_riders]
    n_units = (q_ref.shape[0] // sub) * (q_ref.shape[1] // V_DIM)
    fillers = _cast_rider_fillers(rest[:n_riders] + rest[n_riders + 1:], n_riders, n_units)
    lam = _lambda(lq1, lk1, lq2, lk2, lam_init)
    _attn_units(
        q_ref, o_ref, sub, 1,
        lambda cols: [(kct_ref[cols, :].astype(BF16), True), (kn_ref[:, cols], False)],
        lambda cols: [vc_ref[:, cols].astype(BF16), vn_ref[:, cols]],
        lam, sw_ref[...], lam_init, fillers)


def _attn_cached(q, k, v, cache_kt, cache_v, lams, subln_w, riders, *, n_batch, lam_init, tq,
                 sub, heads):
    m, width = q.shape
    seq = m // n_batch
    past = cache_kt.shape[2]
    nq = seq // tq
    w = heads * V_DIM
    n_groups = width // w
    vec = lambda n: pl.BlockSpec((1, n), lambda b, g, t: (0, 0))
    r_in, r_out, r_shape = _cast_riders(
        riders, n_batch * n_groups * nq, lambda b, g, t: (b * n_groups + g) * nq + t)
    return pl.pallas_call(
        functools.partial(_attn_cached_kernel, lam_init=lam_init, sub=sub,
                          n_riders=len(riders)),
        grid=(n_batch, n_groups, nq),
        in_specs=[pl.BlockSpec((tq, w), lambda b, g, t: (b * nq + t, g)),
                  pl.BlockSpec((seq, w), lambda b, g, t: (b, g)),
                  pl.BlockSpec((seq, w), lambda b, g, t: (b, g)),
                  pl.BlockSpec((None, w, past), lambda b, g, t: (b, g, 0)),
                  pl.BlockSpec((past, w), lambda b, g, t: (b, g)),
                  vec(HEAD_DIM), vec(HEAD_DIM), vec(HEAD_DIM), vec(HEAD_DIM), vec(V_DIM)] + r_in,
        out_specs=[pl.BlockSpec((tq, w), lambda b, g, t: (b * nq + t, g))] + r_out,
        out_shape=[jax.ShapeDtypeStruct((m, width), BF16)] + r_shape,
        compiler_params=_cparams(3),
        name="attn_latent",
    )(q, k, v, cache_kt, cache_v, *lams, subln_w, *[w for w, _ in riders])


def _attn_self_kernel(q_ref, k_ref, v_ref, lq1, lk1, lq2, lk2, sw_ref, *rest, lam_init,
                      n_riders):
    o_ref = rest[n_riders]
    fillers = _cast_rider_fillers(rest[:n_riders] + rest[n_riders + 1:], n_riders, 4)
    lam = _lambda(lq1, lk1, lq2, lk2, lam_init)
    _attn_units(q_ref, o_ref, q_ref.shape[0], q_ref.shape[1] // V_DIM,
                lambda cols: [(k_ref[:, cols], False)],
                lambda cols: [v_ref[:, cols]],
                lam, sw_ref[...], lam_init, fillers)


def _attn_self(q, k, v, lams, subln_w, riders, *, n_batch, lam_init):
    m, width = q.shape
    seq = m // n_batch
    blk = pl.BlockSpec((seq, width), lambda b: (b, 0))
    vec = lambda w: pl.BlockSpec((1, w), lambda b: (0, 0))
    r_in, r_out, r_shape = _cast_riders(riders, n_batch, lambda b: b)
    return pl.pallas_call(
        functools.partial(_attn_self_kernel, lam_init=lam_init, n_riders=len(riders)),
        grid=(n_batch,),
        in_specs=[blk, blk, blk,
                  vec(HEAD_DIM), vec(HEAD_DIM), vec(HEAD_DIM), vec(HEAD_DIM), vec(V_DIM)] + r_in,
        out_specs=[blk] + r_out,
        out_shape=[jax.ShapeDtypeStruct((m, width), BF16)] + r_shape,
        compiler_params=_cparams(1),
        name="attn_ctx",
    )(q, k, v, *lams, subln_w, *[w for w, _ in riders])


def _outproj_kernel(att_ref, sgu_ref, wo_ref, x_ref, g1_ref, sh2_ref, sc2_ref, n2_ref,
                    y_ref, h_ref):
    half = att_ref.shape[1]
    mix = jnp.dot(att_ref[...], wo_ref[:half], preferred_element_type=F32)
    mix = mix + jnp.dot(sgu_ref[...], wo_ref[half:], preferred_element_type=F32)
    y = x_ref[...] + g1_ref[...] * mix
    y_ref[...] = y
    ms = jnp.mean(y * y, axis=-1, keepdims=True)
    n = y * lax.rsqrt(ms + EPS) * n2_ref[...]
    h_ref[...] = (n * (1.0 + sc2_ref[...]) + sh2_ref[...]).astype(BF16)


def _outproj(att, sgu, w_o, x, mod, mod_row, norm2_w, *, tm):
    m, d = x.shape
    half = att.shape[1]
    mod_spec = lambda which: pl.BlockSpec(
        (None, None, 1, d), lambda i: (mod_row(i), which, 0, 0))
    return pl.pallas_call(
        _outproj_kernel,
        grid=(m // tm,),
        in_specs=[pl.BlockSpec((tm, half), lambda i: (i, 0)),
                  pl.BlockSpec((tm, half), lambda i: (i, 0)),
                  pl.BlockSpec(w_o.shape, lambda i: (0, 0)),
                  pl.BlockSpec((tm, d), lambda i: (i, 0)),
                  mod_spec(2), mod_spec(3), mod_spec(4),
                  pl.BlockSpec((1, d), lambda i: (0, 0))],
        out_specs=[pl.BlockSpec((tm, d), lambda i: (i, 0))] * 2,
        out_shape=[jax.ShapeDtypeStruct((m, d), F32), jax.ShapeDtypeStruct((m, d), BF16)],
        compiler_params=_cparams(1),
        name="outproj",
    )(att, sgu, w_o, x, mod, mod, mod, norm2_w)


def _mlp_kernel(h_ref, w1_ref, w2_ref, y1_hbm, g2_ref, o_ref, y1_buf, y1_sem):
    i, f = pl.program_id(0), pl.program_id(1)
    tm = o_ref.shape[0]

    def y1_copy():
        rows = pl.ds(pl.multiple_of(i * tm, tm), tm)
        return pltpu.make_async_copy(y1_hbm.at[rows, :], y1_buf, y1_sem)

    last = pl.num_programs(1) - 1

    def part():
        hid = jnp.dot(h_ref[...], w1_ref[...], preferred_element_type=F32)
        hid = jnp.square(jnp.maximum(hid, 0.0)).astype(BF16)
        return jnp.dot(hid, w2_ref[...], preferred_element_type=F32)

    @pl.when(f == 0)
    def _():
        y1_copy().start()
        o_ref[...] = part()

    @pl.when(jnp.logical_and(f > 0, f < last))
    def _():
        o_ref[...] += part()

    @pl.when(f == last)
    def _():
        y1_copy().wait()
        o_ref[...] = y1_buf[...] + g2_ref[...] * (o_ref[...] + part())


def _mlp(h, w1, w2, y1, mod, mod_row, *, tm, tf):
    m, d = h.shape
    d_ff = w1.shape[1]
    assert d_ff // tf >= 2
    return pl.pallas_call(
        _mlp_kernel,
        grid=(m // tm, d_ff // tf),
        in_specs=[pl.BlockSpec((tm, d), lambda i, f: (i, 0)),
                  pl.BlockSpec((d, tf), lambda i, f: (0, f)),
                  pl.BlockSpec((tf, d), lambda i, f: (f, 0)),
                  pl.BlockSpec(memory_space=pl.ANY),
                  pl.BlockSpec((None, None, 1, d), lambda i, f: (mod_row(i), 5, 0, 0))],
        out_specs=pl.BlockSpec((tm, d), lambda i, f: (i, 0)),
        out_shape=jax.ShapeDtypeStruct((m, d), F32),
        scratch_shapes=[pltpu.VMEM((tm, d), F32), pltpu.SemaphoreType.DMA(())],
        compiler_params=_cparams(2),
        name="mlp",
    )(h, w1, w2, y1, mod)


def _rope_tables(n):
    rows = n // GRID_W
    r, col = jnp.meshgrid(jnp.arange(rows), jnp.arange(GRID_W), indexing="ij")
    r = r.reshape(-1).astype(F32)
    col = col.reshape(-1).astype(F32)
    n_freq = HEAD_DIM // 4
    freqs = ROPE_BASE ** (-jnp.arange(n_freq, dtype=F32) / n_freq)
    ang_r = r[:, None] * freqs
    ang_c = col[:, None] * freqs
    ang = jnp.concatenate([ang_r, ang_r, ang_c, ang_c], axis=-1)
    cos = jnp.tile(jnp.cos(ang), (1, LANES // HEAD_DIM))
    sin = jnp.tile(jnp.sin(ang), (1, LANES // HEAD_DIM))
    first = (jnp.arange(LANES) % (2 * n_freq)) < n_freq
    return cos, jnp.where(first, -sin, 0.0), jnp.where(first, 0.0, sin)


def _block_diag_ones(n, blk):
    idx = np.arange(n) // blk
    return jnp.asarray(idx[:, None] == idx[None, :], dtype=BF16)


def kernel(x_prompt, x_sample, cache_k, cache_v, c, c_ctx, w_ada, b_ada, norm1_w, norm2_w,
           w_in, q_norm_w, k_norm_w, lambda_q1, lambda_k1, lambda_q2, lambda_k2, subln_w,
           sgu_norm_w, w_s, b_s, w_o, w_ff1, w_ff2):
    n_ctx, ctx_len, d = x_prompt.shape
    n_lat, lat_len, _ = x_sample.shape
    depth = w_in.shape[0]
    past = cache_k.shape[2]
    width = w_o.shape[1] // 2
    n_heads = width // V_DIM
    tiles = _TILES
    assert n_lat + 1 <= MOD_ROWS and ctx_len % CHUNK == 0
    assert lat_len % tiles.tm == 0 and lat_len % tiles.tm_mlp == 0

    y_p = x_prompt.reshape(n_ctx * ctx_len, d)
    y_s = x_sample.reshape(n_lat * lat_len, d)
    rope_tabs = _rope_tables(lat_len)
    bd = _block_diag_ones(MXU_DIM, HEAD_DIM)
    c_rows = jnp.concatenate(
        [c_ctx[None, :], c, jnp.zeros((MOD_ROWS - 1 - n_lat, d), F32)], axis=0)
    ctx_row = lambda tm: (lambda i: 0)
    lat_row = lambda tm: (lambda i: 1 + i // (lat_len // tm))

    state_k, state_v = [], []
    for l in range(depth):
        lam_init = 0.8 - 0.6 * math.exp(-0.3 * l)
        w_in_l = w_in[l].astype(BF16)
        ws_l = w_s[l].astype(BF16)
        bs_l = b_s[l][:, :, None]
        row = lambda a: a[l].reshape(1, -1)
        qw = jnp.tile(row(q_norm_w), (1, width // HEAD_DIM))
        kw = jnp.tile(row(k_norm_w), (1, width // HEAD_DIM))
        lams = (row(lambda_q1), row(lambda_k1), row(lambda_q2), row(lambda_k2))

        mod = _adaln(c_rows, w_ada[l], b_ada[l]).reshape(MOD_ROWS, N_MOD, 1, d)

        def mixers(x, mod_row, tabs, state_seq):
            return _inproj(x, mod, mod_row(tiles.tm), row(norm1_w), w_in_l, qw, kw, bd, tabs,
                           row(sgu_norm_w), ws_l, bs_l, tm=tiles.tm, state_seq=state_seq)

        def finish(x, att, sgu, mod_row):
            y1, h2 = _outproj(att, sgu, w_o_l, x, mod, mod_row(tiles.tm), row(norm2_w),
                              tm=tiles.tm)
            return _mlp(h2, w1_l, w2_l, y1, mod, mod_row(tiles.tm_mlp),
                        tm=tiles.tm_mlp, tf=tiles.tf)

        q_p, k_p, v_p, sgu_p, sk, sv = mixers(y_p, ctx_row, None, ctx_len)
        sk = sk.reshape(n_ctx, n_heads, 2, HEAD_DIM, ctx_len)
        state_k.append(jnp.transpose(sk, (0, 4, 1, 2, 3)))
        state_v.append(sv.reshape(n_ctx, ctx_len, n_heads, V_DIM))
        q_s, k_s, v_s, sgu_s = mixers(y_s, lat_row, rope_tabs, None)

        att_p, w2_l = _attn_self(
            q_p, k_p, v_p, lams, row(subln_w), [(w_ff2[l], 0)],
            n_batch=n_ctx, lam_init=lam_init)
        cache_kt = jnp.transpose(cache_k[:, l], (0, 2, 3, 4, 1)).reshape(n_lat, width, past)
        att_s, w1_l, w_o_l = _attn_cached(
            q_s, k_s, v_s, cache_kt, cache_v[:, l].reshape(n_lat * past, width),
            lams, row(subln_w), [(w_ff1[l], 1), (w_o[l], 0)], n_batch=n_lat, lam_init=lam_init,
            tq=tiles.tq, sub=tiles.tq_sub, heads=n_heads)

        y_p = finish(y_p, att_p, sgu_p, ctx_row)
        y_s = finish(y_s, att_s, sgu_s, lat_row)

    return (y_p.reshape(n_ctx, ctx_len, d), y_s.reshape(n_lat, lat_len, d),
            jnp.stack(state_k, axis=1), jnp.stack(state_v, axis=1))
```

```python
import functools
import math
from typing import NamedTuple

import jax
import jax.numpy as jnp
import numpy as np
from jax import lax
from jax.experimental import pallas as pl
from jax.experimental.pallas import tpu as pltpu

F32 = jnp.float32
BF16 = jnp.bfloat16

HEAD_DIM = 64
V_DIM = 128
CHUNK = 128
GROUP = 128
GRID_W = 64
ROPE_BASE = 10000.0
EPS = 1e-6
N_MOD_IN = 2
N_MOD_OUT = 4
MOD_ROWS = 8
LANES = 128
MXU_DIM = 256
BF16_ROWS = 16
VMEM_LIMIT = 60 * 1024 * 1024
Q_SCALE = HEAD_DIM ** -0.5 * math.log2(math.e)


class _Tiles(NamedTuple):
    tm: int
    tm_sub: int
    tm_mlp: int
    tf: int
    tq: int
    tq_sub: int


_TILES = _Tiles(tm=512, tm_sub=256, tm_mlp=1024, tf=1024, tq=512, tq_sub=256)


def _cparams(n_axes):
    return pltpu.CompilerParams(
        dimension_semantics=("arbitrary",) * n_axes,
        vmem_limit_bytes=VMEM_LIMIT)


def _adaln_slab(c_ref, w_ref, b_ref, o_ref):
    c = c_ref[...]
    s = c * (1.0 / (1.0 + jnp.exp(-c)))
    o_ref[...] = jnp.dot(s.astype(BF16), w_ref[...].astype(BF16),
                         preferred_element_type=F32) + b_ref[...]


def _adaln_specs(d, tn, first_block):
    return ([pl.BlockSpec((MOD_ROWS, d), lambda *g: (0, 0)),
             pl.BlockSpec((d, tn), lambda *g: (0, first_block + g[-1])),
             pl.BlockSpec((1, tn), lambda *g: (0, first_block + g[-1]))],
            pl.BlockSpec((MOD_ROWS, tn), lambda *g: (0, g[-1])))


def _adaln(c_rows, w_ada, b_ada, n_cols, tn=1024):
    d = w_ada.shape[0]
    in_specs, out_spec = _adaln_specs(d, tn, 0)
    return pl.pallas_call(
        _adaln_slab,
        grid=(n_cols // tn,),
        in_specs=in_specs,
        out_specs=out_spec,
        out_shape=jax.ShapeDtypeStruct((MOD_ROWS, n_cols), F32),
        compiler_params=_cparams(1),
        name="adaln",
    )(c_rows, w_ada, b_ada)


def _gelu(x):
    return jax.nn.gelu(x, approximate=True)


def _head_norm(z, w_ref, bd_ref):
    outs = []
    for c in range(z.shape[1] // MXU_DIM):
        zc = z[:, c * MXU_DIM:(c + 1) * MXU_DIM]
        ss = jnp.dot((zc * zc).astype(BF16), bd_ref[...], preferred_element_type=F32)
        r = lax.rsqrt(ss * (1.0 / HEAD_DIM) + EPS)
        outs.append(zc * r * w_ref[:, c * MXU_DIM:(c + 1) * MXU_DIM])
    return outs


def _rope128(x, cos, sin_lo, sin_hi):
    quarter = HEAD_DIM // 4
    return (x * cos + pltpu.roll(x, LANES - quarter, 1) * sin_lo
            + pltpu.roll(x, quarter, 1) * sin_hi)


def _inproj_kernel(*refs, rope, emit_state):
    it = iter(refs)
    x_ref, sh_ref, sc_ref, n1_ref, w_ref, qw_ref, kw_ref, bd_ref = (next(it) for _ in range(8))
    if rope:
        cos_ref, slo_ref, shi_ref = (next(it) for _ in range(3))
    sgw_ref, ws_ref, bs_ref = (next(it) for _ in range(3))
    q_ref, k_ref, v_ref, sgu_ref = (next(it) for _ in range(4))
    if emit_state:
        sk_ref, sv_ref = (next(it) for _ in range(2))
    tm = x_ref.shape[0]
    width = q_ref.shape[1]

    x = x_ref[...]
    ms = jnp.mean(x * x, axis=-1, keepdims=True)
    h = (x * lax.rsqrt(ms + EPS) * (n1_ref[...] * (1.0 + sc_ref[...])) + sh_ref[...]).astype(BF16)

    def proj(j):
        return jnp.dot(h, w_ref[:, j * width:(j + 1) * width], preferred_element_type=F32)

    def qk_epilogue(z, w_norm_ref, out_ref, scale, state_ref):
        blocks = _head_norm(z, w_norm_ref, bd_ref)
        for c, n in enumerate(blocks):
            if state_ref is not None:
                seq = state_ref.shape[2]
                for s in range(tm // seq):
                    state_ref[s, c * MXU_DIM:(c + 1) * MXU_DIM, :] = n[s * seq:(s + 1) * seq].T
            for hh in range(MXU_DIM // LANES):
                xh = n[:, hh * LANES:(hh + 1) * LANES]
                if rope:
                    xh = _rope128(xh, cos_ref[...], slo_ref[...], shi_ref[...])
                if scale != 1.0:
                    xh = xh * scale
                lo = c * MXU_DIM + hh * LANES
                out_ref[:, lo:lo + LANES] = xh.astype(BF16)

    zq = proj(0)
    zk = proj(1)
    qk_epilogue(zq, qw_ref, q_ref, Q_SCALE, None)
    zv = proj(2)
    qk_epilogue(zk, kw_ref, k_ref, 1.0, sk_ref if emit_state else None)
    zu = proj(3)
    v_ref[...] = zv.astype(BF16)
    if emit_state:
        sv_ref[...] = zv
    zg = proj(4)
    gu = _gelu(zu)
    g = _gelu(zg)
    mu = jnp.mean(g, axis=-1, keepdims=True)
    gc = g - mu
    var = jnp.mean(gc * gc, axis=-1, keepdims=True)
    gn = (gc * lax.rsqrt(var + EPS) * sgw_ref[...]).astype(BF16)
    n_chunks = tm // CHUNK
    for grp in range(width // GROUP):
        cols = slice(grp * GROUP, (grp + 1) * GROUP)
        rhs = jnp.concatenate(
            [gn[c * CHUNK:(c + 1) * CHUNK, cols] for c in range(n_chunks)], axis=1)
        mixed = jnp.dot(ws_ref[grp], rhs, preferred_element_type=F32) + bs_ref[grp]
        for c in range(n_chunks):
            rows = slice(c * CHUNK, (c + 1) * CHUNK)
            sgu_ref[rows, cols] = (
                gu[rows, cols] * mixed[:, c * CHUNK:(c + 1) * CHUNK]).astype(BF16)


def _inproj(x, mod, mod_row, norm1_w, w_in, qw, kw, bd, rope_tabs, sgw, ws, bs,
            *, tm, state_seq):
    m, d = x.shape
    n_in = w_in.shape[1]
    width = n_in // 5
    rope = rope_tabs is not None
    const = lambda shape: pl.BlockSpec(shape, lambda i: (0,) * len(shape),
                                       pipeline_mode=pl.Buffered(1))
    mod_spec = lambda which: pl.BlockSpec(
        (None, None, 1, d), lambda i: (mod_row(i), which, 0, 0))
    in_specs = [
        pl.BlockSpec((tm, d), lambda i: (i, 0)),
        mod_spec(0), mod_spec(1), const((1, d)),
        const((d, n_in)),
        const((1, width)), const((1, width)),
        const((MXU_DIM, MXU_DIM)),
    ]
    args = [x, mod, mod, norm1_w, w_in, qw, kw, bd]
    if rope:
        seq_tiles = rope_tabs[0].shape[0] // tm
        tab_spec = pl.BlockSpec((tm, LANES), lambda i: (i % seq_tiles, 0))
        in_specs += [tab_spec] * 3
        args += list(rope_tabs)
    in_specs += [const((1, width)), const(ws.shape), const(bs.shape)]
    args += [sgw, ws, bs]
    out_spec = pl.BlockSpec((tm, width), lambda i: (i, 0))
    out_shape = [jax.ShapeDtypeStruct((m, width), BF16)] * 4
    out_specs = [out_spec] * 4
    emit_state = state_seq is not None
    if emit_state:
        out_shape += [jax.ShapeDtypeStruct((m // state_seq, width, state_seq), F32),
                      jax.ShapeDtypeStruct((m, width), F32)]
        out_specs += [pl.BlockSpec((tm // state_seq, width, state_seq), lambda i: (i, 0, 0)),
                      out_spec]
    return pl.pallas_call(
        functools.partial(_inproj_kernel, rope=rope, emit_state=emit_state),
        grid=(m // tm,),
        in_specs=in_specs,
        out_specs=out_specs,
        out_shape=out_shape,
        compiler_params=_cparams(1),
        name="inproj_rope" if rope else "inproj_ctx",
    )(*args)


def _lambda(lq1_ref, lk1_ref, lq2_ref, lk2_ref, lam_init):
    a = jnp.sum(lq1_ref[...] * lk1_ref[...], axis=-1, keepdims=True)
    b = jnp.sum(lq2_ref[...] * lk2_ref[...], axis=-1, keepdims=True)
    return jnp.exp(a) - jnp.exp(b) + lam_init


def _lane_fold(op, acc, tile):
    for c in range(0, tile.shape[1], LANES):
        blk = tile[:, c:c + LANES]
        acc = blk if acc is None else op(acc, blk)
    return acc


class _ScoreTiles:
    def __init__(self, q, keys):
        lane = lax.broadcasted_iota(jnp.int32, q.shape, 1)
        zero = jnp.zeros_like(q)
        self.q2 = jnp.concatenate([jnp.where(lane < HEAD_DIM, q, zero),
                                   jnp.where(lane >= HEAD_DIM, q, zero)], axis=0)
        self.slices = []
        for k, feature_major in keys:
            n_keys = k.shape[1] if feature_major else k.shape[0]
            self.slices += [(k, feature_major, lo) for lo in range(0, n_keys, MXU_DIM)]
        self.tiles = []
        self.m_lanes = None

    def __len__(self):
        return len(self.slices)

    def compute(self, t):
        k, feature_major, lo = self.slices[t]
        if feature_major:
            st = jnp.dot(self.q2, k[:, lo:lo + MXU_DIM], preferred_element_type=F32)
        else:
            st = lax.dot_general(self.q2, k[lo:lo + MXU_DIM], (((1,), (1,)), ((), ())),
                                 preferred_element_type=F32)
        self.tiles.append(st)
        self.m_lanes = _lane_fold(jnp.maximum, self.m_lanes, st)

    def row_max(self):
        return jnp.max(self.m_lanes, axis=-1, keepdims=True)


def _attn_units(q_ref, o_ref, sub, lockstep, load_keys, load_values, lam, subln_w, lam_init,
                fillers):
    units = [(slice(r * sub, (r + 1) * sub), slice(h * V_DIM, (h + 1) * V_DIM))
             for r in range(q_ref.shape[0] // sub) for h in range(q_ref.shape[1] // V_DIM)]
    groups = [units[i:i + lockstep] for i in range(0, len(units), lockstep)]
    n_slots = 3 * len(groups)
    pending = list(fillers)

    def fill(slot):
        done = len(fillers) - len(pending)
        for _ in range(len(fillers) * (slot + 1) // n_slots - done):
            pending.pop(0)()

    def scores(group):
        out = []
        for rows, cols in group:
            tiles = _ScoreTiles(q_ref[rows, cols], load_keys(cols))
            for t in range(len(tiles)):
                tiles.compute(t)
            out.append(tiles)
        return out

    nxt = scores(groups[0])
    for i, group in enumerate(groups):
        cur, nxt = nxt, (scores(groups[i + 1]) if i + 1 < len(groups) else None)
        fill(3 * i)
        ms = [tiles.row_max() for tiles in cur]
        es = [jnp.concatenate([jnp.exp2(st - m).astype(BF16) for st in tiles.tiles], axis=1)
              for tiles, m in zip(cur, ms)]
        fill(3 * i + 1)
        ols = []
        for e, (rows, cols) in zip(es, group):
            v = jnp.concatenate(load_values(cols), axis=0)
            v1 = jnp.concatenate([v, jnp.ones_like(v)], axis=1)
            ols.append(jnp.dot(e, v1, preferred_element_type=F32))
        fill(3 * i + 2)
        outs = []
        for ol in ols:
            l1, l2 = ol[:sub, V_DIM:V_DIM + 1], ol[sub:, V_DIM:V_DIM + 1]
            outs.append(ol[:sub, :V_DIM] * (1.0 / l1) - ol[sub:, :V_DIM] * (lam / l2))
        sq = [jnp.mean(o * o, axis=-1, keepdims=True) for o in outs]
        for o, s2, (rows, cols) in zip(outs, sq, group):
            o_ref[rows, cols] = (o * lax.rsqrt(s2 + EPS) * subln_w
                                 * (1.0 - lam_init)).astype(o_ref.dtype)


def _cast_riders(riders, n_steps, step_of):
    in_specs, out_specs, out_shape = [], [], []
    for w, axis in riders:
        blk = tuple(n // n_steps if a == axis else n for a, n in enumerate(w.shape))
        idx = lambda *g, axis=axis, nd=w.ndim: tuple(
            step_of(*g) if a == axis else 0 for a in range(nd))
        in_specs.append(pl.BlockSpec(blk, idx))
        out_specs.append(pl.BlockSpec(blk, idx))
        out_shape.append(jax.ShapeDtypeStruct(w.shape, BF16))
    return in_specs, out_specs, out_shape


def _cast_rider_fillers(refs, n, n_chunks):
    def chunk(src, dst, rows):
        def cast():
            dst[rows, :] = src[rows, :].astype(dst.dtype)
        return cast

    fillers = []
    for src, dst in zip(refs[:n], refs[n:]):
        k = min(n_chunks, src.shape[0] // BF16_ROWS)
        step = src.shape[0] // k
        fillers += [chunk(src, dst, slice(c * step, (c + 1) * step)) for c in range(k)]
    return fillers


def _attn_cached_kernel(q_ref, kn_ref, vn_ref, kct_ref, vc_ref, lq1, lk1, lq2, lk2, sw_ref,
                        *rest, lam_init, sub, n_riders):
    o_ref = rest[n_riders]
    n_units = (q_ref.shape[0] // sub) * (q_ref.shape[1] // V_DIM)
    fillers = _cast_rider_fillers(rest[:n_riders] + rest[n_riders + 1:], n_riders, n_units)
    lam = _lambda(lq1, lk1, lq2, lk2, lam_init)
    _attn_units(
        q_ref, o_ref, sub, 1,
        lambda cols: [(kct_ref[cols, :].astype(BF16), True), (kn_ref[:, cols], False)],
        lambda cols: [vc_ref[:, cols].astype(BF16), vn_ref[:, cols]],
        lam, sw_ref[...], lam_init, fillers)


def _attn_cached(q, k, v, cache_kt, cache_v, lams, subln_w, riders, *, n_batch, lam_init, tq,
                 sub, heads):
    m, width = q.shape
    seq = m // n_batch
    past = cache_kt.shape[2]
    nq = seq // tq
    w = heads * V_DIM
    n_groups = width // w
    vec = lambda n: pl.BlockSpec((1, n), lambda b, g, t: (0, 0))
    r_in, r_out, r_shape = _cast_riders(
        riders, n_batch * n_groups * nq, lambda b, g, t: (b * n_groups + g) * nq + t)
    return pl.pallas_call(
        functools.partial(_attn_cached_kernel, lam_init=lam_init, sub=sub,
                          n_riders=len(riders)),
        grid=(n_batch, n_groups, nq),
        in_specs=[pl.BlockSpec((tq, w), lambda b, g, t: (b * nq + t, g)),
                  pl.BlockSpec((seq, w), lambda b, g, t: (b, g)),
                  pl.BlockSpec((seq, w), lambda b, g, t: (b, g)),
                  pl.BlockSpec((None, w, past), lambda b, g, t: (b, g, 0)),
                  pl.BlockSpec((past, w), lambda b, g, t: (b, g)),
                  vec(HEAD_DIM), vec(HEAD_DIM), vec(HEAD_DIM), vec(HEAD_DIM), vec(V_DIM)] + r_in,
        out_specs=[pl.BlockSpec((tq, w), lambda b, g, t: (b * nq + t, g))] + r_out,
        out_shape=[jax.ShapeDtypeStruct((m, width), BF16)] + r_shape,
        compiler_params=_cparams(3),
        name="attn_latent",
    )(q, k, v, cache_kt, cache_v, *lams, subln_w, *[w for w, _ in riders])


def _attn_self_kernel(q_ref, k_ref, v_ref, lq1, lk1, lq2, lk2, sw_ref, c_ref, wa_ref, ba_ref,
                      *rest, lam_init, n_riders):
    o_ref, mod_ref = rest[n_riders], rest[n_riders + 1]
    fillers = _cast_rider_fillers(rest[:n_riders] + rest[n_riders + 2:], n_riders, 4)
    _adaln_slab(c_ref, wa_ref, ba_ref, mod_ref)
    lam = _lambda(lq1, lk1, lq2, lk2, lam_init)
    _attn_units(q_ref, o_ref, q_ref.shape[0], q_ref.shape[1] // V_DIM,
                lambda cols: [(k_ref[:, cols], False)],
                lambda cols: [v_ref[:, cols]],
                lam, sw_ref[...], lam_init, fillers)


def _attn_self(q, k, v, lams, subln_w, c_rows, w_ada, b_ada, first_col, riders, *, n_batch,
               lam_init):
    m, width = q.shape
    seq = m // n_batch
    d, n_mod = w_ada.shape
    tn = (n_mod - first_col) // n_batch
    blk = pl.BlockSpec((seq, width), lambda b: (b, 0))
    vec = lambda w: pl.BlockSpec((1, w), lambda b: (0, 0))
    a_in, a_out = _adaln_specs(d, tn, first_col // tn)
    r_in, r_out, r_shape = _cast_riders(riders, n_batch, lambda b: b)
    return pl.pallas_call(
        functools.partial(_attn_self_kernel, lam_init=lam_init, n_riders=len(riders)),
        grid=(n_batch,),
        in_specs=[blk, blk, blk,
                  vec(HEAD_DIM), vec(HEAD_DIM), vec(HEAD_DIM), vec(HEAD_DIM), vec(V_DIM)]
                 + a_in + r_in,
        out_specs=[blk, a_out] + r_out,
        out_shape=[jax.ShapeDtypeStruct((m, width), BF16),
                   jax.ShapeDtypeStruct((MOD_ROWS, n_mod - first_col), F32)] + r_shape,
        compiler_params=_cparams(1),
        name="attn_ctx",
    )(q, k, v, *lams, subln_w, c_rows, w_ada, b_ada, *[w for w, _ in riders])


def _outproj_kernel(att_ref, sgu_ref, wo_ref, x_ref, g1_ref, sh2_ref, sc2_ref, n2_ref,
                    y_ref, h_ref, *, sub):
    half = att_ref.shape[1]
    tm = x_ref.shape[0]
    n_sub = tm // sub

    def mix(r):
        rows = slice(r * sub, (r + 1) * sub)
        return (jnp.dot(att_ref[rows, :], wo_ref[:half], preferred_element_type=F32)
                + jnp.dot(sgu_ref[rows, :], wo_ref[half:], preferred_element_type=F32))

    nxt = mix(0)
    w2 = n2_ref[...] * (1.0 + sc2_ref[...])
    for r in range(n_sub):
        cur, nxt = nxt, (mix(r + 1) if r + 1 < n_sub else None)
        rows = slice(r * sub, (r + 1) * sub)
        y = x_ref[rows, :] + g1_ref[...] * cur
        y_ref[rows, :] = y
        ms = jnp.mean(y * y, axis=-1, keepdims=True)
        h_ref[rows, :] = (y * lax.rsqrt(ms + EPS) * w2 + sh2_ref[...]).astype(BF16)


def _outproj(att, sgu, w_o, x, mod, mod_row, norm2_w, *, tm, sub):
    m, d = x.shape
    half = att.shape[1]
    mod_spec = lambda which: pl.BlockSpec(
        (None, None, 1, d), lambda i: (mod_row(i), which, 0, 0))
    return pl.pallas_call(
        functools.partial(_outproj_kernel, sub=sub),
        grid=(m // tm,),
        in_specs=[pl.BlockSpec((tm, half), lambda i: (i, 0)),
                  pl.BlockSpec((tm, half), lambda i: (i, 0)),
                  pl.BlockSpec(w_o.shape, lambda i: (0, 0)),
                  pl.BlockSpec((tm, d), lambda i: (i, 0)),
                  mod_spec(0), mod_spec(1), mod_spec(2),
                  pl.BlockSpec((1, d), lambda i: (0, 0))],
        out_specs=[pl.BlockSpec((tm, d), lambda i: (i, 0))] * 2,
        out_shape=[jax.ShapeDtypeStruct((m, d), F32), jax.ShapeDtypeStruct((m, d), BF16)],
        compiler_params=_cparams(1),
        name="outproj",
    )(att, sgu, w_o, x, mod, mod, mod, norm2_w)


def _mlp_kernel(h_ref, w1_ref, w2_ref, y1_hbm, g2_ref, o_ref, y1_buf, y1_sem):
    i, f = pl.program_id(0), pl.program_id(1)
    tm = o_ref.shape[0]

    def y1_copy():
        rows = pl.ds(pl.multiple_of(i * tm, tm), tm)
        return pltpu.make_async_copy(y1_hbm.at[rows, :], y1_buf, y1_sem)

    last = pl.num_programs(1) - 1

    def part():
        hid = jnp.dot(h_ref[...], w1_ref[...], preferred_element_type=F32)
        hid = jnp.square(jnp.maximum(hid, 0.0)).astype(BF16)
        return jnp.dot(hid, w2_ref[...], preferred_element_type=F32)

    @pl.when(f == 0)
    def _():
        y1_copy().start()
        o_ref[...] = part()

    @pl.when(jnp.logical_and(f > 0, f < last))
    def _():
        o_ref[...] += part()

    @pl.when(f == last)
    def _():
        y1_copy().wait()
        o_ref[...] = y1_buf[...] + g2_ref[...] * (o_ref[...] + part())


def _mlp(h, w1, w2, y1, mod, mod_row, *, tm, tf):
    m, d = h.shape
    d_ff = w1.shape[1]
    assert d_ff // tf >= 2
    return pl.pallas_call(
        _mlp_kernel,
        grid=(m // tm, d_ff // tf),
        in_specs=[pl.BlockSpec((tm, d), lambda i, f: (i, 0)),
                  pl.BlockSpec((d, tf), lambda i, f: (0, f)),
                  pl.BlockSpec((tf, d), lambda i, f: (f, 0)),
                  pl.BlockSpec(memory_space=pl.ANY),
                  pl.BlockSpec((None, None, 1, d), lambda i, f: (mod_row(i), 3, 0, 0))],
        out_specs=pl.BlockSpec((tm, d), lambda i, f: (i, 0)),
        out_shape=jax.ShapeDtypeStruct((m, d), F32),
        scratch_shapes=[pltpu.VMEM((tm, d), F32), pltpu.SemaphoreType.DMA(())],
        compiler_params=_cparams(2),
        name="mlp",
    )(h, w1, w2, y1, mod)


def _rope_tables(n):
    rows = n // GRID_W
    r, col = jnp.meshgrid(jnp.arange(rows), jnp.arange(GRID_W), indexing="ij")
    r = r.reshape(-1).astype(F32)
    col = col.reshape(-1).astype(F32)
    n_freq = HEAD_DIM // 4
    freqs = ROPE_BASE ** (-jnp.arange(n_freq, dtype=F32) / n_freq)
    ang_r = r[:, None] * freqs
    ang_c = col[:, None] * freqs
    ang = jnp.concatenate([ang_r, ang_r, ang_c, ang_c], axis=-1)
    cos = jnp.tile(jnp.cos(ang), (1, LANES // HEAD_DIM))
    sin = jnp.tile(jnp.sin(ang), (1, LANES // HEAD_DIM))
    first = (jnp.arange(LANES) % (2 * n_freq)) < n_freq
    return cos, jnp.where(first, -sin, 0.0), jnp.where(first, 0.0, sin)


def _block_diag_ones(n, blk):
    idx = np.arange(n) // blk
    return jnp.asarray(idx[:, None] == idx[None, :], dtype=BF16)


def kernel(x_prompt, x_sample, cache_k, cache_v, c, c_ctx, w_ada, b_ada, norm1_w, norm2_w,
           w_in, q_norm_w, k_norm_w, lambda_q1, lambda_k1, lambda_q2, lambda_k2, subln_w,
           sgu_norm_w, w_s, b_s, w_o, w_ff1, w_ff2):
    n_ctx, ctx_len, d = x_prompt.shape
    n_lat, lat_len, _ = x_sample.shape
    depth = w_in.shape[0]
    past = cache_k.shape[2]
    width = w_o.shape[1] // 2
    n_heads = width // V_DIM
    tiles = _TILES
    assert n_lat + 1 <= MOD_ROWS and ctx_len % CHUNK == 0
    assert lat_len % tiles.tm == 0 and lat_len % tiles.tm_mlp == 0

    y_p = x_prompt.reshape(n_ctx * ctx_len, d)
    y_s = x_sample.reshape(n_lat * lat_len, d)
    rope_tabs = _rope_tables(lat_len)
    bd = _block_diag_ones(MXU_DIM, HEAD_DIM)
    c_rows = jnp.concatenate(
        [c_ctx[None, :], c, jnp.zeros((MOD_ROWS - 1 - n_lat, d), F32)], axis=0)
    ctx_row = lambda tm: (lambda i: 0)
    lat_row = lambda tm: (lambda i: 1 + i // (lat_len // tm))

    state_k, state_v = [], []
    for l in range(depth):
        lam_init = 0.8 - 0.6 * math.exp(-0.3 * l)
        w_in_l = w_in[l].astype(BF16)
        ws_l = w_s[l].astype(BF16)
        bs_l = b_s[l][:, :, None]
        row = lambda a: a[l].reshape(1, -1)
        qw = jnp.tile(row(q_norm_w), (1, width // HEAD_DIM))
        kw = jnp.tile(row(k_norm_w), (1, width // HEAD_DIM))
        lams = (row(lambda_q1), row(lambda_k1), row(lambda_q2), row(lambda_k2))

        b_ada_l = b_ada[l].reshape(1, -1)
        mod_in = _adaln(c_rows, w_ada[l], b_ada_l, N_MOD_IN * d).reshape(
            MOD_ROWS, N_MOD_IN, 1, d)

        def mixers(x, mod_row, tabs, state_seq):
            return _inproj(x, mod_in, mod_row(tiles.tm), row(norm1_w), w_in_l, qw, kw, bd, tabs,
                           row(sgu_norm_w), ws_l, bs_l, tm=tiles.tm, state_seq=state_seq)

        def finish(x, att, sgu, mod_row):
            y1, h2 = _outproj(att, sgu, w_o_l, x, mod_out, mod_row(tiles.tm), row(norm2_w),
                              tm=tiles.tm, sub=tiles.tm_sub)
            return _mlp(h2, w1_l, w2_l, y1, mod_out, mod_row(tiles.tm_mlp),
                        tm=tiles.tm_mlp, tf=tiles.tf)

        q_p, k_p, v_p, sgu_p, sk, sv = mixers(y_p, ctx_row, None, ctx_len)
        sk = sk.reshape(n_ctx, n_heads, 2, HEAD_DIM, ctx_len)
        state_k.append(jnp.transpose(sk, (0, 4, 1, 2, 3)))
        state_v.append(sv.reshape(n_ctx, ctx_len, n_heads, V_DIM))
        q_s, k_s, v_s, sgu_s = mixers(y_s, lat_row, rope_tabs, None)

        att_p, mod_out, w2_l = _attn_self(
            q_p, k_p, v_p, lams, row(subln_w), c_rows, w_ada[l], b_ada_l, N_MOD_IN * d,
            [(w_ff2[l], 0)], n_batch=n_ctx, lam_init=lam_init)
        mod_out = mod_out.reshape(MOD_ROWS, N_MOD_OUT, 1, d)
        cache_kt = jnp.transpose(cache_k[:, l], (0, 2, 3, 4, 1)).reshape(n_lat, width, past)
        att_s, w1_l, w_o_l = _attn_cached(
            q_s, k_s, v_s, cache_kt, cache_v[:, l].reshape(n_lat * past, width),
            lams, row(subln_w), [(w_ff1[l], 1), (w_o[l], 0)], n_batch=n_lat, lam_init=lam_init,
            tq=tiles.tq, sub=tiles.tq_sub, heads=n_heads)

        y_p = finish(y_p, att_p, sgu_p, ctx_row)
        y_s = finish(y_s, att_s, sgu_s, lat_row)

    return (y_p.reshape(n_ctx, ctx_len, d), y_s.reshape(n_lat, lat_len, d),
            jnp.stack(state_k, axis=1), jnp.stack(state_v, axis=1))
```

```python
import functools
import math
from typing import NamedTuple

import jax
import jax.numpy as jnp
import numpy as np
from jax import lax
from jax.experimental import pallas as pl
from jax.experimental.pallas import tpu as pltpu

F32 = jnp.float32
BF16 = jnp.bfloat16

HEAD_DIM = 64
V_DIM = 128
CHUNK = 128
GROUP = 128
GRID_W = 64
ROPE_BASE = 10000.0
EPS = 1e-6
N_MOD_IN = 2
N_MOD_OUT = 4
MOD_ROWS = 8
LANES = 128
MXU_DIM = 256
BF16_ROWS = 16
VMEM_LIMIT = 60 * 1024 * 1024
Q_SCALE = HEAD_DIM ** -0.5 * math.log2(math.e)


class _Tiles(NamedTuple):
    tm: int
    tm_sub: int
    tm_mlp: int
    tf: int
    tq: int
    tq_sub: int


_TILES = _Tiles(tm=512, tm_sub=256, tm_mlp=1024, tf=1024, tq=512, tq_sub=128)


def _cparams(n_axes):
    return pltpu.CompilerParams(
        dimension_semantics=("arbitrary",) * n_axes,
        vmem_limit_bytes=VMEM_LIMIT)


def _adaln_slab(c_ref, w_ref, b_ref, o_ref):
    c = c_ref[...]
    s = c * (1.0 / (1.0 + jnp.exp(-c)))
    o_ref[...] = jnp.dot(s.astype(BF16), w_ref[...].astype(BF16),
                         preferred_element_type=F32) + b_ref[...]


def _adaln_specs(d, tn, first_block):
    return ([pl.BlockSpec((MOD_ROWS, d), lambda *g: (0, 0)),
             pl.BlockSpec((d, tn), lambda *g: (0, first_block + g[-1])),
             pl.BlockSpec((1, tn), lambda *g: (0, first_block + g[-1]))],
            pl.BlockSpec((MOD_ROWS, tn), lambda *g: (0, g[-1])))


def _adaln(c_rows, w_ada, b_ada, n_cols, tn=1024):
    d = w_ada.shape[0]
    in_specs, out_spec = _adaln_specs(d, tn, 0)
    return pl.pallas_call(
        _adaln_slab,
        grid=(n_cols // tn,),
        in_specs=in_specs,
        out_specs=out_spec,
        out_shape=jax.ShapeDtypeStruct((MOD_ROWS, n_cols), F32),
        compiler_params=_cparams(1),
        name="adaln",
    )(c_rows, w_ada, b_ada)


def _gelu(x):
    return jax.nn.gelu(x, approximate=True)


def _head_norm(z, w_ref, bd_ref):
    outs = []
    for c in range(z.shape[1] // MXU_DIM):
        zc = z[:, c * MXU_DIM:(c + 1) * MXU_DIM]
        ss = jnp.dot((zc * zc).astype(BF16), bd_ref[...], preferred_element_type=F32)
        r = lax.rsqrt(ss * (1.0 / HEAD_DIM) + EPS)
        outs.append(zc * r * w_ref[:, c * MXU_DIM:(c + 1) * MXU_DIM])
    return outs


def _rope128(x, cos, sin_lo, sin_hi):
    quarter = HEAD_DIM // 4
    return (x * cos + pltpu.roll(x, LANES - quarter, 1) * sin_lo
            + pltpu.roll(x, quarter, 1) * sin_hi)


def _inproj_kernel(*refs, rope, emit_state):
    it = iter(refs)
    x_ref, sh_ref, sc_ref, n1_ref, w_ref, qw_ref, kw_ref, bd_ref = (next(it) for _ in range(8))
    if rope:
        cos_ref, slo_ref, shi_ref = (next(it) for _ in range(3))
    sgw_ref, ws_ref, bs_ref = (next(it) for _ in range(3))
    q_ref, k_ref, v_ref, sgu_ref = (next(it) for _ in range(4))
    if emit_state:
        sk_ref, sv_ref = (next(it) for _ in range(2))
    tm = x_ref.shape[0]
    width = q_ref.shape[1]

    x = x_ref[...]
    ms = jnp.mean(x * x, axis=-1, keepdims=True)
    h = (x * lax.rsqrt(ms + EPS) * (n1_ref[...] * (1.0 + sc_ref[...])) + sh_ref[...]).astype(BF16)

    def proj(j):
        return jnp.dot(h, w_ref[:, j * width:(j + 1) * width], preferred_element_type=F32)

    def qk_epilogue(z, w_norm_ref, out_ref, scale, state_ref):
        blocks = _head_norm(z, w_norm_ref, bd_ref)
        for c, n in enumerate(blocks):
            if state_ref is not None:
                seq = state_ref.shape[2]
                for s in range(tm // seq):
                    state_ref[s, c * MXU_DIM:(c + 1) * MXU_DIM, :] = n[s * seq:(s + 1) * seq].T
            for hh in range(MXU_DIM // LANES):
                xh = n[:, hh * LANES:(hh + 1) * LANES]
                if rope:
                    xh = _rope128(xh, cos_ref[...], slo_ref[...], shi_ref[...])
                if scale != 1.0:
                    xh = xh * scale
                lo = c * MXU_DIM + hh * LANES
                out_ref[:, lo:lo + LANES] = xh.astype(BF16)

    zq = proj(0)
    zk = proj(1)
    qk_epilogue(zq, qw_ref, q_ref, Q_SCALE, None)
    zv = proj(2)
    qk_epilogue(zk, kw_ref, k_ref, 1.0, sk_ref if emit_state else None)
    zu = proj(3)
    v_ref[...] = zv.astype(BF16)
    if emit_state:
        sv_ref[...] = zv
    zg = proj(4)
    gu = _gelu(zu)
    g = _gelu(zg)
    mu = jnp.mean(g, axis=-1, keepdims=True)
    gc = g - mu
    var = jnp.mean(gc * gc, axis=-1, keepdims=True)
    gn = (gc * lax.rsqrt(var + EPS) * sgw_ref[...]).astype(BF16)
    n_chunks = tm // CHUNK
    for grp in range(width // GROUP):
        cols = slice(grp * GROUP, (grp + 1) * GROUP)
        rhs = jnp.concatenate(
            [gn[c * CHUNK:(c + 1) * CHUNK, cols] for c in range(n_chunks)], axis=1)
        mixed = jnp.dot(ws_ref[grp], rhs, preferred_element_type=F32) + bs_ref[grp]
        for c in range(n_chunks):
            rows = slice(c * CHUNK, (c + 1) * CHUNK)
            sgu_ref[rows, cols] = (
                gu[rows, cols] * mixed[:, c * CHUNK:(c + 1) * CHUNK]).astype(BF16)


def _inproj(x, mod, mod_row, norm1_w, w_in, qw, kw, bd, rope_tabs, sgw, ws, bs,
            *, tm, state_seq):
    m, d = x.shape
    n_in = w_in.shape[1]
    width = n_in // 5
    rope = rope_tabs is not None
    const = lambda shape: pl.BlockSpec(shape, lambda i: (0,) * len(shape),
                                       pipeline_mode=pl.Buffered(1))
    mod_spec = lambda which: pl.BlockSpec(
        (None, None, 1, d), lambda i: (mod_row(i), which, 0, 0))
    in_specs = [
        pl.BlockSpec((tm, d), lambda i: (i, 0)),
        mod_spec(0), mod_spec(1), const((1, d)),
        const((d, n_in)),
        const((1, width)), const((1, width)),
        const((MXU_DIM, MXU_DIM)),
    ]
    args = [x, mod, mod, norm1_w, w_in, qw, kw, bd]
    if rope:
        seq_tiles = rope_tabs[0].shape[0] // tm
        tab_spec = pl.BlockSpec((tm, LANES), lambda i: (i % seq_tiles, 0))
        in_specs += [tab_spec] * 3
        args += list(rope_tabs)
    in_specs += [const((1, width)), const(ws.shape), const(bs.shape)]
    args += [sgw, ws, bs]
    out_spec = pl.BlockSpec((tm, width), lambda i: (i, 0))
    out_shape = [jax.ShapeDtypeStruct((m, width), BF16)] * 4
    out_specs = [out_spec] * 4
    emit_state = state_seq is not None
    if emit_state:
        out_shape += [jax.ShapeDtypeStruct((m // state_seq, width, state_seq), F32),
                      jax.ShapeDtypeStruct((m, width), F32)]
        out_specs += [pl.BlockSpec((tm // state_seq, width, state_seq), lambda i: (i, 0, 0)),
                      out_spec]
    return pl.pallas_call(
        functools.partial(_inproj_kernel, rope=rope, emit_state=emit_state),
        grid=(m // tm,),
        in_specs=in_specs,
        out_specs=out_specs,
        out_shape=out_shape,
        compiler_params=_cparams(1),
        name="inproj_rope" if rope else "inproj_ctx",
    )(*args)


def _lambda(lq1_ref, lk1_ref, lq2_ref, lk2_ref, lam_init):
    a = jnp.sum(lq1_ref[...] * lk1_ref[...], axis=-1, keepdims=True)
    b = jnp.sum(lq2_ref[...] * lk2_ref[...], axis=-1, keepdims=True)
    return jnp.exp(a) - jnp.exp(b) + lam_init


def _lane_fold(op, acc, tile):
    for c in range(0, tile.shape[1], LANES):
        blk = tile[:, c:c + LANES]
        acc = blk if acc is None else op(acc, blk)
    return acc


class _ScoreTiles:
    def __init__(self, q, keys):
        lane = lax.broadcasted_iota(jnp.int32, q.shape, 1)
        zero = jnp.zeros_like(q)
        self.q2 = jnp.concatenate([jnp.where(lane < HEAD_DIM, q, zero),
                                   jnp.where(lane >= HEAD_DIM, q, zero)], axis=0)
        self.slices = []
        for k, feature_major in keys:
            n_keys = k.shape[1] if feature_major else k.shape[0]
            self.slices += [(k, feature_major, lo) for lo in range(0, n_keys, MXU_DIM)]
        self.tiles = []
        self.m_lanes = None

    def __len__(self):
        return len(self.slices)

    def compute(self, t):
        k, feature_major, lo = self.slices[t]
        if feature_major:
            st = jnp.dot(self.q2, k[:, lo:lo + MXU_DIM], preferred_element_type=F32)
        else:
            st = lax.dot_general(self.q2, k[lo:lo + MXU_DIM], (((1,), (1,)), ((), ())),
                                 preferred_element_type=F32)
        self.tiles.append(st)
        self.m_lanes = _lane_fold(jnp.maximum, self.m_lanes, st)

    def row_max(self):
        return jnp.max(self.m_lanes, axis=-1, keepdims=True)


def _attn_units(q_ref, o_ref, sub, lockstep, load_keys, load_values, lam, subln_w, lam_init,
                fillers):
    units = [(slice(r * sub, (r + 1) * sub), slice(h * V_DIM, (h + 1) * V_DIM))
             for r in range(q_ref.shape[0] // sub) for h in range(q_ref.shape[1] // V_DIM)]
    groups = [units[i:i + lockstep] for i in range(0, len(units), lockstep)]
    n_slots = 3 * len(groups)
    pending = list(fillers)

    def fill(slot):
        done = len(fillers) - len(pending)
        for _ in range(len(fillers) * (slot + 1) // n_slots - done):
            pending.pop(0)()

    def scores(group):
        out = []
        for rows, cols in group:
            tiles = _ScoreTiles(q_ref[rows, cols], load_keys(cols))
            for t in range(len(tiles)):
                tiles.compute(t)
            out.append(tiles)
        return out

    nxt = scores(groups[0])
    for i, group in enumerate(groups):
        cur, nxt = nxt, (scores(groups[i + 1]) if i + 1 < len(groups) else None)
        fill(3 * i)
        ms = [tiles.row_max() for tiles in cur]
        es = [jnp.concatenate([jnp.exp2(st - m).astype(BF16) for st in tiles.tiles], axis=1)
              for tiles, m in zip(cur, ms)]
        fill(3 * i + 1)
        ols = []
        for e, (rows, cols) in zip(es, group):
            v = jnp.concatenate(load_values(cols), axis=0)
            v1 = jnp.concatenate([v, jnp.ones_like(v)], axis=1)
            ols.append(jnp.dot(e, v1, preferred_element_type=F32))
        fill(3 * i + 2)
        outs = []
        for ol in ols:
            l1, l2 = ol[:sub, V_DIM:V_DIM + 1], ol[sub:, V_DIM:V_DIM + 1]
            outs.append(ol[:sub, :V_DIM] * (1.0 / l1) - ol[sub:, :V_DIM] * (lam / l2))
        sq = [jnp.mean(o * o, axis=-1, keepdims=True) for o in outs]
        for o, s2, (rows, cols) in zip(outs, sq, group):
            o_ref[rows, cols] = (o * lax.rsqrt(s2 + EPS) * subln_w
                                 * (1.0 - lam_init)).astype(o_ref.dtype)


def _cast_riders(riders, n_steps, step_of):
    in_specs, out_specs, out_shape = [], [], []
    for w, axis in riders:
        blk = tuple(n // n_steps if a == axis else n for a, n in enumerate(w.shape))
        idx = lambda *g, axis=axis, nd=w.ndim: tuple(
            step_of(*g) if a == axis else 0 for a in range(nd))
        in_specs.append(pl.BlockSpec(blk, idx))
        out_specs.append(pl.BlockSpec(blk, idx))
        out_shape.append(jax.ShapeDtypeStruct(w.shape, BF16))
    return in_specs, out_specs, out_shape


def _cast_rider_fillers(refs, n, n_chunks):
    def chunk(src, dst, rows):
        def cast():
            dst[rows, :] = src[rows, :].astype(dst.dtype)
        return cast

    fillers = []
    for src, dst in zip(refs[:n], refs[n:]):
        k = min(n_chunks, src.shape[0] // BF16_ROWS)
        step = src.shape[0] // k
        fillers += [chunk(src, dst, slice(c * step, (c + 1) * step)) for c in range(k)]
    return fillers


def _attn_cached_kernel(q_ref, kn_ref, vn_ref, kct_ref, vc_ref, lq1, lk1, lq2, lk2, sw_ref,
                        *rest, lam_init, sub, n_riders):
    o_ref = rest[n_riders]
    n_units = (q_ref.shape[0] // sub) * (q_ref.shape[1] // V_DIM)
    fillers = _cast_rider_fillers(rest[:n_riders] + rest[n_riders + 1:], n_riders, n_units)
    lam = _lambda(lq1, lk1, lq2, lk2, lam_init)
    _attn_units(
        q_ref, o_ref, sub, 1,
        lambda cols: [(kct_ref[cols, :].astype(BF16), True), (kn_ref[:, cols], False)],
        lambda cols: [vc_ref[:, cols].astype(BF16), vn_ref[:, cols]],
        lam, sw_ref[...], lam_init, fillers)


def _attn_cached(q, k, v, cache_kt, cache_v, lams, subln_w, riders, *, n_batch, lam_init, tq,
                 sub, heads):
    m, width = q.shape
    seq = m // n_batch
    past = cache_kt.shape[2]
    nq = seq // tq
    w = heads * V_DIM
    n_groups = width // w
    vec = lambda n: pl.BlockSpec((1, n), lambda b, g, t: (0, 0))
    r_in, r_out, r_shape = _cast_riders(
        riders, n_batch * n_groups * nq, lambda b, g, t: (b * n_groups + g) * nq + t)
    return pl.pallas_call(
        functools.partial(_attn_cached_kernel, lam_init=lam_init, sub=sub,
                          n_riders=len(riders)),
        grid=(n_batch, n_groups, nq),
        in_specs=[pl.BlockSpec((tq, w), lambda b, g, t: (b * nq + t, g)),
                  pl.BlockSpec((seq, w), lambda b, g, t: (b, g)),
                  pl.BlockSpec((seq, w), lambda b, g, t: (b, g)),
                  pl.BlockSpec((None, w, past), lambda b, g, t: (b, g, 0)),
                  pl.BlockSpec((past, w), lambda b, g, t: (b, g)),
                  vec(HEAD_DIM), vec(HEAD_DIM), vec(HEAD_DIM), vec(HEAD_DIM), vec(V_DIM)] + r_in,
        out_specs=[pl.BlockSpec((tq, w), lambda b, g, t: (b * nq + t, g))] + r_out,
        out_shape=[jax.ShapeDtypeStruct((m, width), BF16)] + r_shape,
        compiler_params=_cparams(3),
        name="attn_latent",
    )(q, k, v, cache_kt, cache_v, *lams, subln_w, *[w for w, _ in riders])


def _attn_self_kernel(q_ref, k_ref, v_ref, lq1, lk1, lq2, lk2, sw_ref, c_ref, wa_ref, ba_ref,
                      *rest, lam_init, n_riders):
    o_ref, mod_ref = rest[n_riders], rest[n_riders + 1]
    fillers = _cast_rider_fillers(rest[:n_riders] + rest[n_riders + 2:], n_riders, 4)
    _adaln_slab(c_ref, wa_ref, ba_ref, mod_ref)
    lam = _lambda(lq1, lk1, lq2, lk2, lam_init)
    _attn_units(q_ref, o_ref, q_ref.shape[0], q_ref.shape[1] // V_DIM,
                lambda cols: [(k_ref[:, cols], False)],
                lambda cols: [v_ref[:, cols]],
                lam, sw_ref[...], lam_init, fillers)


def _attn_self(q, k, v, lams, subln_w, c_rows, w_ada, b_ada, first_col, riders, *, n_batch,
               lam_init):
    m, width = q.shape
    seq = m // n_batch
    d, n_mod = w_ada.shape
    tn = (n_mod - first_col) // n_batch
    blk = pl.BlockSpec((seq, width), lambda b: (b, 0))
    vec = lambda w: pl.BlockSpec((1, w), lambda b: (0, 0))
    a_in, a_out = _adaln_specs(d, tn, first_col // tn)
    r_in, r_out, r_shape = _cast_riders(riders, n_batch, lambda b: b)
    return pl.pallas_call(
        functools.partial(_attn_self_kernel, lam_init=lam_init, n_riders=len(riders)),
        grid=(n_batch,),
        in_specs=[blk, blk, blk,
                  vec(HEAD_DIM), vec(HEAD_DIM), vec(HEAD_DIM), vec(HEAD_DIM), vec(V_DIM)]
                 + a_in + r_in,
        out_specs=[blk, a_out] + r_out,
        out_shape=[jax.ShapeDtypeStruct((m, width), BF16),
                   jax.ShapeDtypeStruct((MOD_ROWS, n_mod - first_col), F32)] + r_shape,
        compiler_params=_cparams(1),
        name="attn_ctx",
    )(q, k, v, *lams, subln_w, c_rows, w_ada, b_ada, *[w for w, _ in riders])


def _outproj_kernel(att_ref, sgu_ref, wo_ref, x_ref, g1_ref, sh2_ref, sc2_ref, n2_ref,
                    y_ref, h_ref, *, sub):
    half = att_ref.shape[1]
    tm = x_ref.shape[0]
    n_sub = tm // sub

    def mix(r):
        rows = slice(r * sub, (r + 1) * sub)
        return (jnp.dot(att_ref[rows, :], wo_ref[:half], preferred_element_type=F32)
                + jnp.dot(sgu_ref[rows, :], wo_ref[half:], preferred_element_type=F32))

    nxt = mix(0)
    w2 = n2_ref[...] * (1.0 + sc2_ref[...])
    for r in range(n_sub):
        cur, nxt = nxt, (mix(r + 1) if r + 1 < n_sub else None)
        rows = slice(r * sub, (r + 1) * sub)
        y = x_ref[rows, :] + g1_ref[...] * cur
        y_ref[rows, :] = y
        ms = jnp.mean(y * y, axis=-1, keepdims=True)
        h_ref[rows, :] = (y * lax.rsqrt(ms + EPS) * w2 + sh2_ref[...]).astype(BF16)


def _outproj(att, sgu, w_o, x, mod, mod_row, norm2_w, *, tm, sub):
    m, d = x.shape
    half = att.shape[1]
    mod_spec = lambda which: pl.BlockSpec(
        (None, None, 1, d), lambda i: (mod_row(i), which, 0, 0))
    return pl.pallas_call(
        functools.partial(_outproj_kernel, sub=sub),
        grid=(m // tm,),
        in_specs=[pl.BlockSpec((tm, half), lambda i: (i, 0)),
                  pl.BlockSpec((tm, half), lambda i: (i, 0)),
                  pl.BlockSpec(w_o.shape, lambda i: (0, 0)),
                  pl.BlockSpec((tm, d), lambda i: (i, 0)),
                  mod_spec(0), mod_spec(1), mod_spec(2),
                  pl.BlockSpec((1, d), lambda i: (0, 0))],
        out_specs=[pl.BlockSpec((tm, d), lambda i: (i, 0))] * 2,
        out_shape=[jax.ShapeDtypeStruct((m, d), F32), jax.ShapeDtypeStruct((m, d), BF16)],
        compiler_params=_cparams(1),
        name="outproj",
    )(att, sgu, w_o, x, mod, mod, mod, norm2_w)


def _mlp_kernel(h_ref, w1_ref, w2_ref, y1_hbm, g2_ref, o_ref, y1_buf, y1_sem):
    i, f = pl.program_id(0), pl.program_id(1)
    tm = o_ref.shape[0]

    def y1_copy():
        rows = pl.ds(pl.multiple_of(i * tm, tm), tm)
        return pltpu.make_async_copy(y1_hbm.at[rows, :], y1_buf, y1_sem)

    last = pl.num_programs(1) - 1

    def part():
        hid = jnp.dot(h_ref[...], w1_ref[...], preferred_element_type=F32)
        hid = jnp.square(jnp.maximum(hid, 0.0)).astype(BF16)
        return jnp.dot(hid, w2_ref[...], preferred_element_type=F32)

    @pl.when(f == 0)
    def _():
        y1_copy().start()
        o_ref[...] = part()

    @pl.when(jnp.logical_and(f > 0, f < last))
    def _():
        o_ref[...] += part()

    @pl.when(f == last)
    def _():
        y1_copy().wait()
        o_ref[...] = y1_buf[...] + g2_ref[...] * (o_ref[...] + part())


def _mlp(h, w1, w2, y1, mod, mod_row, *, tm, tf):
    m, d = h.shape
    d_ff = w1.shape[1]
    assert d_ff // tf >= 2
    return pl.pallas_call(
        _mlp_kernel,
        grid=(m // tm, d_ff // tf),
        in_specs=[pl.BlockSpec((tm, d), lambda i, f: (i, 0)),
                  pl.BlockSpec((d, tf), lambda i, f: (0, f)),
                  pl.BlockSpec((tf, d), lambda i, f: (f, 0)),
                  pl.BlockSpec(memory_space=pl.ANY),
                  pl.BlockSpec((None, None, 1, d), lambda i, f: (mod_row(i), 3, 0, 0))],
        out_specs=pl.BlockSpec((tm, d), lambda i, f: (i, 0)),
        out_shape=jax.ShapeDtypeStruct((m, d), F32),
        scratch_shapes=[pltpu.VMEM((tm, d), F32), pltpu.SemaphoreType.DMA(())],
        compiler_params=_cparams(2),
        name="mlp",
    )(h, w1, w2, y1, mod)


def _rope_tables(n):
    rows = n // GRID_W
    r, col = np.meshgrid(np.arange(rows), np.arange(GRID_W), indexing="ij")
    r = r.reshape(-1).astype(np.float32)
    col = col.reshape(-1).astype(np.float32)
    n_freq = HEAD_DIM // 4
    freqs = np.float32(ROPE_BASE) ** (-np.arange(n_freq, dtype=np.float32) / np.float32(n_freq))
    ang_r = r[:, None] * freqs
    ang_c = col[:, None] * freqs
    ang = np.concatenate([ang_r, ang_r, ang_c, ang_c], axis=-1).astype(np.float32)
    cos = np.tile(np.cos(ang), (1, LANES // HEAD_DIM)).astype(np.float32)
    sin = np.tile(np.sin(ang), (1, LANES // HEAD_DIM)).astype(np.float32)
    first = (np.arange(LANES) % (2 * n_freq)) < n_freq
    zero = np.float32(0.0)
    return (jnp.asarray(cos), jnp.asarray(np.where(first, -sin, zero)),
            jnp.asarray(np.where(first, zero, sin)))


def _block_diag_ones(n, blk):
    idx = np.arange(n) // blk
    return jnp.asarray(idx[:, None] == idx[None, :], dtype=BF16)


def kernel(x_prompt, x_sample, cache_k, cache_v, c, c_ctx, w_ada, b_ada, norm1_w, norm2_w,
           w_in, q_norm_w, k_norm_w, lambda_q1, lambda_k1, lambda_q2, lambda_k2, subln_w,
           sgu_norm_w, w_s, b_s, w_o, w_ff1, w_ff2):
    n_ctx, ctx_len, d = x_prompt.shape
    n_lat, lat_len, _ = x_sample.shape
    depth = w_in.shape[0]
    past = cache_k.shape[2]
    width = w_o.shape[1] // 2
    n_heads = width // V_DIM
    tiles = _TILES
    assert n_lat + 1 <= MOD_ROWS and ctx_len % CHUNK == 0
    assert lat_len % tiles.tm == 0 and lat_len % tiles.tm_mlp == 0

    y_p = x_prompt.reshape(n_ctx * ctx_len, d)
    y_s = x_sample.reshape(n_lat * lat_len, d)
    rope_tabs = _rope_tables(lat_len)
    bd = _block_diag_ones(MXU_DIM, HEAD_DIM)
    c_rows = jnp.concatenate(
        [c_ctx[None, :], c, jnp.zeros((MOD_ROWS - 1 - n_lat, d), F32)], axis=0)
    ctx_row = lambda tm: (lambda i: 0)
    lat_row = lambda tm: (lambda i: 1 + i // (lat_len // tm))

    state_k, state_v = [], []
    for l in range(depth):
        lam_init = 0.8 - 0.6 * math.exp(-0.3 * l)
        w_in_l = w_in[l].astype(BF16)
        ws_l = w_s[l].astype(BF16)
        bs_l = b_s[l][:, :, None]
        row = lambda a: a[l].reshape(1, -1)
        qw = jnp.tile(row(q_norm_w), (1, width // HEAD_DIM))
        kw = jnp.tile(row(k_norm_w), (1, width // HEAD_DIM))
        lams = (row(lambda_q1), row(lambda_k1), row(lambda_q2), row(lambda_k2))

        b_ada_l = b_ada[l].reshape(1, -1)
        mod_in = _adaln(c_rows, w_ada[l], b_ada_l, N_MOD_IN * d).reshape(
            MOD_ROWS, N_MOD_IN, 1, d)

        def mixers(x, mod_row, tabs, state_seq):
            return _inproj(x, mod_in, mod_row(tiles.tm), row(norm1_w), w_in_l, qw, kw, bd, tabs,
                           row(sgu_norm_w), ws_l, bs_l, tm=tiles.tm, state_seq=state_seq)

        def finish(x, att, sgu, mod_row):
            y1, h2 = _outproj(att, sgu, w_o_l, x, mod_out, mod_row(tiles.tm), row(norm2_w),
                              tm=tiles.tm, sub=tiles.tm_sub)
            return _mlp(h2, w1_l, w2_l, y1, mod_out, mod_row(tiles.tm_mlp),
                        tm=tiles.tm_mlp, tf=tiles.tf)

        q_p, k_p, v_p, sgu_p, sk, sv = mixers(y_p, ctx_row, None, ctx_len)
        sk = sk.reshape(n_ctx, n_heads, 2, HEAD_DIM, ctx_len)
        state_k.append(jnp.transpose(sk, (0, 4, 1, 2, 3)))
        state_v.append(sv.reshape(n_ctx, ctx_len, n_heads, V_DIM))
        q_s, k_s, v_s, sgu_s = mixers(y_s, lat_row, rope_tabs, None)

        att_p, mod_out, w2_l = _attn_self(
            q_p, k_p, v_p, lams, row(subln_w), c_rows, w_ada[l], b_ada_l, N_MOD_IN * d,
            [(w_ff2[l], 0)], n_batch=n_ctx, lam_init=lam_init)
        mod_out = mod_out.reshape(MOD_ROWS, N_MOD_OUT, 1, d)
        cache_kt = jnp.transpose(cache_k[:, l], (0, 2, 3, 4, 1)).reshape(n_lat, width, past)
        att_s, w1_l, w_o_l = _attn_cached(
            q_s, k_s, v_s, cache_kt, cache_v[:, l].reshape(n_lat * past, width),
            lams, row(subln_w), [(w_ff1[l], 1), (w_o[l], 0)], n_batch=n_lat, lam_init=lam_init,
            tq=tiles.tq, sub=tiles.tq_sub, heads=n_heads)

        y_p = finish(y_p, att_p, sgu_p, ctx_row)
        y_s = finish(y_s, att_s, sgu_s, lat_row)

    return (y_p.reshape(n_ctx, ctx_len, d), y_s.reshape(n_lat, lat_len, d),
            jnp.stack(state_k, axis=1), jnp.stack(state_v, axis=1))
```

```python
import functools
import math
from typing import NamedTuple

import jax
import jax.numpy as jnp
import numpy as np
from jax import lax
from jax.experimental import pallas as pl
from jax.experimental.pallas import tpu as pltpu

F32 = jnp.float32
BF16 = jnp.bfloat16

HEAD_DIM = 64
V_DIM = 128
CHUNK = 128
GROUP = 128
GRID_W = 64
ROPE_BASE = 10000.0
EPS = 1e-6
N_MOD_IN = 2
N_MOD_OUT = 4
MOD_ROWS = 8
LANES = 128
MXU_DIM = 256
BF16_ROWS = 16
VMEM_LIMIT = 60 * 1024 * 1024
Q_SCALE = HEAD_DIM ** -0.5 * math.log2(math.e)


class _Tiles(NamedTuple):
    tm: int
    tm_sub: int
    tm_mlp: int
    tf: int
    tq: int
    tq_sub: int


_TILES = _Tiles(tm=512, tm_sub=256, tm_mlp=1024, tf=1024, tq=512, tq_sub=256)


def _cparams(n_axes):
    return pltpu.CompilerParams(
        dimension_semantics=("arbitrary",) * n_axes,
        vmem_limit_bytes=VMEM_LIMIT)


def _adaln_slab(c_ref, w_ref, b_ref, o_ref):
    c = c_ref[...]
    s = c * (1.0 / (1.0 + jnp.exp(-c)))
    o_ref[...] = jnp.dot(s.astype(BF16), w_ref[...].astype(BF16),
                         preferred_element_type=F32) + b_ref[...]


def _adaln_specs(d, tn, first_block):
    return ([pl.BlockSpec((MOD_ROWS, d), lambda *g: (0, 0)),
             pl.BlockSpec((d, tn), lambda *g: (0, first_block + g[-1])),
             pl.BlockSpec((1, tn), lambda *g: (0, first_block + g[-1]))],
            pl.BlockSpec((MOD_ROWS, tn), lambda *g: (0, g[-1])))


def _adaln(c_rows, w_ada, b_ada, n_cols, tn=1024):
    d = w_ada.shape[0]
    in_specs, out_spec = _adaln_specs(d, tn, 0)
    return pl.pallas_call(
        _adaln_slab,
        grid=(n_cols // tn,),
        in_specs=in_specs,
        out_specs=out_spec,
        out_shape=jax.ShapeDtypeStruct((MOD_ROWS, n_cols), F32),
        compiler_params=_cparams(1),
        name="adaln",
    )(c_rows, w_ada, b_ada)


def _gelu(x):
    return jax.nn.gelu(x, approximate=True)


def _head_norm(z, w_ref, bd_ref):
    outs = []
    for c in range(z.shape[1] // MXU_DIM):
        zc = z[:, c * MXU_DIM:(c + 1) * MXU_DIM]
        ss = jnp.dot((zc * zc).astype(BF16), bd_ref[...], preferred_element_type=F32)
        r = lax.rsqrt(ss * (1.0 / HEAD_DIM) + EPS)
        outs.append(zc * r * w_ref[:, c * MXU_DIM:(c + 1) * MXU_DIM])
    return outs


def _rope128(x, cos, sin_lo, sin_hi):
    quarter = HEAD_DIM // 4
    return (x * cos + pltpu.roll(x, LANES - quarter, 1) * sin_lo
            + pltpu.roll(x, quarter, 1) * sin_hi)


def _inproj_kernel(*refs, rope, emit_state):
    it = iter(refs)
    x_ref, sh_ref, sc_ref, n1_ref, w_ref, qw_ref, kw_ref, bd_ref = (next(it) for _ in range(8))
    if rope:
        cos_ref, slo_ref, shi_ref = (next(it) for _ in range(3))
    sgw_ref, ws_ref, bs_ref = (next(it) for _ in range(3))
    q_ref, k_ref, v_ref, sgu_ref = (next(it) for _ in range(4))
    if emit_state:
        sk_ref, sv_ref = (next(it) for _ in range(2))
    tm = x_ref.shape[0]
    width = q_ref.shape[1]

    x = x_ref[...]
    ms = jnp.mean(x * x, axis=-1, keepdims=True)
    h = (x * lax.rsqrt(ms + EPS) * (n1_ref[...] * (1.0 + sc_ref[...])) + sh_ref[...]).astype(BF16)

    def proj(j):
        return jnp.dot(h, w_ref[:, j * width:(j + 1) * width], preferred_element_type=F32)

    def qk_epilogue(z, w_norm_ref, out_ref, scale, state_ref):
        blocks = _head_norm(z, w_norm_ref, bd_ref)
        for c, n in enumerate(blocks):
            if state_ref is not None:
                seq = state_ref.shape[2]
                for s in range(tm // seq):
                    state_ref[s, c * MXU_DIM:(c + 1) * MXU_DIM, :] = n[s * seq:(s + 1) * seq].T
            for hh in range(MXU_DIM // LANES):
                xh = n[:, hh * LANES:(hh + 1) * LANES]
                if rope:
                    xh = _rope128(xh, cos_ref[...], slo_ref[...], shi_ref[...])
                if scale != 1.0:
                    xh = xh * scale
                lo = c * MXU_DIM + hh * LANES
                out_ref[:, lo:lo + LANES] = xh.astype(BF16)

    zq = proj(0)
    zk = proj(1)
    qk_epilogue(zq, qw_ref, q_ref, Q_SCALE, None)
    zv = proj(2)
    qk_epilogue(zk, kw_ref, k_ref, 1.0, sk_ref if emit_state else None)
    zu = proj(3)
    v_ref[...] = zv.astype(BF16)
    if emit_state:
        sv_ref[...] = zv
    zg = proj(4)
    gu = _gelu(zu)
    g = _gelu(zg)
    mu = jnp.mean(g, axis=-1, keepdims=True)
    gc = g - mu
    var = jnp.mean(gc * gc, axis=-1, keepdims=True)
    gn = (gc * lax.rsqrt(var + EPS) * sgw_ref[...]).astype(BF16)
    n_chunks = tm // CHUNK
    for grp in range(width // GROUP):
        cols = slice(grp * GROUP, (grp + 1) * GROUP)
        rhs = jnp.concatenate(
            [gn[c * CHUNK:(c + 1) * CHUNK, cols] for c in range(n_chunks)], axis=1)
        mixed = jnp.dot(ws_ref[grp], rhs, preferred_element_type=F32) + bs_ref[grp]
        for c in range(n_chunks):
            rows = slice(c * CHUNK, (c + 1) * CHUNK)
            sgu_ref[rows, cols] = (
                gu[rows, cols] * mixed[:, c * CHUNK:(c + 1) * CHUNK]).astype(BF16)


def _inproj(x, mod, mod_row, norm1_w, w_in, qw, kw, bd, rope_tabs, sgw, ws, bs,
            *, tm, state_seq):
    m, d = x.shape
    n_in = w_in.shape[1]
    width = n_in // 5
    rope = rope_tabs is not None
    const = lambda shape: pl.BlockSpec(shape, lambda i: (0,) * len(shape),
                                       pipeline_mode=pl.Buffered(1))
    mod_spec = lambda which: pl.BlockSpec(
        (None, None, 1, d), lambda i: (mod_row(i), which, 0, 0))
    in_specs = [
        pl.BlockSpec((tm, d), lambda i: (i, 0)),
        mod_spec(0), mod_spec(1), const((1, d)),
        const((d, n_in)),
        const((1, width)), const((1, width)),
        const((MXU_DIM, MXU_DIM)),
    ]
    args = [x, mod, mod, norm1_w, w_in, qw, kw, bd]
    if rope:
        seq_tiles = rope_tabs[0].shape[0] // tm
        tab_spec = pl.BlockSpec((tm, LANES), lambda i: (i % seq_tiles, 0))
        in_specs += [tab_spec] * 3
        args += list(rope_tabs)
    in_specs += [const((1, width)), const(ws.shape), const(bs.shape)]
    args += [sgw, ws, bs]
    out_spec = pl.BlockSpec((tm, width), lambda i: (i, 0))
    out_shape = [jax.ShapeDtypeStruct((m, width), BF16)] * 4
    out_specs = [out_spec] * 4
    emit_state = state_seq is not None
    if emit_state:
        out_shape += [jax.ShapeDtypeStruct((m // state_seq, width, state_seq), F32),
                      jax.ShapeDtypeStruct((m, width), F32)]
        out_specs += [pl.BlockSpec((tm // state_seq, width, state_seq), lambda i: (i, 0, 0)),
                      out_spec]
    return pl.pallas_call(
        functools.partial(_inproj_kernel, rope=rope, emit_state=emit_state),
        grid=(m // tm,),
        in_specs=in_specs,
        out_specs=out_specs,
        out_shape=out_shape,
        compiler_params=_cparams(1),
        name="inproj_rope" if rope else "inproj_ctx",
    )(*args)


def _lambda(lq1_ref, lk1_ref, lq2_ref, lk2_ref, lam_init):
    a = jnp.sum(lq1_ref[...] * lk1_ref[...], axis=-1, keepdims=True)
    b = jnp.sum(lq2_ref[...] * lk2_ref[...], axis=-1, keepdims=True)
    return jnp.exp(a) - jnp.exp(b) + lam_init


def _lane_fold(op, acc, tile):
    for c in range(0, tile.shape[1], LANES):
        blk = tile[:, c:c + LANES]
        acc = blk if acc is None else op(acc, blk)
    return acc


class _ScoreTiles:
    def __init__(self, q, keys):
        lane = lax.broadcasted_iota(jnp.int32, q.shape, 1)
        zero = jnp.zeros_like(q)
        self.q2 = jnp.concatenate([jnp.where(lane < HEAD_DIM, q, zero),
                                   jnp.where(lane >= HEAD_DIM, q, zero)], axis=0)
        self.slices = []
        for k, feature_major in keys:
            n_keys = k.shape[1] if feature_major else k.shape[0]
            self.slices += [(k, feature_major, lo) for lo in range(0, n_keys, MXU_DIM)]
        self.tiles = []
        self.m_lanes = None

    def __len__(self):
        return len(self.slices)

    def compute(self, t):
        k, feature_major, lo = self.slices[t]
        if feature_major:
            st = jnp.dot(self.q2, k[:, lo:lo + MXU_DIM], preferred_element_type=F32)
        else:
            st = lax.dot_general(self.q2, k[lo:lo + MXU_DIM], (((1,), (1,)), ((), ())),
                                 preferred_element_type=F32)
        self.tiles.append(st)
        self.m_lanes = _lane_fold(jnp.maximum, self.m_lanes, st)

    def row_max(self):
        return jnp.max(self.m_lanes, axis=-1, keepdims=True)


def _attn_units(q_ref, o_ref, sub, lockstep, load_keys, load_values, lam, subln_w, lam_init,
                fillers):
    units = [(slice(r * sub, (r + 1) * sub), slice(h * V_DIM, (h + 1) * V_DIM))
             for r in range(q_ref.shape[0] // sub) for h in range(q_ref.shape[1] // V_DIM)]
    groups = [units[i:i + lockstep] for i in range(0, len(units), lockstep)]
    n_slots = 3 * len(groups)
    pending = list(fillers)

    def fill(slot):
        done = len(fillers) - len(pending)
        for _ in range(len(fillers) * (slot + 1) // n_slots - done):
            pending.pop(0)()

    def scores(group):
        out = []
        for rows, cols in group:
            tiles = _ScoreTiles(q_ref[rows, cols], load_keys(cols))
            for t in range(len(tiles)):
                tiles.compute(t)
            out.append(tiles)
        return out

    nxt = scores(groups[0])
    for i, group in enumerate(groups):
        cur, nxt = nxt, (scores(groups[i + 1]) if i + 1 < len(groups) else None)
        fill(3 * i)
        ms = [tiles.row_max() for tiles in cur]
        es = [jnp.concatenate([jnp.exp2(st - m).astype(BF16) for st in tiles.tiles], axis=1)
              for tiles, m in zip(cur, ms)]
        fill(3 * i + 1)
        ols = []
        for e, (rows, cols) in zip(es, group):
            v = jnp.concatenate(load_values(cols), axis=0)
            v1 = jnp.concatenate([v, jnp.ones_like(v)], axis=1)
            ols.append(jnp.dot(e, v1, preferred_element_type=F32))
        fill(3 * i + 2)
        outs = []
        for ol in ols:
            l1, l2 = ol[:sub, V_DIM:V_DIM + 1], ol[sub:, V_DIM:V_DIM + 1]
            outs.append(ol[:sub, :V_DIM] * (1.0 / l1) - ol[sub:, :V_DIM] * (lam / l2))
        sq = [jnp.mean(o * o, axis=-1, keepdims=True) for o in outs]
        for o, s2, (rows, cols) in zip(outs, sq, group):
            o_ref[rows, cols] = (o * lax.rsqrt(s2 + EPS) * subln_w
                                 * (1.0 - lam_init)).astype(o_ref.dtype)


def _cast_riders(riders, n_steps, step_of):
    in_specs, out_specs, out_shape = [], [], []
    for w, axis in riders:
        blk = tuple(n // n_steps if a == axis else n for a, n in enumerate(w.shape))
        idx = lambda *g, axis=axis, nd=w.ndim: tuple(
            step_of(*g) if a == axis else 0 for a in range(nd))
        in_specs.append(pl.BlockSpec(blk, idx))
        out_specs.append(pl.BlockSpec(blk, idx))
        out_shape.append(jax.ShapeDtypeStruct(w.shape, BF16))
    return in_specs, out_specs, out_shape


def _cast_rider_fillers(refs, n, n_chunks):
    def chunk(src, dst, rows):
        def cast():
            dst[rows, :] = src[rows, :].astype(dst.dtype)
        return cast

    fillers = []
    for src, dst in zip(refs[:n], refs[n:]):
        k = min(n_chunks, src.shape[0] // BF16_ROWS)
        step = src.shape[0] // k
        fillers += [chunk(src, dst, slice(c * step, (c + 1) * step)) for c in range(k)]
    return fillers


def _attn_cached_kernel(q_ref, kn_ref, vn_ref, kct_ref, vc_ref, lq1, lk1, lq2, lk2, sw_ref,
                        *rest, lam_init, sub, n_riders):
    o_ref = rest[n_riders]
    n_units = (q_ref.shape[0] // sub) * (q_ref.shape[1] // V_DIM)
    fillers = _cast_rider_fillers(rest[:n_riders] + rest[n_riders + 1:], n_riders, n_units)
    lam = _lambda(lq1, lk1, lq2, lk2, lam_init)
    _attn_units(
        q_ref, o_ref, sub, 1,
        lambda cols: [(kct_ref[cols, :].astype(BF16), True), (kn_ref[:, cols], False)],
        lambda cols: [vc_ref[:, cols].astype(BF16), vn_ref[:, cols]],
        lam, sw_ref[...], lam_init, fillers)


def _attn_cached(q, k, v, cache_kt, cache_v, lams, subln_w, riders, *, n_batch, lam_init, tq,
                 sub, heads):
    m, width = q.shape
    seq = m // n_batch
    past = cache_kt.shape[2]
    nq = seq // tq
    w = heads * V_DIM
    n_groups = width // w
    vec = lambda n: pl.BlockSpec((1, n), lambda b, g, t: (0, 0))
    r_in, r_out, r_shape = _cast_riders(
        riders, n_batch * n_groups * nq, lambda b, g, t: (b * n_groups + g) * nq + t)
    return pl.pallas_call(
        functools.partial(_attn_cached_kernel, lam_init=lam_init, sub=sub,
                          n_riders=len(riders)),
        grid=(n_batch, n_groups, nq),
        in_specs=[pl.BlockSpec((tq, w), lambda b, g, t: (b * nq + t, g)),
                  pl.BlockSpec((seq, w), lambda b, g, t: (b, g)),
                  pl.BlockSpec((seq, w), lambda b, g, t: (b, g)),
                  pl.BlockSpec((None, w, past), lambda b, g, t: (b, g, 0)),
                  pl.BlockSpec((past, w), lambda b, g, t: (b, g)),
                  vec(HEAD_DIM), vec(HEAD_DIM), vec(HEAD_DIM), vec(HEAD_DIM), vec(V_DIM)] + r_in,
        out_specs=[pl.BlockSpec((tq, w), lambda b, g, t: (b * nq + t, g))] + r_out,
        out_shape=[jax.ShapeDtypeStruct((m, width), BF16)] + r_shape,
        compiler_params=_cparams(3),
        name="attn_latent",
    )(q, k, v, cache_kt, cache_v, *lams, subln_w, *[w for w, _ in riders])


def _attn_self_kernel(q_ref, k_ref, v_ref, lq1, lk1, lq2, lk2, sw_ref, c_ref, wa_ref, ba_ref,
                      *rest, lam_init, n_riders):
    o_ref, mod_ref = rest[n_riders], rest[n_riders + 1]
    fillers = _cast_rider_fillers(rest[:n_riders] + rest[n_riders + 2:], n_riders, 4)
    _adaln_slab(c_ref, wa_ref, ba_ref, mod_ref)
    lam = _lambda(lq1, lk1, lq2, lk2, lam_init)
    _attn_units(q_ref, o_ref, q_ref.shape[0], q_ref.shape[1] // V_DIM,
                lambda cols: [(k_ref[:, cols], False)],
                lambda cols: [v_ref[:, cols]],
                lam, sw_ref[...], lam_init, fillers)


def _attn_self(q, k, v, lams, subln_w, c_rows, w_ada, b_ada, first_col, riders, *, n_batch,
               lam_init):
    m, width = q.shape
    seq = m // n_batch
    d, n_mod = w_ada.shape
    tn = (n_mod - first_col) // n_batch
    blk = pl.BlockSpec((seq, width), lambda b: (b, 0))
    vec = lambda w: pl.BlockSpec((1, w), lambda b: (0, 0))
    a_in, a_out = _adaln_specs(d, tn, first_col // tn)
    r_in, r_out, r_shape = _cast_riders(riders, n_batch, lambda b: b)
    return pl.pallas_call(
        functools.partial(_attn_self_kernel, lam_init=lam_init, n_riders=len(riders)),
        grid=(n_batch,),
        in_specs=[blk, blk, blk,
                  vec(HEAD_DIM), vec(HEAD_DIM), vec(HEAD_DIM), vec(HEAD_DIM), vec(V_DIM)]
                 + a_in + r_in,
        out_specs=[blk, a_out] + r_out,
        out_shape=[jax.ShapeDtypeStruct((m, width), BF16),
                   jax.ShapeDtypeStruct((MOD_ROWS, n_mod - first_col), F32)] + r_shape,
        compiler_params=_cparams(1),
        name="attn_ctx",
    )(q, k, v, *lams, subln_w, c_rows, w_ada, b_ada, *[w for w, _ in riders])


def _outproj_kernel(att_ref, sgu_ref, wo_ref, x_ref, g1_ref, sh2_ref, sc2_ref, n2_ref,
                    y_ref, h_ref, *, sub):
    half = att_ref.shape[1]
    tm = x_ref.shape[0]
    n_sub = tm // sub

    def mix(r):
        rows = slice(r * sub, (r + 1) * sub)
        return (jnp.dot(att_ref[rows, :], wo_ref[:half], preferred_element_type=F32)
                + jnp.dot(sgu_ref[rows, :], wo_ref[half:], preferred_element_type=F32))

    nxt = mix(0)
    w2 = n2_ref[...] * (1.0 + sc2_ref[...])
    for r in range(n_sub):
        cur, nxt = nxt, (mix(r + 1) if r + 1 < n_sub else None)
        rows = slice(r * sub, (r + 1) * sub)
        y = x_ref[rows, :] + g1_ref[...] * cur
        y_ref[rows, :] = y
        ms = jnp.mean(y * y, axis=-1, keepdims=True)
        h_ref[rows, :] = (y * lax.rsqrt(ms + EPS) * w2 + sh2_ref[...]).astype(BF16)


def _outproj(att, sgu, w_o, x, mod, mod_row, norm2_w, *, tm, sub):
    m, d = x.shape
    half = att.shape[1]
    mod_spec = lambda which: pl.BlockSpec(
        (None, None, 1, d), lambda i: (mod_row(i), which, 0, 0))
    return pl.pallas_call(
        functools.partial(_outproj_kernel, sub=sub),
        grid=(m // tm,),
        in_specs=[pl.BlockSpec((tm, half), lambda i: (i, 0)),
                  pl.BlockSpec((tm, half), lambda i: (i, 0)),
                  pl.BlockSpec(w_o.shape, lambda i: (0, 0)),
                  pl.BlockSpec((tm, d), lambda i: (i, 0)),
                  mod_spec(0), mod_spec(1), mod_spec(2),
                  pl.BlockSpec((1, d), lambda i: (0, 0))],
        out_specs=[pl.BlockSpec((tm, d), lambda i: (i, 0))] * 2,
        out_shape=[jax.ShapeDtypeStruct((m, d), F32), jax.ShapeDtypeStruct((m, d), BF16)],
        compiler_params=_cparams(1),
        name="outproj",
    )(att, sgu, w_o, x, mod, mod, mod, norm2_w)


def _mlp_kernel(h_ref, w1_ref, w2_ref, y1_hbm, g2_ref, o_ref, y1_buf, y1_sem):
    i, f = pl.program_id(0), pl.program_id(1)
    tm = o_ref.shape[0]

    def y1_copy():
        rows = pl.ds(pl.multiple_of(i * tm, tm), tm)
        return pltpu.make_async_copy(y1_hbm.at[rows, :], y1_buf, y1_sem)

    last = pl.num_programs(1) - 1

    def part():
        hid = jnp.dot(h_ref[...], w1_ref[...], preferred_element_type=F32)
        hid = jnp.square(jnp.maximum(hid, 0.0)).astype(BF16)
        return jnp.dot(hid, w2_ref[...], preferred_element_type=F32)

    @pl.when(f == 0)
    def _():
        y1_copy().start()
        o_ref[...] = part()

    @pl.when(jnp.logical_and(f > 0, f < last))
    def _():
        o_ref[...] += part()

    @pl.when(f == last)
    def _():
        y1_copy().wait()
        o_ref[...] = y1_buf[...] + g2_ref[...] * (o_ref[...] + part())


def _mlp(h, w1, w2, y1, mod, mod_row, *, tm, tf):
    m, d = h.shape
    d_ff = w1.shape[1]
    assert d_ff // tf >= 2
    return pl.pallas_call(
        _mlp_kernel,
        grid=(m // tm, d_ff // tf),
        in_specs=[pl.BlockSpec((tm, d), lambda i, f: (i, 0)),
                  pl.BlockSpec((d, tf), lambda i, f: (0, f)),
                  pl.BlockSpec((tf, d), lambda i, f: (f, 0)),
                  pl.BlockSpec(memory_space=pl.ANY),
                  pl.BlockSpec((None, None, 1, d), lambda i, f: (mod_row(i), 3, 0, 0))],
        out_specs=pl.BlockSpec((tm, d), lambda i, f: (i, 0)),
        out_shape=jax.ShapeDtypeStruct((m, d), F32),
        scratch_shapes=[pltpu.VMEM((tm, d), F32), pltpu.SemaphoreType.DMA(())],
        compiler_params=_cparams(2),
        name="mlp",
    )(h, w1, w2, y1, mod)


def _rope_tables(n):
    rows = n // GRID_W
    r, col = np.meshgrid(np.arange(rows), np.arange(GRID_W), indexing="ij")
    r = r.reshape(-1).astype(np.float32)
    col = col.reshape(-1).astype(np.float32)
    n_freq = HEAD_DIM // 4
    freqs = np.float32(ROPE_BASE) ** (-np.arange(n_freq, dtype=np.float32) / np.float32(n_freq))
    ang_r = r[:, None] * freqs
    ang_c = col[:, None] * freqs
    ang = np.concatenate([ang_r, ang_r, ang_c, ang_c], axis=-1).astype(np.float32)
    cos = np.tile(np.cos(ang), (1, LANES // HEAD_DIM)).astype(np.float32)
    sin = np.tile(np.sin(ang), (1, LANES // HEAD_DIM)).astype(np.float32)
    first = (np.arange(LANES) % (2 * n_freq)) < n_freq
    zero = np.float32(0.0)
    return (jnp.asarray(cos), jnp.asarray(np.where(first, -sin, zero)),
            jnp.asarray(np.where(first, zero, sin)))


def _block_diag_ones(n, blk):
    idx = np.arange(n) // blk
    return jnp.asarray(idx[:, None] == idx[None, :], dtype=BF16)


def kernel(x_prompt, x_sample, cache_k, cache_v, c, c_ctx, w_ada, b_ada, norm1_w, norm2_w,
           w_in, q_norm_w, k_norm_w, lambda_q1, lambda_k1, lambda_q2, lambda_k2, subln_w,
           sgu_norm_w, w_s, b_s, w_o, w_ff1, w_ff2):
    n_ctx, ctx_len, d = x_prompt.shape
    n_lat, lat_len, _ = x_sample.shape
    depth = w_in.shape[0]
    past = cache_k.shape[2]
    width = w_o.shape[1] // 2
    n_heads = width // V_DIM
    tiles = _TILES
    assert n_lat + 1 <= MOD_ROWS and ctx_len % CHUNK == 0
    assert lat_len % tiles.tm == 0 and lat_len % tiles.tm_mlp == 0

    y_p = x_prompt.reshape(n_ctx * ctx_len, d)
    y_s = x_sample.reshape(n_lat * lat_len, d)
    rope_tabs = _rope_tables(lat_len)
    bd = _block_diag_ones(MXU_DIM, HEAD_DIM)
    c_rows = jnp.concatenate(
        [c_ctx[None, :], c, jnp.zeros((MOD_ROWS - 1 - n_lat, d), F32)], axis=0)
    ctx_row = lambda tm: (lambda i: 0)
    lat_row = lambda tm: (lambda i: 1 + i // (lat_len // tm))

    state_k, state_v = [], []
    for l in range(depth):
        lam_init = 0.8 - 0.6 * math.exp(-0.3 * l)
        w_in_l = w_in[l].astype(BF16)
        ws_l = w_s[l].astype(BF16)
        bs_l = b_s[l][:, :, None]
        row = lambda a: a[l].reshape(1, -1)
        qw = jnp.tile(row(q_norm_w), (1, width // HEAD_DIM))
        kw = jnp.tile(row(k_norm_w), (1, width // HEAD_DIM))
        lams = (row(lambda_q1), row(lambda_k1), row(lambda_q2), row(lambda_k2))

        b_ada_l = b_ada[l].reshape(1, -1)
        mod_in = _adaln(c_rows, w_ada[l], b_ada_l, N_MOD_IN * d).reshape(
            MOD_ROWS, N_MOD_IN, 1, d)

        def mixers(x, mod_row, tabs, state_seq):
            return _inproj(x, mod_in, mod_row(tiles.tm), row(norm1_w), w_in_l, qw, kw, bd, tabs,
                           row(sgu_norm_w), ws_l, bs_l, tm=tiles.tm, state_seq=state_seq)

        def finish(x, att, sgu, mod_row):
            y1, h2 = _outproj(att, sgu, w_o_l, x, mod_out, mod_row(tiles.tm), row(norm2_w),
                              tm=tiles.tm, sub=tiles.tm_sub)
            return _mlp(h2, w1_l, w2_l, y1, mod_out, mod_row(tiles.tm_mlp),
                        tm=tiles.tm_mlp, tf=tiles.tf)

        q_p, k_p, v_p, sgu_p, sk, sv = mixers(y_p, ctx_row, None, ctx_len)
        sk = sk.reshape(n_ctx, n_heads, 2, HEAD_DIM, ctx_len)
        state_k.append(jnp.transpose(sk, (0, 4, 1, 2, 3)))
        state_v.append(sv.reshape(n_ctx, ctx_len, n_heads, V_DIM))
        q_s, k_s, v_s, sgu_s = mixers(y_s, lat_row, rope_tabs, None)

        att_p, mod_out, w2_l = _attn_self(
            q_p, k_p, v_p, lams, row(subln_w), c_rows, w_ada[l], b_ada_l, N_MOD_IN * d,
            [(w_ff2[l], 0)], n_batch=n_ctx, lam_init=lam_init)
        mod_out = mod_out.reshape(MOD_ROWS, N_MOD_OUT, 1, d)
        cache_kt = jnp.transpose(cache_k[:, l], (0, 2, 3, 4, 1)).reshape(n_lat, width, past)
        att_s, w1_l, w_o_l = _attn_cached(
            q_s, k_s, v_s, cache_kt, cache_v[:, l].reshape(n_lat * past, width),
            lams, row(subln_w), [(w_ff1[l], 1), (w_o[l], 0)], n_batch=n_lat, lam_init=lam_init,
            tq=tiles.tq, sub=tiles.tq_sub, heads=n_heads)

        y_p = finish(y_p, att_p, sgu_p, ctx_row)
        y_s = finish(y_s, att_s, sgu_s, lat_row)

    return (y_p.reshape(n_ctx, ctx_len, d), y_s.reshape(n_lat, lat_len, d),
            jnp.stack(state_k, axis=1), jnp.stack(state_v, axis=1))
```

```python
import functools
import math
from typing import NamedTuple

import jax
import jax.numpy as jnp
import numpy as np
from jax import lax
from jax.experimental import pallas as pl
from jax.experimental.pallas import tpu as pltpu

F32 = jnp.float32
BF16 = jnp.bfloat16

HEAD_DIM = 64
V_DIM = 128
CHUNK = 128
GROUP = 128
GRID_W = 64
ROPE_BASE = 10000.0
EPS = 1e-6
N_MOD_IN = 2
N_MOD_OUT = 4
MOD_ROWS = 8
LANES = 128
MXU_DIM = 256
BF16_ROWS = 16
VMEM_BYTES = 64 * 1024 * 1024
VMEM_LIMIT = VMEM_BYTES - 4 * 1024 * 1024
CTX_RIDER_CHUNKS = 4
Q_SCALE = HEAD_DIM ** -0.5 * math.log2(math.e)


class _Tiles(NamedTuple):
    tm: int
    tm_sub: int
    tm_mlp: int
    tf: int
    tq: int
    tq_sub: int


_TILES = _Tiles(tm=512, tm_sub=256, tm_mlp=1024, tf=1024, tq=512, tq_sub=256)


def _cparams(n_axes):
    return pltpu.CompilerParams(
        dimension_semantics=("arbitrary",) * n_axes,
        vmem_limit_bytes=VMEM_LIMIT)


def _adaln_slab(c_ref, w_ref, b_ref, o_ref):
    c = c_ref[...]
    s = c * (1.0 / (1.0 + jnp.exp(-c)))
    o_ref[...] = jnp.dot(s.astype(BF16), w_ref[...].astype(BF16),
                         preferred_element_type=F32) + b_ref[...]


def _adaln_specs(d, tn, first_block):
    return ([pl.BlockSpec((MOD_ROWS, d), lambda *g: (0, 0)),
             pl.BlockSpec((d, tn), lambda *g: (0, first_block + g[-1])),
             pl.BlockSpec((1, tn), lambda *g: (0, first_block + g[-1]))],
            pl.BlockSpec((MOD_ROWS, tn), lambda *g: (0, g[-1])))


def _adaln(c_rows, w_ada, b_ada, n_cols, tn=1024):
    d = w_ada.shape[0]
    in_specs, out_spec = _adaln_specs(d, tn, 0)
    return pl.pallas_call(
        _adaln_slab,
        grid=(n_cols // tn,),
        in_specs=in_specs,
        out_specs=out_spec,
        out_shape=jax.ShapeDtypeStruct((MOD_ROWS, n_cols), F32),
        compiler_params=_cparams(1),
        name="adaln",
    )(c_rows, w_ada, b_ada)


def _gelu(x):
    return jax.nn.gelu(x, approximate=True)


def _head_norm(z, w_ref, bd_ref):
    outs = []
    for c in range(z.shape[1] // MXU_DIM):
        zc = z[:, c * MXU_DIM:(c + 1) * MXU_DIM]
        ss = jnp.dot((zc * zc).astype(BF16), bd_ref[...], preferred_element_type=F32)
        r = lax.rsqrt(ss * (1.0 / HEAD_DIM) + EPS)
        outs.append(zc * r * w_ref[:, c * MXU_DIM:(c + 1) * MXU_DIM])
    return outs


def _rope128(x, cos, sin_lo, sin_hi):
    quarter = HEAD_DIM // 4
    return (x * cos + pltpu.roll(x, LANES - quarter, 1) * sin_lo
            + pltpu.roll(x, quarter, 1) * sin_hi)


def _inproj_kernel(*refs, rope, emit_state):
    it = iter(refs)
    x_ref, sh_ref, sc_ref, n1_ref, w_ref, qw_ref, kw_ref, bd_ref = (next(it) for _ in range(8))
    if rope:
        cos_ref, slo_ref, shi_ref = (next(it) for _ in range(3))
    sgw_ref, ws_ref, bs_ref = (next(it) for _ in range(3))
    q_ref, k_ref, v_ref, sgu_ref = (next(it) for _ in range(4))
    if emit_state:
        sk_ref, sv_ref = (next(it) for _ in range(2))
    tm = x_ref.shape[0]
    width = q_ref.shape[1]

    x = x_ref[...]
    ms = jnp.mean(x * x, axis=-1, keepdims=True)
    h = (x * lax.rsqrt(ms + EPS) * (n1_ref[...] * (1.0 + sc_ref[...])) + sh_ref[...]).astype(BF16)

    def proj(j):
        return jnp.dot(h, w_ref[:, j * width:(j + 1) * width], preferred_element_type=F32)

    def qk_epilogue(z, w_norm_ref, out_ref, scale, state_ref):
        blocks = _head_norm(z, w_norm_ref, bd_ref)
        for c, n in enumerate(blocks):
            if state_ref is not None:
                seq = state_ref.shape[2]
                for s in range(tm // seq):
                    state_ref[s, c * MXU_DIM:(c + 1) * MXU_DIM, :] = n[s * seq:(s + 1) * seq].T
            for hh in range(MXU_DIM // LANES):
                xh = n[:, hh * LANES:(hh + 1) * LANES]
                if rope:
                    xh = _rope128(xh, cos_ref[...], slo_ref[...], shi_ref[...])
                if scale != 1.0:
                    xh = xh * scale
                lo = c * MXU_DIM + hh * LANES
                out_ref[:, lo:lo + LANES] = xh.astype(BF16)

    zq = proj(0)
    zk = proj(1)
    qk_epilogue(zq, qw_ref, q_ref, Q_SCALE, None)
    zv = proj(2)
    qk_epilogue(zk, kw_ref, k_ref, 1.0, sk_ref if emit_state else None)
    zu = proj(3)
    v_ref[...] = zv.astype(BF16)
    if emit_state:
        sv_ref[...] = zv
    zg = proj(4)
    gu = _gelu(zu)
    g = _gelu(zg)
    mu = jnp.mean(g, axis=-1, keepdims=True)
    gc = g - mu
    var = jnp.mean(gc * gc, axis=-1, keepdims=True)
    gn = (gc * lax.rsqrt(var + EPS) * sgw_ref[...]).astype(BF16)
    n_chunks = tm // CHUNK
    for grp in range(width // GROUP):
        cols = slice(grp * GROUP, (grp + 1) * GROUP)
        rhs = jnp.concatenate(
            [gn[c * CHUNK:(c + 1) * CHUNK, cols] for c in range(n_chunks)], axis=1)
        mixed = jnp.dot(ws_ref[grp], rhs, preferred_element_type=F32) + bs_ref[grp]
        for c in range(n_chunks):
            rows = slice(c * CHUNK, (c + 1) * CHUNK)
            sgu_ref[rows, cols] = (
                gu[rows, cols] * mixed[:, c * CHUNK:(c + 1) * CHUNK]).astype(BF16)


def _inproj(x, mod, mod_row, norm1_w, w_in, qw, kw, bd, rope_tabs, sgw, ws, bs,
            *, tm, state_seq):
    m, d = x.shape
    n_in = w_in.shape[1]
    width = n_in // 5
    rope = rope_tabs is not None
    const = lambda shape: pl.BlockSpec(shape, lambda i: (0,) * len(shape),
                                       pipeline_mode=pl.Buffered(1))
    mod_spec = lambda which: pl.BlockSpec(
        (None, None, 1, d), lambda i: (mod_row(i), which, 0, 0))
    in_specs = [
        pl.BlockSpec((tm, d), lambda i: (i, 0)),
        mod_spec(0), mod_spec(1), const((1, d)),
        const((d, n_in)),
        const((1, width)), const((1, width)),
        const((MXU_DIM, MXU_DIM)),
    ]
    args = [x, mod, mod, norm1_w, w_in, qw, kw, bd]
    if rope:
        seq_tiles = rope_tabs[0].shape[0] // tm
        tab_spec = pl.BlockSpec((tm, LANES), lambda i: (i % seq_tiles, 0))
        in_specs += [tab_spec] * 3
        args += list(rope_tabs)
    in_specs += [const((1, width)), const(ws.shape), const(bs.shape)]
    args += [sgw, ws, bs]
    out_spec = pl.BlockSpec((tm, width), lambda i: (i, 0))
    out_shape = [jax.ShapeDtypeStruct((m, width), BF16)] * 4
    out_specs = [out_spec] * 4
    emit_state = state_seq is not None
    if emit_state:
        out_shape += [jax.ShapeDtypeStruct((m // state_seq, width, state_seq), F32),
                      jax.ShapeDtypeStruct((m, width), F32)]
        out_specs += [pl.BlockSpec((tm // state_seq, width, state_seq), lambda i: (i, 0, 0)),
                      out_spec]
    return pl.pallas_call(
        functools.partial(_inproj_kernel, rope=rope, emit_state=emit_state),
        grid=(m // tm,),
        in_specs=in_specs,
        out_specs=out_specs,
        out_shape=out_shape,
        compiler_params=_cparams(1),
        name="inproj_rope" if rope else "inproj_ctx",
    )(*args)


def _lambda(lq1_ref, lk1_ref, lq2_ref, lk2_ref, lam_init):
    a = jnp.sum(lq1_ref[...] * lk1_ref[...], axis=-1, keepdims=True)
    b = jnp.sum(lq2_ref[...] * lk2_ref[...], axis=-1, keepdims=True)
    return jnp.exp(a) - jnp.exp(b) + lam_init


def _lane_fold(op, acc, tile):
    for c in range(0, tile.shape[1], LANES):
        blk = tile[:, c:c + LANES]
        acc = blk if acc is None else op(acc, blk)
    return acc


class _ScoreTiles:
    def __init__(self, q, keys):
        lane = lax.broadcasted_iota(jnp.int32, q.shape, 1)
        zero = jnp.zeros_like(q)
        self.q2 = jnp.concatenate([jnp.where(lane < HEAD_DIM, q, zero),
                                   jnp.where(lane >= HEAD_DIM, q, zero)], axis=0)
        self.slices = []
        for k, feature_major in keys:
            n_keys = k.shape[1] if feature_major else k.shape[0]
            self.slices += [(k, feature_major, lo) for lo in range(0, n_keys, MXU_DIM)]
        self.tiles = []
        self.m_lanes = None

    def __len__(self):
        return len(self.slices)

    def compute(self, t):
        k, feature_major, lo = self.slices[t]
        if feature_major:
            st = jnp.dot(self.q2, k[:, lo:lo + MXU_DIM], preferred_element_type=F32)
        else:
            st = lax.dot_general(self.q2, k[lo:lo + MXU_DIM], (((1,), (1,)), ((), ())),
                                 preferred_element_type=F32)
        self.tiles.append(st)
        self.m_lanes = _lane_fold(jnp.maximum, self.m_lanes, st)

    def row_max(self):
        return jnp.max(self.m_lanes, axis=-1, keepdims=True)


def _attn_units(q_ref, o_ref, sub, lockstep, load_keys, load_values, lam, subln_w, lam_init,
                fillers):
    units = [(slice(r * sub, (r + 1) * sub), slice(h * V_DIM, (h + 1) * V_DIM))
             for r in range(q_ref.shape[0] // sub) for h in range(q_ref.shape[1] // V_DIM)]
    groups = [units[i:i + lockstep] for i in range(0, len(units), lockstep)]
    n_slots = 3 * len(groups)
    pending = list(fillers)

    def fill(slot):
        done = len(fillers) - len(pending)
        for _ in range(len(fillers) * (slot + 1) // n_slots - done):
            pending.pop(0)()

    def scores(group):
        out = []
        for rows, cols in group:
            tiles = _ScoreTiles(q_ref[rows, cols], load_keys(cols))
            for t in range(len(tiles)):
                tiles.compute(t)
            out.append(tiles)
        return out

    nxt = scores(groups[0])
    for i, group in enumerate(groups):
        cur, nxt = nxt, (scores(groups[i + 1]) if i + 1 < len(groups) else None)
        fill(3 * i)
        ms = [tiles.row_max() for tiles in cur]
        es = [jnp.concatenate([jnp.exp2(st - m).astype(BF16) for st in tiles.tiles], axis=1)
              for tiles, m in zip(cur, ms)]
        fill(3 * i + 1)
        ols = []
        for e, (rows, cols) in zip(es, group):
            v = jnp.concatenate(load_values(cols), axis=0)
            v1 = jnp.concatenate([v, jnp.ones_like(v)], axis=1)
            ols.append(jnp.dot(e, v1, preferred_element_type=F32))
        fill(3 * i + 2)
        outs = []
        for ol in ols:
            l1, l2 = ol[:sub, V_DIM:V_DIM + 1], ol[sub:, V_DIM:V_DIM + 1]
            outs.append(ol[:sub, :V_DIM] * (1.0 / l1) - ol[sub:, :V_DIM] * (lam / l2))
        sq = [jnp.mean(o * o, axis=-1, keepdims=True) for o in outs]
        for o, s2, (rows, cols) in zip(outs, sq, group):
            o_ref[rows, cols] = (o * lax.rsqrt(s2 + EPS) * subln_w
                                 * (1.0 - lam_init)).astype(o_ref.dtype)


def _cast_riders(riders, n_steps, step_of):
    in_specs, out_specs, out_shape = [], [], []
    for w, axis in riders:
        blk = tuple(n // n_steps if a == axis else n for a, n in enumerate(w.shape))
        idx = lambda *g, axis=axis, nd=w.ndim: tuple(
            step_of(*g) if a == axis else 0 for a in range(nd))
        in_specs.append(pl.BlockSpec(blk, idx))
        out_specs.append(pl.BlockSpec(blk, idx))
        out_shape.append(jax.ShapeDtypeStruct(w.shape, BF16))
    return in_specs, out_specs, out_shape


def _cast_rider_fillers(refs, n, n_chunks):
    def chunk(src, dst, rows):
        def cast():
            dst[rows, :] = src[rows, :].astype(dst.dtype)
        return cast

    fillers = []
    for src, dst in zip(refs[:n], refs[n:]):
        k = min(n_chunks, src.shape[0] // BF16_ROWS)
        step = src.shape[0] // k
        fillers += [chunk(src, dst, slice(c * step, (c + 1) * step)) for c in range(k)]
    return fillers


def _attn_cached_kernel(q_ref, kn_ref, vn_ref, kct_ref, vc_ref, lq1, lk1, lq2, lk2, sw_ref,
                        *rest, lam_init, sub, n_riders):
    o_ref = rest[n_riders]
    n_units = (q_ref.shape[0] // sub) * (q_ref.shape[1] // V_DIM)
    fillers = _cast_rider_fillers(rest[:n_riders] + rest[n_riders + 1:], n_riders, n_units)
    lam = _lambda(lq1, lk1, lq2, lk2, lam_init)
    _attn_units(
        q_ref, o_ref, sub, 1,
        lambda cols: [(kct_ref[cols, :].astype(BF16), True), (kn_ref[:, cols], False)],
        lambda cols: [vc_ref[:, cols].astype(BF16), vn_ref[:, cols]],
        lam, sw_ref[...], lam_init, fillers)


def _attn_cached(q, k, v, cache_kt, cache_v, lams, subln_w, riders, *, n_batch, lam_init, tq,
                 sub, heads):
    m, width = q.shape
    seq = m // n_batch
    past = cache_kt.shape[2]
    nq = seq // tq
    w = heads * V_DIM
    n_groups = width // w
    vec = lambda n: pl.BlockSpec((1, n), lambda b, g, t: (0, 0))
    r_in, r_out, r_shape = _cast_riders(
        riders, n_batch * n_groups * nq, lambda b, g, t: (b * n_groups + g) * nq + t)
    return pl.pallas_call(
        functools.partial(_attn_cached_kernel, lam_init=lam_init, sub=sub,
                          n_riders=len(riders)),
        grid=(n_batch, n_groups, nq),
        in_specs=[pl.BlockSpec((tq, w), lambda b, g, t: (b * nq + t, g)),
                  pl.BlockSpec((seq, w), lambda b, g, t: (b, g)),
                  pl.BlockSpec((seq, w), lambda b, g, t: (b, g)),
                  pl.BlockSpec((None, w, past), lambda b, g, t: (b, g, 0)),
                  pl.BlockSpec((past, w), lambda b, g, t: (b, g)),
                  vec(HEAD_DIM), vec(HEAD_DIM), vec(HEAD_DIM), vec(HEAD_DIM), vec(V_DIM)] + r_in,
        out_specs=[pl.BlockSpec((tq, w), lambda b, g, t: (b * nq + t, g))] + r_out,
        out_shape=[jax.ShapeDtypeStruct((m, width), BF16)] + r_shape,
        compiler_params=_cparams(3),
        name="attn_latent",
    )(q, k, v, cache_kt, cache_v, *lams, subln_w, *[w for w, _ in riders])


def _attn_self_kernel(q_ref, k_ref, v_ref, lq1, lk1, lq2, lk2, sw_ref, c_ref, wa_ref, ba_ref,
                      *rest, lam_init, n_riders):
    o_ref, mod_ref = rest[n_riders], rest[n_riders + 1]
    fillers = _cast_rider_fillers(rest[:n_riders] + rest[n_riders + 2:], n_riders,
                                  CTX_RIDER_CHUNKS)
    _adaln_slab(c_ref, wa_ref, ba_ref, mod_ref)
    lam = _lambda(lq1, lk1, lq2, lk2, lam_init)
    _attn_units(q_ref, o_ref, q_ref.shape[0], q_ref.shape[1] // V_DIM,
                lambda cols: [(k_ref[:, cols], False)],
                lambda cols: [v_ref[:, cols]],
                lam, sw_ref[...], lam_init, fillers)


def _attn_self(q, k, v, lams, subln_w, c_rows, w_ada, b_ada, first_col, riders, *, n_batch,
               lam_init):
    m, width = q.shape
    seq = m // n_batch
    d, n_mod = w_ada.shape
    tn = (n_mod - first_col) // n_batch
    blk = pl.BlockSpec((seq, width), lambda b: (b, 0))
    vec = lambda w: pl.BlockSpec((1, w), lambda b: (0, 0))
    a_in, a_out = _adaln_specs(d, tn, first_col // tn)
    r_in, r_out, r_shape = _cast_riders(riders, n_batch, lambda b: b)
    return pl.pallas_call(
        functools.partial(_attn_self_kernel, lam_init=lam_init, n_riders=len(riders)),
        grid=(n_batch,),
        in_specs=[blk, blk, blk,
                  vec(HEAD_DIM), vec(HEAD_DIM), vec(HEAD_DIM), vec(HEAD_DIM), vec(V_DIM)]
                 + a_in + r_in,
        out_specs=[blk, a_out] + r_out,
        out_shape=[jax.ShapeDtypeStruct((m, width), BF16),
                   jax.ShapeDtypeStruct((MOD_ROWS, n_mod - first_col), F32)] + r_shape,
        compiler_params=_cparams(1),
        name="attn_ctx",
    )(q, k, v, *lams, subln_w, c_rows, w_ada, b_ada, *[w for w, _ in riders])


def _outproj_kernel(att_ref, sgu_ref, wo_ref, x_ref, g1_ref, sh2_ref, sc2_ref, n2_ref,
                    y_ref, h_ref, *, sub):
    half = att_ref.shape[1]
    tm = x_ref.shape[0]
    n_sub = tm // sub

    def mix(r):
        rows = slice(r * sub, (r + 1) * sub)
        return (jnp.dot(att_ref[rows, :], wo_ref[:half], preferred_element_type=F32)
                + jnp.dot(sgu_ref[rows, :], wo_ref[half:], preferred_element_type=F32))

    nxt = mix(0)
    w2 = n2_ref[...] * (1.0 + sc2_ref[...])
    for r in range(n_sub):
        cur, nxt = nxt, (mix(r + 1) if r + 1 < n_sub else None)
        rows = slice(r * sub, (r + 1) * sub)
        y = x_ref[rows, :] + g1_ref[...] * cur
        y_ref[rows, :] = y
        ms = jnp.mean(y * y, axis=-1, keepdims=True)
        h_ref[rows, :] = (y * lax.rsqrt(ms + EPS) * w2 + sh2_ref[...]).astype(BF16)


def _outproj(att, sgu, w_o, x, mod, mod_row, norm2_w, *, tm, sub):
    m, d = x.shape
    half = att.shape[1]
    mod_spec = lambda which: pl.BlockSpec(
        (None, None, 1, d), lambda i: (mod_row(i), which, 0, 0))
    return pl.pallas_call(
        functools.partial(_outproj_kernel, sub=sub),
        grid=(m // tm,),
        in_specs=[pl.BlockSpec((tm, half), lambda i: (i, 0)),
                  pl.BlockSpec((tm, half), lambda i: (i, 0)),
                  pl.BlockSpec(w_o.shape, lambda i: (0, 0)),
                  pl.BlockSpec((tm, d), lambda i: (i, 0)),
                  mod_spec(0), mod_spec(1), mod_spec(2),
                  pl.BlockSpec((1, d), lambda i: (0, 0))],
        out_specs=[pl.BlockSpec((tm, d), lambda i: (i, 0))] * 2,
        out_shape=[jax.ShapeDtypeStruct((m, d), F32), jax.ShapeDtypeStruct((m, d), BF16)],
        compiler_params=_cparams(1),
        name="outproj",
    )(att, sgu, w_o, x, mod, mod, mod, norm2_w)


def _mlp_kernel(h_ref, w1_ref, w2_ref, y1_hbm, g2_ref, o_ref, y1_buf, y1_sem):
    i, f = pl.program_id(0), pl.program_id(1)
    tm = o_ref.shape[0]

    def y1_copy():
        rows = pl.ds(pl.multiple_of(i * tm, tm), tm)
        return pltpu.make_async_copy(y1_hbm.at[rows, :], y1_buf, y1_sem)

    last = pl.num_programs(1) - 1

    def part():
        hid = jnp.dot(h_ref[...], w1_ref[...], preferred_element_type=F32)
        hid = jnp.square(jnp.maximum(hid, 0.0)).astype(BF16)
        return jnp.dot(hid, w2_ref[...], preferred_element_type=F32)

    @pl.when(f == 0)
    def _():
        y1_copy().start()
        o_ref[...] = part()

    @pl.when(jnp.logical_and(f > 0, f < last))
    def _():
        o_ref[...] += part()

    @pl.when(f == last)
    def _():
        y1_copy().wait()
        o_ref[...] = y1_buf[...] + g2_ref[...] * (o_ref[...] + part())


def _mlp(h, w1, w2, y1, mod, mod_row, *, tm, tf):
    m, d = h.shape
    d_ff = w1.shape[1]
    assert d_ff // tf >= 2
    return pl.pallas_call(
        _mlp_kernel,
        grid=(m // tm, d_ff // tf),
        in_specs=[pl.BlockSpec((tm, d), lambda i, f: (i, 0)),
                  pl.BlockSpec((d, tf), lambda i, f: (0, f)),
                  pl.BlockSpec((tf, d), lambda i, f: (f, 0)),
                  pl.BlockSpec(memory_space=pl.ANY),
                  pl.BlockSpec((None, None, 1, d), lambda i, f: (mod_row(i), 3, 0, 0))],
        out_specs=pl.BlockSpec((tm, d), lambda i, f: (i, 0)),
        out_shape=jax.ShapeDtypeStruct((m, d), F32),
        scratch_shapes=[pltpu.VMEM((tm, d), F32), pltpu.SemaphoreType.DMA(())],
        compiler_params=_cparams(2),
        name="mlp",
    )(h, w1, w2, y1, mod)


def _rope_tables(n):
    rows = n // GRID_W
    r, col = np.meshgrid(np.arange(rows), np.arange(GRID_W), indexing="ij")
    r = r.reshape(-1).astype(np.float32)
    col = col.reshape(-1).astype(np.float32)
    n_freq = HEAD_DIM // 4
    freqs = np.float32(ROPE_BASE) ** (-np.arange(n_freq, dtype=np.float32) / np.float32(n_freq))
    ang_r = r[:, None] * freqs
    ang_c = col[:, None] * freqs
    ang = np.concatenate([ang_r, ang_r, ang_c, ang_c], axis=-1).astype(np.float32)
    cos = np.tile(np.cos(ang), (1, LANES // HEAD_DIM)).astype(np.float32)
    sin = np.tile(np.sin(ang), (1, LANES // HEAD_DIM)).astype(np.float32)
    first = (np.arange(LANES) % (2 * n_freq)) < n_freq
    zero = np.float32(0.0)
    return (jnp.asarray(cos), jnp.asarray(np.where(first, -sin, zero)),
            jnp.asarray(np.where(first, zero, sin)))


def _block_diag_ones(n, blk):
    idx = np.arange(n) // blk
    return jnp.asarray(idx[:, None] == idx[None, :], dtype=BF16)


def kernel(x_prompt, x_sample, cache_k, cache_v, c, c_ctx, w_ada, b_ada, norm1_w, norm2_w,
           w_in, q_norm_w, k_norm_w, lambda_q1, lambda_k1, lambda_q2, lambda_k2, subln_w,
           sgu_norm_w, w_s, b_s, w_o, w_ff1, w_ff2):
    n_ctx, ctx_len, d = x_prompt.shape
    n_lat, lat_len, _ = x_sample.shape
    depth = w_in.shape[0]
    past = cache_k.shape[2]
    width = w_o.shape[1] // 2
    n_heads = width // V_DIM
    tiles = _TILES
    assert n_lat + 1 <= MOD_ROWS and ctx_len % CHUNK == 0
    assert lat_len % tiles.tm == 0 and lat_len % tiles.tm_mlp == 0

    y_p = x_prompt.reshape(n_ctx * ctx_len, d)
    y_s = x_sample.reshape(n_lat * lat_len, d)
    rope_tabs = _rope_tables(lat_len)
    bd = _block_diag_ones(MXU_DIM, HEAD_DIM)
    c_rows = jnp.concatenate(
        [c_ctx[None, :], c, jnp.zeros((MOD_ROWS - 1 - n_lat, d), F32)], axis=0)
    ctx_row = lambda tm: (lambda i: 0)
    lat_row = lambda tm: (lambda i: 1 + i // (lat_len // tm))

    state_k, state_v = [], []
    for l in range(depth):
        lam_init = 0.8 - 0.6 * math.exp(-0.3 * l)
        w_in_l = w_in[l].astype(BF16)
        ws_l = w_s[l].astype(BF16)
        bs_l = b_s[l][:, :, None]
        row = lambda a: a[l].reshape(1, -1)
        qw = jnp.tile(row(q_norm_w), (1, width // HEAD_DIM))
        kw = jnp.tile(row(k_norm_w), (1, width // HEAD_DIM))
        lams = (row(lambda_q1), row(lambda_k1), row(lambda_q2), row(lambda_k2))

        b_ada_l = b_ada[l].reshape(1, -1)
        mod_in = _adaln(c_rows, w_ada[l], b_ada_l, N_MOD_IN * d).reshape(
            MOD_ROWS, N_MOD_IN, 1, d)

        def mixers(x, mod_row, tabs, state_seq):
            return _inproj(x, mod_in, mod_row(tiles.tm), row(norm1_w), w_in_l, qw, kw, bd, tabs,
                           row(sgu_norm_w), ws_l, bs_l, tm=tiles.tm, state_seq=state_seq)

        def finish(x, att, sgu, mod_row):
            y1, h2 = _outproj(att, sgu, w_o_l, x, mod_out, mod_row(tiles.tm), row(norm2_w),
                              tm=tiles.tm, sub=tiles.tm_sub)
            return _mlp(h2, w1_l, w2_l, y1, mod_out, mod_row(tiles.tm_mlp),
                        tm=tiles.tm_mlp, tf=tiles.tf)

        q_p, k_p, v_p, sgu_p, sk, sv = mixers(y_p, ctx_row, None, ctx_len)
        sk = sk.reshape(n_ctx, n_heads, 2, HEAD_DIM, ctx_len)
        state_k.append(jnp.transpose(sk, (0, 4, 1, 2, 3)))
        state_v.append(sv.reshape(n_ctx, ctx_len, n_heads, V_DIM))
        q_s, k_s, v_s, sgu_s = mixers(y_s, lat_row, rope_tabs, None)

        att_p, mod_out, w2_l = _attn_self(
            q_p, k_p, v_p, lams, row(subln_w), c_rows, w_ada[l], b_ada_l, N_MOD_IN * d,
            [(w_ff2[l], 0)], n_batch=n_ctx, lam_init=lam_init)
        mod_out = mod_out.reshape(MOD_ROWS, N_MOD_OUT, 1, d)
        cache_kt = jnp.transpose(cache_k[:, l], (0, 2, 3, 4, 1)).reshape(n_lat, width, past)
        att_s, w1_l, w_o_l = _attn_cached(
            q_s, k_s, v_s, cache_kt, cache_v[:, l].reshape(n_lat * past, width),
            lams, row(subln_w), [(w_ff1[l], 1), (w_o[l], 0)], n_batch=n_lat, lam_init=lam_init,
            tq=tiles.tq, sub=tiles.tq_sub, heads=n_heads)

        y_p = finish(y_p, att_p, sgu_p, ctx_row)
        y_s = finish(y_s, att_s, sgu_s, lat_row)

    return (y_p.reshape(n_ctx, ctx_len, d), y_s.reshape(n_lat, lat_len, d),
            jnp.stack(state_k, axis=1), jnp.stack(state_v, axis=1))
```

```python
import functools
import math
from typing import NamedTuple

import jax
import jax.numpy as jnp
import numpy as np
from jax import lax
from jax.experimental import pallas as pl
from jax.experimental.pallas import tpu as pltpu

F32 = jnp.float32
BF16 = jnp.bfloat16

HEAD_DIM = 64
V_DIM = 128
CHUNK = 128
GROUP = 128
GRID_W = 64
ROPE_BASE = 10000.0
EPS = 1e-6
N_MOD_IN = 2
N_MOD_OUT = 4
MOD_ROWS = 8
LANES = 128
MXU_DIM = 256
BF16_ROWS = 16
VMEM_BYTES = 64 * 1024 * 1024
VMEM_LIMIT = VMEM_BYTES - 4 * 1024 * 1024
CTX_RIDER_CHUNKS = 4
Q_SCALE = HEAD_DIM ** -0.5 * math.log2(math.e)


class _Tiles(NamedTuple):
    tm: int
    tm_sub: int
    tm_mlp: int
    tf: int
    tq: int
    tq_sub: int


_TILES = _Tiles(tm=512, tm_sub=256, tm_mlp=1024, tf=1024, tq=512, tq_sub=256)


def _cparams(n_axes):
    return pltpu.CompilerParams(
        dimension_semantics=("arbitrary",) * n_axes,
        vmem_limit_bytes=VMEM_LIMIT)


def _adaln_slab(c_ref, w_ref, b_ref, o_ref):
    c = c_ref[...]
    s = c * (1.0 / (1.0 + jnp.exp(-c)))
    o_ref[...] = jnp.dot(s.astype(BF16), w_ref[...].astype(BF16),
                         preferred_element_type=F32) + b_ref[...]


def _adaln_specs(d, tn, first_block):
    return ([pl.BlockSpec((MOD_ROWS, d), lambda *g: (0, 0)),
             pl.BlockSpec((d, tn), lambda *g: (0, first_block + g[-1])),
             pl.BlockSpec((1, tn), lambda *g: (0, first_block + g[-1]))],
            pl.BlockSpec((MOD_ROWS, tn), lambda *g: (0, g[-1])))


def _adaln(c_rows, w_ada, b_ada, n_cols, tn=1024):
    d = w_ada.shape[0]
    in_specs, out_spec = _adaln_specs(d, tn, 0)
    return pl.pallas_call(
        _adaln_slab,
        grid=(n_cols // tn,),
        in_specs=in_specs,
        out_specs=out_spec,
        out_shape=jax.ShapeDtypeStruct((MOD_ROWS, n_cols), F32),
        compiler_params=_cparams(1),
        name="adaln",
    )(c_rows, w_ada, b_ada)


def _gelu(x):
    return jax.nn.gelu(x, approximate=True)


def _head_norm(z, w_ref, bd_ref):
    outs = []
    for c in range(z.shape[1] // MXU_DIM):
        zc = z[:, c * MXU_DIM:(c + 1) * MXU_DIM]
        ss = jnp.dot((zc * zc).astype(BF16), bd_ref[...], preferred_element_type=F32)
        r = lax.rsqrt(ss * (1.0 / HEAD_DIM) + EPS)
        outs.append(zc * r * w_ref[:, c * MXU_DIM:(c + 1) * MXU_DIM])
    return outs


def _rope128(x, cos, sin_lo, sin_hi):
    quarter = HEAD_DIM // 4
    return (x * cos + pltpu.roll(x, LANES - quarter, 1) * sin_lo
            + pltpu.roll(x, quarter, 1) * sin_hi)


def _inproj_kernel(*refs, rope, emit_state):
    it = iter(refs)
    x_ref, sh_ref, sc_ref, n1_ref, w_ref, qw_ref, kw_ref, bd_ref = (next(it) for _ in range(8))
    if rope:
        cos_ref, slo_ref, shi_ref = (next(it) for _ in range(3))
    sgw_ref, ws_ref, bs_ref = (next(it) for _ in range(3))
    q_ref, k_ref, v_ref, sgu_ref = (next(it) for _ in range(4))
    if emit_state:
        sk_ref, sv_ref = (next(it) for _ in range(2))
    tm = x_ref.shape[0]
    width = q_ref.shape[1]

    x = x_ref[...]
    ms = jnp.mean(x * x, axis=-1, keepdims=True)
    h = (x * lax.rsqrt(ms + EPS) * (n1_ref[...] * (1.0 + sc_ref[...])) + sh_ref[...]).astype(BF16)

    def proj(j):
        return jnp.dot(h, w_ref[:, j * width:(j + 1) * width], preferred_element_type=F32)

    def qk_epilogue(z, w_norm_ref, out_ref, scale, state_ref):
        blocks = _head_norm(z, w_norm_ref, bd_ref)
        for c, n in enumerate(blocks):
            if state_ref is not None:
                seq = state_ref.shape[2]
                for s in range(tm // seq):
                    state_ref[s, c * MXU_DIM:(c + 1) * MXU_DIM, :] = n[s * seq:(s + 1) * seq].T
            for hh in range(MXU_DIM // LANES):
                xh = n[:, hh * LANES:(hh + 1) * LANES]
                if rope:
                    xh = _rope128(xh, cos_ref[...], slo_ref[...], shi_ref[...])
                if scale != 1.0:
                    xh = xh * scale
                lo = c * MXU_DIM + hh * LANES
                out_ref[:, lo:lo + LANES] = xh.astype(BF16)

    zq = proj(0)
    zk = proj(1)
    qk_epilogue(zq, qw_ref, q_ref, Q_SCALE, None)
    zv = proj(2)
    qk_epilogue(zk, kw_ref, k_ref, 1.0, sk_ref if emit_state else None)
    zu = proj(3)
    v_ref[...] = zv.astype(BF16)
    if emit_state:
        sv_ref[...] = zv
    zg = proj(4)
    gu = _gelu(zu)
    g = _gelu(zg)
    mu = jnp.mean(g, axis=-1, keepdims=True)
    gc = g - mu
    var = jnp.mean(gc * gc, axis=-1, keepdims=True)
    gn = (gc * lax.rsqrt(var + EPS) * sgw_ref[...]).astype(BF16)
    n_chunks = tm // CHUNK
    for grp in range(width // GROUP):
        cols = slice(grp * GROUP, (grp + 1) * GROUP)
        rhs = jnp.concatenate(
            [gn[c * CHUNK:(c + 1) * CHUNK, cols] for c in range(n_chunks)], axis=1)
        mixed = jnp.dot(ws_ref[grp], rhs, preferred_element_type=F32) + bs_ref[grp]
        for c in range(n_chunks):
            rows = slice(c * CHUNK, (c + 1) * CHUNK)
            sgu_ref[rows, cols] = (
                gu[rows, cols] * mixed[:, c * CHUNK:(c + 1) * CHUNK]).astype(BF16)


def _inproj(x, mod, mod_row, norm1_w, w_in, qw, kw, bd, rope_tabs, sgw, ws, bs,
            *, tm, state_seq):
    m, d = x.shape
    n_in = w_in.shape[1]
    width = n_in // 5
    rope = rope_tabs is not None
    const = lambda shape: pl.BlockSpec(shape, lambda i: (0,) * len(shape),
                                       pipeline_mode=pl.Buffered(1))
    mod_spec = lambda which: pl.BlockSpec(
        (None, None, 1, d), lambda i: (mod_row(i), which, 0, 0))
    in_specs = [
        pl.BlockSpec((tm, d), lambda i: (i, 0)),
        mod_spec(0), mod_spec(1), const((1, d)),
        const((d, n_in)),
        const((1, width)), const((1, width)),
        const((MXU_DIM, MXU_DIM)),
    ]
    args = [x, mod, mod, norm1_w, w_in, qw, kw, bd]
    if rope:
        seq_tiles = rope_tabs[0].shape[0] // tm
        tab_spec = pl.BlockSpec((tm, LANES), lambda i: (i % seq_tiles, 0))
        in_specs += [tab_spec] * 3
        args += list(rope_tabs)
    in_specs += [const((1, width)), const(ws.shape), const(bs.shape)]
    args += [sgw, ws, bs]
    out_spec = pl.BlockSpec((tm, width), lambda i: (i, 0))
    out_shape = [jax.ShapeDtypeStruct((m, width), BF16)] * 4
    out_specs = [out_spec] * 4
    emit_state = state_seq is not None
    if emit_state:
        out_shape += [jax.ShapeDtypeStruct((m // state_seq, width, state_seq), F32),
                      jax.ShapeDtypeStruct((m, width), F32)]
        out_specs += [pl.BlockSpec((tm // state_seq, width, state_seq), lambda i: (i, 0, 0)),
                      out_spec]
    return pl.pallas_call(
        functools.partial(_inproj_kernel, rope=rope, emit_state=emit_state),
        grid=(m // tm,),
        in_specs=in_specs,
        out_specs=out_specs,
        out_shape=out_shape,
        compiler_params=_cparams(1),
        name="inproj_rope" if rope else "inproj_ctx",
    )(*args)


def _lambda(lq1_ref, lk1_ref, lq2_ref, lk2_ref, lam_init):
    a = jnp.sum(lq1_ref[...] * lk1_ref[...], axis=-1, keepdims=True)
    b = jnp.sum(lq2_ref[...] * lk2_ref[...], axis=-1, keepdims=True)
    return jnp.exp(a) - jnp.exp(b) + lam_init


def _lane_fold(op, acc, tile):
    for c in range(0, tile.shape[1], LANES):
        blk = tile[:, c:c + LANES]
        acc = blk if acc is None else op(acc, blk)
    return acc


class _ScoreTiles:
    def __init__(self, q, keys):
        lane = lax.broadcasted_iota(jnp.int32, q.shape, 1)
        zero = jnp.zeros_like(q)
        self.q2 = jnp.concatenate([jnp.where(lane < HEAD_DIM, q, zero),
                                   jnp.where(lane >= HEAD_DIM, q, zero)], axis=0)
        self.slices = []
        for k, feature_major in keys:
            n_keys = k.shape[1] if feature_major else k.shape[0]
            self.slices += [(k, feature_major, lo) for lo in range(0, n_keys, MXU_DIM)]
        self.tiles = []
        self.m_lanes = None

    def __len__(self):
        return len(self.slices)

    def compute(self, t):
        k, feature_major, lo = self.slices[t]
        if feature_major:
            st = jnp.dot(self.q2, k[:, lo:lo + MXU_DIM], preferred_element_type=F32)
        else:
            st = lax.dot_general(self.q2, k[lo:lo + MXU_DIM], (((1,), (1,)), ((), ())),
                                 preferred_element_type=F32)
        self.tiles.append(st)
        self.m_lanes = _lane_fold(jnp.maximum, self.m_lanes, st)

    def row_max(self):
        return jnp.max(self.m_lanes, axis=-1, keepdims=True)


def _attn_units(q_ref, o_ref, sub, lockstep, load_keys, load_values, lam, subln_w, lam_init,
                fillers):
    units = [(slice(r * sub, (r + 1) * sub), slice(h * V_DIM, (h + 1) * V_DIM))
             for r in range(q_ref.shape[0] // sub) for h in range(q_ref.shape[1] // V_DIM)]
    groups = [units[i:i + lockstep] for i in range(0, len(units), lockstep)]
    n_slots = 3 * len(groups)
    pending = list(fillers)

    def fill(slot):
        done = len(fillers) - len(pending)
        for _ in range(len(fillers) * (slot + 1) // n_slots - done):
            pending.pop(0)()

    def scores(group):
        out = []
        for rows, cols in group:
            tiles = _ScoreTiles(q_ref[rows, cols], load_keys(cols))
            for t in range(len(tiles)):
                tiles.compute(t)
            out.append(tiles)
        return out

    nxt = scores(groups[0])
    for i, group in enumerate(groups):
        cur, nxt = nxt, (scores(groups[i + 1]) if i + 1 < len(groups) else None)
        fill(3 * i)
        ms = [tiles.row_max() for tiles in cur]
        es = [[jnp.exp2(st - m).astype(BF16) for st in tiles.tiles]
              for tiles, m in zip(cur, ms)]
        fill(3 * i + 1)
        ols = []
        for e_tiles, (rows, cols) in zip(es, group):
            v = jnp.concatenate(load_values(cols), axis=0)
            v1 = jnp.concatenate([v, jnp.ones_like(v)], axis=1)
            ol = None
            for t, et in enumerate(e_tiles):
                part = jnp.dot(et, v1[t * MXU_DIM:(t + 1) * MXU_DIM],
                               preferred_element_type=F32)
                ol = part if ol is None else ol + part
            ols.append(ol)
        fill(3 * i + 2)
        outs = []
        for ol in ols:
            l1, l2 = ol[:sub, V_DIM:V_DIM + 1], ol[sub:, V_DIM:V_DIM + 1]
            outs.append(ol[:sub, :V_DIM] * (1.0 / l1) - ol[sub:, :V_DIM] * (lam / l2))
        sq = [jnp.mean(o * o, axis=-1, keepdims=True) for o in outs]
        for o, s2, (rows, cols) in zip(outs, sq, group):
            o_ref[rows, cols] = (o * lax.rsqrt(s2 + EPS) * subln_w
                                 * (1.0 - lam_init)).astype(o_ref.dtype)


def _cast_riders(riders, n_steps, step_of):
    in_specs, out_specs, out_shape = [], [], []
    for w, axis in riders:
        blk = tuple(n // n_steps if a == axis else n for a, n in enumerate(w.shape))
        idx = lambda *g, axis=axis, nd=w.ndim: tuple(
            step_of(*g) if a == axis else 0 for a in range(nd))
        in_specs.append(pl.BlockSpec(blk, idx))
        out_specs.append(pl.BlockSpec(blk, idx))
        out_shape.append(jax.ShapeDtypeStruct(w.shape, BF16))
    return in_specs, out_specs, out_shape


def _cast_rider_fillers(refs, n, n_chunks):
    def chunk(src, dst, rows):
        def cast():
            dst[rows, :] = src[rows, :].astype(dst.dtype)
        return cast

    fillers = []
    for src, dst in zip(refs[:n], refs[n:]):
        k = min(n_chunks, src.shape[0] // BF16_ROWS)
        step = src.shape[0] // k
        fillers += [chunk(src, dst, slice(c * step, (c + 1) * step)) for c in range(k)]
    return fillers


def _attn_cached_kernel(q_ref, kn_ref, vn_ref, kct_ref, vc_ref, lq1, lk1, lq2, lk2, sw_ref,
                        *rest, lam_init, sub, n_riders):
    o_ref = rest[n_riders]
    n_units = (q_ref.shape[0] // sub) * (q_ref.shape[1] // V_DIM)
    fillers = _cast_rider_fillers(rest[:n_riders] + rest[n_riders + 1:], n_riders, n_units)
    lam = _lambda(lq1, lk1, lq2, lk2, lam_init)
    _attn_units(
        q_ref, o_ref, sub, 1,
        lambda cols: [(kct_ref[cols, :].astype(BF16), True), (kn_ref[:, cols], False)],
        lambda cols: [vc_ref[:, cols].astype(BF16), vn_ref[:, cols]],
        lam, sw_ref[...], lam_init, fillers)


def _attn_cached(q, k, v, cache_kt, cache_v, lams, subln_w, riders, *, n_batch, lam_init, tq,
                 sub, heads):
    m, width = q.shape
    seq = m // n_batch
    past = cache_kt.shape[2]
    nq = seq // tq
    w = heads * V_DIM
    n_groups = width // w
    vec = lambda n: pl.BlockSpec((1, n), lambda b, g, t: (0, 0))
    r_in, r_out, r_shape = _cast_riders(
        riders, n_batch * n_groups * nq, lambda b, g, t: (b * n_groups + g) * nq + t)
    return pl.pallas_call(
        functools.partial(_attn_cached_kernel, lam_init=lam_init, sub=sub,
                          n_riders=len(riders)),
        grid=(n_batch, n_groups, nq),
        in_specs=[pl.BlockSpec((tq, w), lambda b, g, t: (b * nq + t, g)),
                  pl.BlockSpec((seq, w), lambda b, g, t: (b, g)),
                  pl.BlockSpec((seq, w), lambda b, g, t: (b, g)),
                  pl.BlockSpec((None, w, past), lambda b, g, t: (b, g, 0)),
                  pl.BlockSpec((past, w), lambda b, g, t: (b, g)),
                  vec(HEAD_DIM), vec(HEAD_DIM), vec(HEAD_DIM), vec(HEAD_DIM), vec(V_DIM)] + r_in,
        out_specs=[pl.BlockSpec((tq, w), lambda b, g, t: (b * nq + t, g))] + r_out,
        out_shape=[jax.ShapeDtypeStruct((m, width), BF16)] + r_shape,
        compiler_params=_cparams(3),
        name="attn_latent",
    )(q, k, v, cache_kt, cache_v, *lams, subln_w, *[w for w, _ in riders])


def _attn_self_kernel(q_ref, k_ref, v_ref, lq1, lk1, lq2, lk2, sw_ref, c_ref, wa_ref, ba_ref,
                      *rest, lam_init, n_riders):
    o_ref, mod_ref = rest[n_riders], rest[n_riders + 1]
    fillers = _cast_rider_fillers(rest[:n_riders] + rest[n_riders + 2:], n_riders,
                                  CTX_RIDER_CHUNKS)
    _adaln_slab(c_ref, wa_ref, ba_ref, mod_ref)
    lam = _lambda(lq1, lk1, lq2, lk2, lam_init)
    _attn_units(q_ref, o_ref, q_ref.shape[0], q_ref.shape[1] // V_DIM,
                lambda cols: [(k_ref[:, cols], False)],
                lambda cols: [v_ref[:, cols]],
                lam, sw_ref[...], lam_init, fillers)


def _attn_self(q, k, v, lams, subln_w, c_rows, w_ada, b_ada, first_col, riders, *, n_batch,
               lam_init):
    m, width = q.shape
    seq = m // n_batch
    d, n_mod = w_ada.shape
    tn = (n_mod - first_col) // n_batch
    blk = pl.BlockSpec((seq, width), lambda b: (b, 0))
    vec = lambda w: pl.BlockSpec((1, w), lambda b: (0, 0))
    a_in, a_out = _adaln_specs(d, tn, first_col // tn)
    r_in, r_out, r_shape = _cast_riders(riders, n_batch, lambda b: b)
    return pl.pallas_call(
        functools.partial(_attn_self_kernel, lam_init=lam_init, n_riders=len(riders)),
        grid=(n_batch,),
        in_specs=[blk, blk, blk,
                  vec(HEAD_DIM), vec(HEAD_DIM), vec(HEAD_DIM), vec(HEAD_DIM), vec(V_DIM)]
                 + a_in + r_in,
        out_specs=[blk, a_out] + r_out,
        out_shape=[jax.ShapeDtypeStruct((m, width), BF16),
                   jax.ShapeDtypeStruct((MOD_ROWS, n_mod - first_col), F32)] + r_shape,
        compiler_params=_cparams(1),
        name="attn_ctx",
    )(q, k, v, *lams, subln_w, c_rows, w_ada, b_ada, *[w for w, _ in riders])


def _outproj_kernel(att_ref, sgu_ref, wo_ref, x_ref, g1_ref, sh2_ref, sc2_ref, n2_ref,
                    y_ref, h_ref, *, sub):
    half = att_ref.shape[1]
    tm = x_ref.shape[0]
    n_sub = tm // sub

    def mix(r):
        rows = slice(r * sub, (r + 1) * sub)
        return (jnp.dot(att_ref[rows, :], wo_ref[:half], preferred_element_type=F32)
                + jnp.dot(sgu_ref[rows, :], wo_ref[half:], preferred_element_type=F32))

    nxt = mix(0)
    w2 = n2_ref[...] * (1.0 + sc2_ref[...])
    for r in range(n_sub):
        cur, nxt = nxt, (mix(r + 1) if r + 1 < n_sub else None)
        rows = slice(r * sub, (r + 1) * sub)
        y = x_ref[rows, :] + g1_ref[...] * cur
        y_ref[rows, :] = y
        ms = jnp.mean(y * y, axis=-1, keepdims=True)
        h_ref[rows, :] = (y * lax.rsqrt(ms + EPS) * w2 + sh2_ref[...]).astype(BF16)


def _outproj(att, sgu, w_o, x, mod, mod_row, norm2_w, *, tm, sub):
    m, d = x.shape
    half = att.shape[1]
    mod_spec = lambda which: pl.BlockSpec(
        (None, None, 1, d), lambda i: (mod_row(i), which, 0, 0))
    return pl.pallas_call(
        functools.partial(_outproj_kernel, sub=sub),
        grid=(m // tm,),
        in_specs=[pl.BlockSpec((tm, half), lambda i: (i, 0)),
                  pl.BlockSpec((tm, half), lambda i: (i, 0)),
                  pl.BlockSpec(w_o.shape, lambda i: (0, 0)),
                  pl.BlockSpec((tm, d), lambda i: (i, 0)),
                  mod_spec(0), mod_spec(1), mod_spec(2),
                  pl.BlockSpec((1, d), lambda i: (0, 0))],
        out_specs=[pl.BlockSpec((tm, d), lambda i: (i, 0))] * 2,
        out_shape=[jax.ShapeDtypeStruct((m, d), F32), jax.ShapeDtypeStruct((m, d), BF16)],
        compiler_params=_cparams(1),
        name="outproj",
    )(att, sgu, w_o, x, mod, mod, mod, norm2_w)


def _mlp_kernel(h_ref, w1_ref, w2_ref, y1_hbm, g2_ref, o_ref, y1_buf, y1_sem):
    i, f = pl.program_id(0), pl.program_id(1)
    tm = o_ref.shape[0]

    def y1_copy():
        rows = pl.ds(pl.multiple_of(i * tm, tm), tm)
        return pltpu.make_async_copy(y1_hbm.at[rows, :], y1_buf, y1_sem)

    last = pl.num_programs(1) - 1

    def part():
        hid = jnp.dot(h_ref[...], w1_ref[...], preferred_element_type=F32)
        hid = jnp.square(jnp.maximum(hid, 0.0)).astype(BF16)
        return jnp.dot(hid, w2_ref[...], preferred_element_type=F32)

    @pl.when(f == 0)
    def _():
        y1_copy().start()
        o_ref[...] = part()

    @pl.when(jnp.logical_and(f > 0, f < last))
    def _():
        o_ref[...] += part()

    @pl.when(f == last)
    def _():
        y1_copy().wait()
        o_ref[...] = y1_buf[...] + g2_ref[...] * (o_ref[...] + part())


def _mlp(h, w1, w2, y1, mod, mod_row, *, tm, tf):
    m, d = h.shape
    d_ff = w1.shape[1]
    assert d_ff // tf >= 2
    return pl.pallas_call(
        _mlp_kernel,
        grid=(m // tm, d_ff // tf),
        in_specs=[pl.BlockSpec((tm, d), lambda i, f: (i, 0)),
                  pl.BlockSpec((d, tf), lambda i, f: (0, f)),
                  pl.BlockSpec((tf, d), lambda i, f: (f, 0)),
                  pl.BlockSpec(memory_space=pl.ANY),
                  pl.BlockSpec((None, None, 1, d), lambda i, f: (mod_row(i), 3, 0, 0))],
        out_specs=pl.BlockSpec((tm, d), lambda i, f: (i, 0)),
        out_shape=jax.ShapeDtypeStruct((m, d), F32),
        scratch_shapes=[pltpu.VMEM((tm, d), F32), pltpu.SemaphoreType.DMA(())],
        compiler_params=_cparams(2),
        name="mlp",
    )(h, w1, w2, y1, mod)


def _rope_tables(n):
    rows = n // GRID_W
    r, col = np.meshgrid(np.arange(rows), np.arange(GRID_W), indexing="ij")
    r = r.reshape(-1).astype(np.float32)
    col = col.reshape(-1).astype(np.float32)
    n_freq = HEAD_DIM // 4
    freqs = np.float32(ROPE_BASE) ** (-np.arange(n_freq, dtype=np.float32) / np.float32(n_freq))
    ang_r = r[:, None] * freqs
    ang_c = col[:, None] * freqs
    ang = np.concatenate([ang_r, ang_r, ang_c, ang_c], axis=-1).astype(np.float32)
    cos = np.tile(np.cos(ang), (1, LANES // HEAD_DIM)).astype(np.float32)
    sin = np.tile(np.sin(ang), (1, LANES // HEAD_DIM)).astype(np.float32)
    first = (np.arange(LANES) % (2 * n_freq)) < n_freq
    zero = np.float32(0.0)
    return (jnp.asarray(cos), jnp.asarray(np.where(first, -sin, zero)),
            jnp.asarray(np.where(first, zero, sin)))


def _block_diag_ones(n, blk):
    idx = np.arange(n) // blk
    return jnp.asarray(idx[:, None] == idx[None, :], dtype=BF16)


def kernel(x_prompt, x_sample, cache_k, cache_v, c, c_ctx, w_ada, b_ada, norm1_w, norm2_w,
           w_in, q_norm_w, k_norm_w, lambda_q1, lambda_k1, lambda_q2, lambda_k2, subln_w,
           sgu_norm_w, w_s, b_s, w_o, w_ff1, w_ff2):
    n_ctx, ctx_len, d = x_prompt.shape
    n_lat, lat_len, _ = x_sample.shape
    depth = w_in.shape[0]
    past = cache_k.shape[2]
    width = w_o.shape[1] // 2
    n_heads = width // V_DIM
    tiles = _TILES
    assert n_lat + 1 <= MOD_ROWS and ctx_len % CHUNK == 0
    assert lat_len % tiles.tm == 0 and lat_len % tiles.tm_mlp == 0

    y_p = x_prompt.reshape(n_ctx * ctx_len, d)
    y_s = x_sample.reshape(n_lat * lat_len, d)
    rope_tabs = _rope_tables(lat_len)
    bd = _block_diag_ones(MXU_DIM, HEAD_DIM)
    c_rows = jnp.concatenate(
        [c_ctx[None, :], c, jnp.zeros((MOD_ROWS - 1 - n_lat, d), F32)], axis=0)
    ctx_row = lambda tm: (lambda i: 0)
    lat_row = lambda tm: (lambda i: 1 + i // (lat_len // tm))

    state_k, state_v = [], []
    for l in range(depth):
        lam_init = 0.8 - 0.6 * math.exp(-0.3 * l)
        w_in_l = w_in[l].astype(BF16)
        ws_l = w_s[l].astype(BF16)
        bs_l = b_s[l][:, :, None]
        row = lambda a: a[l].reshape(1, -1)
        qw = jnp.tile(row(q_norm_w), (1, width // HEAD_DIM))
        kw = jnp.tile(row(k_norm_w), (1, width // HEAD_DIM))
        lams = (row(lambda_q1), row(lambda_k1), row(lambda_q2), row(lambda_k2))

        b_ada_l = b_ada[l].reshape(1, -1)
        mod_in = _adaln(c_rows, w_ada[l], b_ada_l, N_MOD_IN * d).reshape(
            MOD_ROWS, N_MOD_IN, 1, d)

        def mixers(x, mod_row, tabs, state_seq):
            return _inproj(x, mod_in, mod_row(tiles.tm), row(norm1_w), w_in_l, qw, kw, bd, tabs,
                           row(sgu_norm_w), ws_l, bs_l, tm=tiles.tm, state_seq=state_seq)

        def finish(x, att, sgu, mod_row):
            y1, h2 = _outproj(att, sgu, w_o_l, x, mod_out, mod_row(tiles.tm), row(norm2_w),
                              tm=tiles.tm, sub=tiles.tm_sub)
            return _mlp(h2, w1_l, w2_l, y1, mod_out, mod_row(tiles.tm_mlp),
                        tm=tiles.tm_mlp, tf=tiles.tf)

        q_p, k_p, v_p, sgu_p, sk, sv = mixers(y_p, ctx_row, None, ctx_len)
        sk = sk.reshape(n_ctx, n_heads, 2, HEAD_DIM, ctx_len)
        state_k.append(jnp.transpose(sk, (0, 4, 1, 2, 3)))
        state_v.append(sv.reshape(n_ctx, ctx_len, n_heads, V_DIM))
        q_s, k_s, v_s, sgu_s = mixers(y_s, lat_row, rope_tabs, None)

        att_p, mod_out, w2_l = _attn_self(
            q_p, k_p, v_p, lams, row(subln_w), c_rows, w_ada[l], b_ada_l, N_MOD_IN * d,
            [(w_ff2[l], 0)], n_batch=n_ctx, lam_init=lam_init)
        mod_out = mod_out.reshape(MOD_ROWS, N_MOD_OUT, 1, d)
        cache_kt = jnp.transpose(cache_k[:, l], (0, 2, 3, 4, 1)).reshape(n_lat, width, past)
        att_s, w1_l, w_o_l = _attn_cached(
            q_s, k_s, v_s, cache_kt, cache_v[:, l].reshape(n_lat * past, width),
            lams, row(subln_w), [(w_ff1[l], 1), (w_o[l], 0)], n_batch=n_lat, lam_init=lam_init,
            tq=tiles.tq, sub=tiles.tq_sub, heads=n_heads)

        y_p = finish(y_p, att_p, sgu_p, ctx_row)
        y_s = finish(y_s, att_s, sgu_s, lat_row)

    return (y_p.reshape(n_ctx, ctx_len, d), y_s.reshape(n_lat, lat_len, d),
            jnp.stack(state_k, axis=1), jnp.stack(state_v, axis=1))
```

```python
import functools
import math
from typing import NamedTuple

import jax
import jax.numpy as jnp
import numpy as np
from jax import lax
from jax.experimental import pallas as pl
from jax.experimental.pallas import tpu as pltpu

F32 = jnp.float32
BF16 = jnp.bfloat16

HEAD_DIM = 64
V_DIM = 128
CHUNK = 128
GROUP = 128
GRID_W = 64
ROPE_BASE = 10000.0
EPS = 1e-6
N_MOD_IN = 2
N_MOD_OUT = 4
MOD_ROWS = 8
LANES = 128
MXU_DIM = 256
BF16_ROWS = 16
VMEM_BYTES = 64 * 1024 * 1024
VMEM_LIMIT = VMEM_BYTES - 4 * 1024 * 1024
CTX_RIDER_CHUNKS = 4
Q_SCALE = HEAD_DIM ** -0.5 * math.log2(math.e)


class _Tiles(NamedTuple):
    tm: int
    tm_sub: int
    tm_mlp: int
    tf: int
    tq: int
    tq_sub: int


_TILES = _Tiles(tm=512, tm_sub=256, tm_mlp=1024, tf=1024, tq=512, tq_sub=256)


def _cparams(n_axes):
    return pltpu.CompilerParams(
        dimension_semantics=("arbitrary",) * n_axes,
        vmem_limit_bytes=VMEM_LIMIT)


def _adaln_slab(c_ref, w_ref, b_ref, o_ref):
    c = c_ref[...]
    s = c * (1.0 / (1.0 + jnp.exp(-c)))
    o_ref[...] = jnp.dot(s.astype(BF16), w_ref[...].astype(BF16),
                         preferred_element_type=F32) + b_ref[...]


def _adaln_specs(d, tn, first_block):
    return ([pl.BlockSpec((MOD_ROWS, d), lambda *g: (0, 0)),
             pl.BlockSpec((d, tn), lambda *g: (0, first_block + g[-1])),
             pl.BlockSpec((1, tn), lambda *g: (0, first_block + g[-1]))],
            pl.BlockSpec((MOD_ROWS, tn), lambda *g: (0, g[-1])))


def _adaln(c_rows, w_ada, b_ada, n_cols, tn=1024):
    d = w_ada.shape[0]
    in_specs, out_spec = _adaln_specs(d, tn, 0)
    return pl.pallas_call(
        _adaln_slab,
        grid=(n_cols // tn,),
        in_specs=in_specs,
        out_specs=out_spec,
        out_shape=jax.ShapeDtypeStruct((MOD_ROWS, n_cols), F32),
        compiler_params=_cparams(1),
        name="adaln",
    )(c_rows, w_ada, b_ada)


def _gelu(x):
    return jax.nn.gelu(x, approximate=True)


def _head_norm(z, w_ref, bd_ref):
    outs = []
    for c in range(z.shape[1] // MXU_DIM):
        zc = z[:, c * MXU_DIM:(c + 1) * MXU_DIM]
        ss = jnp.dot((zc * zc).astype(BF16), bd_ref[...], preferred_element_type=F32)
        r = lax.rsqrt(ss * (1.0 / HEAD_DIM) + EPS)
        outs.append(zc * r * w_ref[:, c * MXU_DIM:(c + 1) * MXU_DIM])
    return outs


def _rope128(x, cos, sin_lo, sin_hi):
    quarter = HEAD_DIM // 4
    return (x * cos + pltpu.roll(x, LANES - quarter, 1) * sin_lo
            + pltpu.roll(x, quarter, 1) * sin_hi)


def _inproj_kernel(*refs, rope, emit_state):
    it = iter(refs)
    x_ref, sh_ref, sc_ref, n1_ref, w_ref, qw_ref, kw_ref, bd_ref = (next(it) for _ in range(8))
    if rope:
        cos_ref, slo_ref, shi_ref = (next(it) for _ in range(3))
    sgw_ref, ws_ref, bs_ref = (next(it) for _ in range(3))
    q_ref, k_ref, v_ref, sgu_ref = (next(it) for _ in range(4))
    if emit_state:
        sk_ref, sv_ref = (next(it) for _ in range(2))
    tm = x_ref.shape[0]
    width = q_ref.shape[1]

    x = x_ref[...]
    ms = jnp.mean(x * x, axis=-1, keepdims=True)
    h = (x * lax.rsqrt(ms + EPS) * (n1_ref[...] * (1.0 + sc_ref[...])) + sh_ref[...]).astype(BF16)

    def proj(j):
        return jnp.dot(h, w_ref[:, j * width:(j + 1) * width], preferred_element_type=F32)

    def qk_epilogue(z, w_norm_ref, out_ref, scale, state_ref):
        blocks = _head_norm(z, w_norm_ref, bd_ref)
        for c, n in enumerate(blocks):
            if state_ref is not None:
                seq = state_ref.shape[2]
                for s in range(tm // seq):
                    state_ref[s, c * MXU_DIM:(c + 1) * MXU_DIM, :] = n[s * seq:(s + 1) * seq].T
            for hh in range(MXU_DIM // LANES):
                xh = n[:, hh * LANES:(hh + 1) * LANES]
                if rope:
                    xh = _rope128(xh, cos_ref[...], slo_ref[...], shi_ref[...])
                if scale != 1.0:
                    xh = xh * scale
                lo = c * MXU_DIM + hh * LANES
                out_ref[:, lo:lo + LANES] = xh.astype(BF16)

    zq = proj(0)
    zk = proj(1)
    qk_epilogue(zq, qw_ref, q_ref, Q_SCALE, None)
    zv = proj(2)
    qk_epilogue(zk, kw_ref, k_ref, 1.0, sk_ref if emit_state else None)
    zu = proj(3)
    v_ref[...] = zv.astype(BF16)
    if emit_state:
        sv_ref[...] = zv
    zg = proj(4)
    gu = _gelu(zu)
    g = _gelu(zg)
    mu = jnp.mean(g, axis=-1, keepdims=True)
    gc = g - mu
    var = jnp.mean(gc * gc, axis=-1, keepdims=True)
    gn = (gc * lax.rsqrt(var + EPS) * sgw_ref[...]).astype(BF16)
    n_chunks = tm // CHUNK
    for grp in range(width // GROUP):
        cols = slice(grp * GROUP, (grp + 1) * GROUP)
        rhs = jnp.concatenate(
            [gn[c * CHUNK:(c + 1) * CHUNK, cols] for c in range(n_chunks)], axis=1)
        mixed = jnp.dot(ws_ref[grp], rhs, preferred_element_type=F32) + bs_ref[grp]
        for c in range(n_chunks):
            rows = slice(c * CHUNK, (c + 1) * CHUNK)
            sgu_ref[rows, cols] = (
                gu[rows, cols] * mixed[:, c * CHUNK:(c + 1) * CHUNK]).astype(BF16)


def _inproj(x, mod, mod_row, norm1_w, w_in, qw, kw, bd, rope_tabs, sgw, ws, bs,
            *, tm, state_seq):
    m, d = x.shape
    n_in = w_in.shape[1]
    width = n_in // 5
    rope = rope_tabs is not None
    const = lambda shape: pl.BlockSpec(shape, lambda i: (0,) * len(shape),
                                       pipeline_mode=pl.Buffered(1))
    mod_spec = lambda which: pl.BlockSpec(
        (None, None, 1, d), lambda i: (mod_row(i), which, 0, 0))
    in_specs = [
        pl.BlockSpec((tm, d), lambda i: (i, 0)),
        mod_spec(0), mod_spec(1), const((1, d)),
        const((d, n_in)),
        const((1, width)), const((1, width)),
        const((MXU_DIM, MXU_DIM)),
    ]
    args = [x, mod, mod, norm1_w, w_in, qw, kw, bd]
    if rope:
        seq_tiles = rope_tabs[0].shape[0] // tm
        tab_spec = pl.BlockSpec((tm, LANES), lambda i: (i % seq_tiles, 0))
        in_specs += [tab_spec] * 3
        args += list(rope_tabs)
    in_specs += [const((1, width)), const(ws.shape), const(bs.shape)]
    args += [sgw, ws, bs]
    out_spec = pl.BlockSpec((tm, width), lambda i: (i, 0))
    out_shape = [jax.ShapeDtypeStruct((m, width), BF16)] * 4
    out_specs = [out_spec] * 4
    emit_state = state_seq is not None
    if emit_state:
        out_shape += [jax.ShapeDtypeStruct((m // state_seq, width, state_seq), F32),
                      jax.ShapeDtypeStruct((m, width), F32)]
        out_specs += [pl.BlockSpec((tm // state_seq, width, state_seq), lambda i: (i, 0, 0)),
                      out_spec]
    return pl.pallas_call(
        functools.partial(_inproj_kernel, rope=rope, emit_state=emit_state),
        grid=(m // tm,),
        in_specs=in_specs,
        out_specs=out_specs,
        out_shape=out_shape,
        compiler_params=_cparams(1),
        name="inproj_rope" if rope else "inproj_ctx",
    )(*args)


def _lambda(lq1_ref, lk1_ref, lq2_ref, lk2_ref, lam_init):
    a = jnp.sum(lq1_ref[...] * lk1_ref[...], axis=-1, keepdims=True)
    b = jnp.sum(lq2_ref[...] * lk2_ref[...], axis=-1, keepdims=True)
    return jnp.exp(a) - jnp.exp(b) + lam_init


def _lane_fold(op, acc, tile):
    for c in range(0, tile.shape[1], LANES):
        blk = tile[:, c:c + LANES]
        acc = blk if acc is None else op(acc, blk)
    return acc


class _ScoreTiles:
    def __init__(self, q, keys):
        lane = lax.broadcasted_iota(jnp.int32, q.shape, 1)
        zero = jnp.zeros_like(q)
        self.q2 = jnp.concatenate([jnp.where(lane < HEAD_DIM, q, zero),
                                   jnp.where(lane >= HEAD_DIM, q, zero)], axis=0)
        self.slices = []
        for k, feature_major in keys:
            n_keys = k.shape[1] if feature_major else k.shape[0]
            self.slices += [(k, feature_major, lo) for lo in range(0, n_keys, MXU_DIM)]
        self.tiles = []
        self.m_lanes = None

    def __len__(self):
        return len(self.slices)

    def compute(self, t):
        k, feature_major, lo = self.slices[t]
        if feature_major:
            st = jnp.dot(self.q2, k[:, lo:lo + MXU_DIM], preferred_element_type=F32)
        else:
            st = lax.dot_general(self.q2, k[lo:lo + MXU_DIM], (((1,), (1,)), ((), ())),
                                 preferred_element_type=F32)
        self.tiles.append(st)
        self.m_lanes = _lane_fold(jnp.maximum, self.m_lanes, st)

    def row_max(self):
        return jnp.max(self.m_lanes, axis=-1, keepdims=True)


def _attn_units(q_ref, o_ref, sub, lockstep, load_keys, load_values, lam, subln_w, lam_init,
                fillers):
    units = [(slice(r * sub, (r + 1) * sub), slice(h * V_DIM, (h + 1) * V_DIM))
             for r in range(q_ref.shape[0] // sub) for h in range(q_ref.shape[1] // V_DIM)]
    groups = [units[i:i + lockstep] for i in range(0, len(units), lockstep)]
    n_slots = 3 * len(groups)
    pending = list(fillers)

    def fill(slot):
        done = len(fillers) - len(pending)
        for _ in range(len(fillers) * (slot + 1) // n_slots - done):
            pending.pop(0)()

    def scores(group):
        out = []
        for rows, cols in group:
            tiles = _ScoreTiles(q_ref[rows, cols], load_keys(cols))
            for t in range(len(tiles)):
                tiles.compute(t)
            out.append(tiles)
        return out

    nxt = scores(groups[0])
    for i, group in enumerate(groups):
        cur, nxt = nxt, (scores(groups[i + 1]) if i + 1 < len(groups) else None)
        fill(3 * i)
        ms = [tiles.row_max() for tiles in cur]
        es = [jnp.concatenate([jnp.exp2(st - m).astype(BF16) for st in tiles.tiles], axis=1)
              for tiles, m in zip(cur, ms)]
        fill(3 * i + 1)
        ols = []
        for e, (rows, cols) in zip(es, group):
            v = jnp.concatenate(load_values(cols), axis=0)
            v1 = jnp.concatenate([v, jnp.ones_like(v)], axis=1)
            ols.append(jnp.dot(e, v1, preferred_element_type=F32))
        fill(3 * i + 2)
        outs = []
        for ol in ols:
            l1, l2 = ol[:sub, V_DIM:V_DIM + 1], ol[sub:, V_DIM:V_DIM + 1]
            outs.append(ol[:sub, :V_DIM] * (1.0 / l1) - ol[sub:, :V_DIM] * (lam / l2))
        sq = [jnp.mean(o * o, axis=-1, keepdims=True) for o in outs]
        for o, s2, (rows, cols) in zip(outs, sq, group):
            o_ref[rows, cols] = (o * lax.rsqrt(s2 + EPS) * subln_w
                                 * (1.0 - lam_init)).astype(o_ref.dtype)


def _cast_riders(riders, n_steps, step_of):
    in_specs, out_specs, out_shape = [], [], []
    for w, axis in riders:
        blk = tuple(n // n_steps if a == axis else n for a, n in enumerate(w.shape))
        idx = lambda *g, axis=axis, nd=w.ndim: tuple(
            step_of(*g) if a == axis else 0 for a in range(nd))
        in_specs.append(pl.BlockSpec(blk, idx))
        out_specs.append(pl.BlockSpec(blk, idx))
        out_shape.append(jax.ShapeDtypeStruct(w.shape, BF16))
    return in_specs, out_specs, out_shape


def _cast_rider_fillers(refs, n, n_chunks):
    def chunk(src, dst, rows):
        def cast():
            dst[rows, :] = src[rows, :].astype(dst.dtype)
        return cast

    fillers = []
    for src, dst in zip(refs[:n], refs[n:]):
        k = min(n_chunks, src.shape[0] // BF16_ROWS)
        step = src.shape[0] // k
        fillers += [chunk(src, dst, slice(c * step, (c + 1) * step)) for c in range(k)]
    return fillers


def _attn_cached_kernel(q_ref, kn_ref, vn_ref, kct_ref, vc_ref, lq1, lk1, lq2, lk2, sw_ref,
                        *rest, lam_init, sub, n_riders):
    o_ref = rest[n_riders]
    n_units = (q_ref.shape[0] // sub) * (q_ref.shape[1] // V_DIM)
    fillers = _cast_rider_fillers(rest[:n_riders] + rest[n_riders + 1:], n_riders, n_units)
    lam = _lambda(lq1, lk1, lq2, lk2, lam_init)
    _attn_units(
        q_ref, o_ref, sub, 1,
        lambda cols: [(kct_ref[cols, :].astype(BF16), True), (kn_ref[:, cols], False)],
        lambda cols: [vc_ref[:, cols].astype(BF16), vn_ref[:, cols]],
        lam, sw_ref[...], lam_init, fillers)


def _attn_cached(q, k, v, cache_kt, cache_v, lams, subln_w, riders, *, n_batch, lam_init, tq,
                 sub, heads):
    m, width = q.shape
    seq = m // n_batch
    past = cache_kt.shape[2]
    nq = seq // tq
    w = heads * V_DIM
    n_groups = width // w
    vec = lambda n: pl.BlockSpec((1, n), lambda b, g, t: (0, 0))
    r_in, r_out, r_shape = _cast_riders(
        riders, n_batch * n_groups * nq, lambda b, g, t: (b * n_groups + g) * nq + t)
    return pl.pallas_call(
        functools.partial(_attn_cached_kernel, lam_init=lam_init, sub=sub,
                          n_riders=len(riders)),
        grid=(n_batch, n_groups, nq),
        in_specs=[pl.BlockSpec((tq, w), lambda b, g, t: (b * nq + t, g)),
                  pl.BlockSpec((seq, w), lambda b, g, t: (b, g)),
                  pl.BlockSpec((seq, w), lambda b, g, t: (b, g)),
                  pl.BlockSpec((None, w, past), lambda b, g, t: (b, g, 0)),
                  pl.BlockSpec((past, w), lambda b, g, t: (b, g)),
                  vec(HEAD_DIM), vec(HEAD_DIM), vec(HEAD_DIM), vec(HEAD_DIM), vec(V_DIM)] + r_in,
        out_specs=[pl.BlockSpec((tq, w), lambda b, g, t: (b * nq + t, g))] + r_out,
        out_shape=[jax.ShapeDtypeStruct((m, width), BF16)] + r_shape,
        compiler_params=_cparams(3),
        name="attn_latent",
    )(q, k, v, cache_kt, cache_v, *lams, subln_w, *[w for w, _ in riders])


def _attn_self_kernel(q_ref, k_ref, v_ref, lq1, lk1, lq2, lk2, sw_ref, c_ref, *rest, lam_init,
                      n_riders, n_mod_blocks):
    wa_refs, ba_refs = rest[:n_mod_blocks], rest[n_mod_blocks:2 * n_mod_blocks]
    rest = rest[2 * n_mod_blocks:]
    o_ref, mod_ref = rest[n_riders], rest[n_riders + 1]
    fillers = _cast_rider_fillers(rest[:n_riders] + rest[n_riders + 2:], n_riders,
                                  CTX_RIDER_CHUNKS)
    blk = ba_refs[0].shape[1]

    @pl.when(pl.program_id(0) == 0)
    def _():
        for j, ba_ref in enumerate(ba_refs):
            mod_ref[:, j * blk:(j + 1) * blk] = jnp.broadcast_to(ba_ref[...], (MOD_ROWS, blk))

    c = c_ref[...]
    s = (c * (1.0 / (1.0 + jnp.exp(-c)))).astype(BF16)
    for j, wa_ref in enumerate(wa_refs):
        mod_ref[:, j * blk:(j + 1) * blk] += jnp.dot(s, wa_ref[...].astype(BF16),
                                                     preferred_element_type=F32)
    lam = _lambda(lq1, lk1, lq2, lk2, lam_init)
    _attn_units(q_ref, o_ref, q_ref.shape[0], q_ref.shape[1] // V_DIM,
                lambda cols: [(k_ref[:, cols], False)],
                lambda cols: [v_ref[:, cols]],
                lam, sw_ref[...], lam_init, fillers)


def _attn_self(q, k, v, lams, subln_w, c_rows, w_ada, b_ada, first_col, riders, *, n_batch,
               lam_init):
    m, width = q.shape
    seq = m // n_batch
    d, n_mod = w_ada.shape
    n_blocks = (n_mod - first_col) // first_col
    assert n_blocks * first_col == n_mod - first_col and d % n_batch == 0
    rows = d // n_batch
    blk = pl.BlockSpec((seq, width), lambda b: (b, 0))
    vec = lambda w: pl.BlockSpec((1, w), lambda b: (0, 0))
    a_in = ([pl.BlockSpec((MOD_ROWS, rows), lambda b: (0, b))]
            + [pl.BlockSpec((rows, first_col), lambda b, j=j: (b, 1 + j)) for j in range(n_blocks)]
            + [pl.BlockSpec((1, first_col), lambda b, j=j: (0, 1 + j)) for j in range(n_blocks)])
    a_out = pl.BlockSpec((MOD_ROWS, n_mod - first_col), lambda b: (0, 0))
    r_in, r_out, r_shape = _cast_riders(riders, n_batch, lambda b: b)
    return pl.pallas_call(
        functools.partial(_attn_self_kernel, lam_init=lam_init, n_riders=len(riders),
                          n_mod_blocks=n_blocks),
        grid=(n_batch,),
        in_specs=[blk, blk, blk,
                  vec(HEAD_DIM), vec(HEAD_DIM), vec(HEAD_DIM), vec(HEAD_DIM), vec(V_DIM)]
                 + a_in + r_in,
        out_specs=[blk, a_out] + r_out,
        out_shape=[jax.ShapeDtypeStruct((m, width), BF16),
                   jax.ShapeDtypeStruct((MOD_ROWS, n_mod - first_col), F32)] + r_shape,
        compiler_params=_cparams(1),
        name="attn_ctx",
    )(q, k, v, *lams, subln_w, c_rows, *([w_ada] * n_blocks), *([b_ada] * n_blocks),
      *[w for w, _ in riders])


def _outproj_kernel(att_ref, sgu_ref, wo_ref, x_ref, g1_ref, sh2_ref, sc2_ref, n2_ref,
                    y_ref, h_ref, *, sub):
    half = att_ref.shape[1]
    tm = x_ref.shape[0]
    n_sub = tm // sub

    def mix(r):
        rows = slice(r * sub, (r + 1) * sub)
        return (jnp.dot(att_ref[rows, :], wo_ref[:half], preferred_element_type=F32)
                + jnp.dot(sgu_ref[rows, :], wo_ref[half:], preferred_element_type=F32))

    nxt = mix(0)
    w2 = n2_ref[...] * (1.0 + sc2_ref[...])
    for r in range(n_sub):
        cur, nxt = nxt, (mix(r + 1) if r + 1 < n_sub else None)
        rows = slice(r * sub, (r + 1) * sub)
        y = x_ref[rows, :] + g1_ref[...] * cur
        y_ref[rows, :] = y
        ms = jnp.mean(y * y, axis=-1, keepdims=True)
        h_ref[rows, :] = (y * lax.rsqrt(ms + EPS) * w2 + sh2_ref[...]).astype(BF16)


def _outproj(att, sgu, w_o, x, mod, mod_row, norm2_w, *, tm, sub):
    m, d = x.shape
    half = att.shape[1]
    mod_spec = lambda which: pl.BlockSpec(
        (None, None, 1, d), lambda i: (mod_row(i), which, 0, 0))
    return pl.pallas_call(
        functools.partial(_outproj_kernel, sub=sub),
        grid=(m // tm,),
        in_specs=[pl.BlockSpec((tm, half), lambda i: (i, 0)),
                  pl.BlockSpec((tm, half), lambda i: (i, 0)),
                  pl.BlockSpec(w_o.shape, lambda i: (0, 0)),
                  pl.BlockSpec((tm, d), lambda i: (i, 0)),
                  mod_spec(0), mod_spec(1), mod_spec(2),
                  pl.BlockSpec((1, d), lambda i: (0, 0))],
        out_specs=[pl.BlockSpec((tm, d), lambda i: (i, 0))] * 2,
        out_shape=[jax.ShapeDtypeStruct((m, d), F32), jax.ShapeDtypeStruct((m, d), BF16)],
        compiler_params=_cparams(1),
        name="outproj",
    )(att, sgu, w_o, x, mod, mod, mod, norm2_w)


def _mlp_kernel(h_ref, w1_ref, w2_ref, y1_hbm, g2_ref, o_ref, y1_buf, y1_sem):
    i, f = pl.program_id(0), pl.program_id(1)
    tm = o_ref.shape[0]

    def y1_copy():
        rows = pl.ds(pl.multiple_of(i * tm, tm), tm)
        return pltpu.make_async_copy(y1_hbm.at[rows, :], y1_buf, y1_sem)

    last = pl.num_programs(1) - 1

    def part():
        hid = jnp.dot(h_ref[...], w1_ref[...], preferred_element_type=F32)
        hid = jnp.square(jnp.maximum(hid, 0.0)).astype(BF16)
        return jnp.dot(hid, w2_ref[...], preferred_element_type=F32)

    @pl.when(f == 0)
    def _():
        y1_copy().start()
        o_ref[...] = part()

    @pl.when(jnp.logical_and(f > 0, f < last))
    def _():
        o_ref[...] += part()

    @pl.when(f == last)
    def _():
        y1_copy().wait()
        o_ref[...] = y1_buf[...] + g2_ref[...] * (o_ref[...] + part())


def _mlp(h, w1, w2, y1, mod, mod_row, *, tm, tf):
    m, d = h.shape
    d_ff = w1.shape[1]
    assert d_ff // tf >= 2
    return pl.pallas_call(
        _mlp_kernel,
        grid=(m // tm, d_ff // tf),
        in_specs=[pl.BlockSpec((tm, d), lambda i, f: (i, 0)),
                  pl.BlockSpec((d, tf), lambda i, f: (0, f)),
                  pl.BlockSpec((tf, d), lambda i, f: (f, 0)),
                  pl.BlockSpec(memory_space=pl.ANY),
                  pl.BlockSpec((None, None, 1, d), lambda i, f: (mod_row(i), 3, 0, 0))],
        out_specs=pl.BlockSpec((tm, d), lambda i, f: (i, 0)),
        out_shape=jax.ShapeDtypeStruct((m, d), F32),
        scratch_shapes=[pltpu.VMEM((tm, d), F32), pltpu.SemaphoreType.DMA(())],
        compiler_params=_cparams(2),
        name="mlp",
    )(h, w1, w2, y1, mod)


def _rope_tables(n):
    rows = n // GRID_W
    r, col = np.meshgrid(np.arange(rows), np.arange(GRID_W), indexing="ij")
    r = r.reshape(-1).astype(np.float32)
    col = col.reshape(-1).astype(np.float32)
    n_freq = HEAD_DIM // 4
    freqs = np.float32(ROPE_BASE) ** (-np.arange(n_freq, dtype=np.float32) / np.float32(n_freq))
    ang_r = r[:, None] * freqs
    ang_c = col[:, None] * freqs
    ang = np.concatenate([ang_r, ang_r, ang_c, ang_c], axis=-1).astype(np.float32)
    cos = np.tile(np.cos(ang), (1, LANES // HEAD_DIM)).astype(np.float32)
    sin = np.tile(np.sin(ang), (1, LANES // HEAD_DIM)).astype(np.float32)
    first = (np.arange(LANES) % (2 * n_freq)) < n_freq
    zero = np.float32(0.0)
    return (jnp.asarray(cos), jnp.asarray(np.where(first, -sin, zero)),
            jnp.asarray(np.where(first, zero, sin)))


def _block_diag_ones(n, blk):
    idx = np.arange(n) // blk
    return jnp.asarray(idx[:, None] == idx[None, :], dtype=BF16)


def kernel(x_prompt, x_sample, cache_k, cache_v, c, c_ctx, w_ada, b_ada, norm1_w, norm2_w,
           w_in, q_norm_w, k_norm_w, lambda_q1, lambda_k1, lambda_q2, lambda_k2, subln_w,
           sgu_norm_w, w_s, b_s, w_o, w_ff1, w_ff2):
    n_ctx, ctx_len, d = x_prompt.shape
    n_lat, lat_len, _ = x_sample.shape
    depth = w_in.shape[0]
    past = cache_k.shape[2]
    width = w_o.shape[1] // 2
    n_heads = width // V_DIM
    tiles = _TILES
    assert n_lat + 1 <= MOD_ROWS and ctx_len % CHUNK == 0
    assert lat_len % tiles.tm == 0 and lat_len % tiles.tm_mlp == 0

    y_p = x_prompt.reshape(n_ctx * ctx_len, d)
    y_s = x_sample.reshape(n_lat * lat_len, d)
    rope_tabs = _rope_tables(lat_len)
    bd = _block_diag_ones(MXU_DIM, HEAD_DIM)
    c_rows = jnp.concatenate(
        [c_ctx[None, :], c, jnp.zeros((MOD_ROWS - 1 - n_lat, d), F32)], axis=0)
    ctx_row = lambda tm: (lambda i: 0)
    lat_row = lambda tm: (lambda i: 1 + i // (lat_len // tm))

    state_k, state_v = [], []
    for l in range(depth):
        lam_init = 0.8 - 0.6 * math.exp(-0.3 * l)
        w_in_l = w_in[l].astype(BF16)
        ws_l = w_s[l].astype(BF16)
        bs_l = b_s[l][:, :, None]
        row = lambda a: a[l].reshape(1, -1)
        qw = jnp.tile(row(q_norm_w), (1, width // HEAD_DIM))
        kw = jnp.tile(row(k_norm_w), (1, width // HEAD_DIM))
        lams = (row(lambda_q1), row(lambda_k1), row(lambda_q2), row(lambda_k2))

        b_ada_l = b_ada[l].reshape(1, -1)
        mod_in = _adaln(c_rows, w_ada[l], b_ada_l, N_MOD_IN * d).reshape(
            MOD_ROWS, N_MOD_IN, 1, d)

        def mixers(x, mod_row, tabs, state_seq):
            return _inproj(x, mod_in, mod_row(tiles.tm), row(norm1_w), w_in_l, qw, kw, bd, tabs,
                           row(sgu_norm_w), ws_l, bs_l, tm=tiles.tm, state_seq=state_seq)

        def finish(x, att, sgu, mod_row):
            y1, h2 = _outproj(att, sgu, w_o_l, x, mod_out, mod_row(tiles.tm), row(norm2_w),
                              tm=tiles.tm, sub=tiles.tm_sub)
            return _mlp(h2, w1_l, w2_l, y1, mod_out, mod_row(tiles.tm_mlp),
                        tm=tiles.tm_mlp, tf=tiles.tf)

        q_p, k_p, v_p, sgu_p, sk, sv = mixers(y_p, ctx_row, None, ctx_len)
        sk = sk.reshape(n_ctx, n_heads, 2, HEAD_DIM, ctx_len)
        state_k.append(jnp.transpose(sk, (0, 4, 1, 2, 3)))
        state_v.append(sv.reshape(n_ctx, ctx_len, n_heads, V_DIM))
        q_s, k_s, v_s, sgu_s = mixers(y_s, lat_row, rope_tabs, None)

        att_p, mod_out, w2_l = _attn_self(
            q_p, k_p, v_p, lams, row(subln_w), c_rows, w_ada[l], b_ada_l, N_MOD_IN * d,
            [(w_ff2[l], 0)], n_batch=n_ctx, lam_init=lam_init)
        mod_out = mod_out.reshape(MOD_ROWS, N_MOD_OUT, 1, d)
        cache_kt = jnp.transpose(cache_k[:, l], (0, 2, 3, 4, 1)).reshape(n_lat, width, past)
        att_s, w1_l, w_o_l = _attn_cached(
            q_s, k_s, v_s, cache_kt, cache_v[:, l].reshape(n_lat * past, width),
            lams, row(subln_w), [(w_ff1[l], 1), (w_o[l], 0)], n_batch=n_lat, lam_init=lam_init,
            tq=tiles.tq, sub=tiles.tq_sub, heads=n_heads)

        y_p = finish(y_p, att_p, sgu_p, ctx_row)
        y_s = finish(y_s, att_s, sgu_s, lat_row)

    return (y_p.reshape(n_ctx, ctx_len, d), y_s.reshape(n_lat, lat_len, d),
            jnp.stack(state_k, axis=1), jnp.stack(state_v, axis=1))
```

```python
import functools
import math
from typing import NamedTuple

import jax
import jax.numpy as jnp
import numpy as np
from jax import lax
from jax.experimental import pallas as pl
from jax.experimental.pallas import tpu as pltpu

F32 = jnp.float32
BF16 = jnp.bfloat16

HEAD_DIM = 64
V_DIM = 128
CHUNK = 128
GROUP = 128
GRID_W = 64
ROPE_BASE = 10000.0
EPS = 1e-6
N_MOD_IN = 2
N_MOD_OUT = 4
MOD_ROWS = 8
LANES = 128
MXU_DIM = 256
BF16_ROWS = 16
VMEM_BYTES = 64 * 1024 * 1024
VMEM_LIMIT = VMEM_BYTES - 4 * 1024 * 1024
CTX_RIDER_CHUNKS = 4
Q_SCALE = HEAD_DIM ** -0.5 * math.log2(math.e)


class _Tiles(NamedTuple):
    tm: int
    tm_sub: int
    tm_mlp: int
    tf: int
    tq: int
    tq_sub: int


_TILES = _Tiles(tm=512, tm_sub=256, tm_mlp=1024, tf=1024, tq=512, tq_sub=256)


def _cparams(n_axes):
    return pltpu.CompilerParams(
        dimension_semantics=("arbitrary",) * n_axes,
        vmem_limit_bytes=VMEM_LIMIT)


def _adaln_slab(c_ref, w_ref, b_ref, o_ref):
    c = c_ref[...]
    s = c * (1.0 / (1.0 + jnp.exp(-c)))
    o_ref[...] = jnp.dot(s.astype(BF16), w_ref[...].astype(BF16),
                         preferred_element_type=F32) + b_ref[...]


def _adaln_specs(d, tn, first_block):
    return ([pl.BlockSpec((MOD_ROWS, d), lambda *g: (0, 0)),
             pl.BlockSpec((d, tn), lambda *g: (0, first_block + g[-1])),
             pl.BlockSpec((1, tn), lambda *g: (0, first_block + g[-1]))],
            pl.BlockSpec((MOD_ROWS, tn), lambda *g: (0, g[-1])))


def _adaln(c_rows, w_ada, b_ada, n_cols, tn=1024):
    d = w_ada.shape[0]
    in_specs, out_spec = _adaln_specs(d, tn, 0)
    return pl.pallas_call(
        _adaln_slab,
        grid=(n_cols // tn,),
        in_specs=in_specs,
        out_specs=out_spec,
        out_shape=jax.ShapeDtypeStruct((MOD_ROWS, n_cols), F32),
        compiler_params=_cparams(1),
        name="adaln",
    )(c_rows, w_ada, b_ada)


def _gelu(x):
    return jax.nn.gelu(x, approximate=True)


def _head_norm(z, w_ref, bd_ref):
    outs = []
    for c in range(z.shape[1] // MXU_DIM):
        zc = z[:, c * MXU_DIM:(c + 1) * MXU_DIM]
        ss = jnp.dot((zc * zc).astype(BF16), bd_ref[...], preferred_element_type=F32)
        r = lax.rsqrt(ss * (1.0 / HEAD_DIM) + EPS)
        outs.append(zc * r * w_ref[:, c * MXU_DIM:(c + 1) * MXU_DIM])
    return outs


def _rope128(x, cos, sin_lo, sin_hi):
    quarter = HEAD_DIM // 4
    return (x * cos + pltpu.roll(x, LANES - quarter, 1) * sin_lo
            + pltpu.roll(x, quarter, 1) * sin_hi)


def _inproj_kernel(*refs, rope, emit_state):
    it = iter(refs)
    x_ref, sh_ref, sc_ref, n1_ref, w_ref, qw_ref, kw_ref, bd_ref = (next(it) for _ in range(8))
    if rope:
        cos_ref, slo_ref, shi_ref = (next(it) for _ in range(3))
    sgw_ref, ws_ref, bs_ref = (next(it) for _ in range(3))
    q_ref, k_ref, v_ref, sgu_ref = (next(it) for _ in range(4))
    if emit_state:
        sk_ref, sv_ref = (next(it) for _ in range(2))
    tm = x_ref.shape[0]
    width = q_ref.shape[1]

    x = x_ref[...]
    ms = jnp.mean(x * x, axis=-1, keepdims=True)
    h = (x * lax.rsqrt(ms + EPS) * (n1_ref[...] * (1.0 + sc_ref[...])) + sh_ref[...]).astype(BF16)

    def proj(j):
        return jnp.dot(h, w_ref[:, j * width:(j + 1) * width], preferred_element_type=F32)

    def qk_epilogue(z, w_norm_ref, out_ref, scale, state_ref):
        blocks = _head_norm(z, w_norm_ref, bd_ref)
        for c, n in enumerate(blocks):
            if state_ref is not None:
                seq = state_ref.shape[2]
                for s in range(tm // seq):
                    state_ref[s, c * MXU_DIM:(c + 1) * MXU_DIM, :] = n[s * seq:(s + 1) * seq].T
            for hh in range(MXU_DIM // LANES):
                xh = n[:, hh * LANES:(hh + 1) * LANES]
                if rope:
                    xh = _rope128(xh, cos_ref[...], slo_ref[...], shi_ref[...])
                if scale != 1.0:
                    xh = xh * scale
                lo = c * MXU_DIM + hh * LANES
                out_ref[:, lo:lo + LANES] = xh.astype(BF16)

    zq = proj(0)
    zk = proj(1)
    qk_epilogue(zq, qw_ref, q_ref, Q_SCALE, None)
    zv = proj(2)
    qk_epilogue(zk, kw_ref, k_ref, 1.0, sk_ref if emit_state else None)
    zu = proj(3)
    v_ref[...] = zv.astype(BF16)
    if emit_state:
        sv_ref[...] = zv
    zg = proj(4)
    gu = _gelu(zu)
    g = _gelu(zg)
    mu = jnp.mean(g, axis=-1, keepdims=True)
    gc = g - mu
    var = jnp.mean(gc * gc, axis=-1, keepdims=True)
    gn = (gc * lax.rsqrt(var + EPS) * sgw_ref[...]).astype(BF16)
    n_chunks = tm // CHUNK
    for grp in range(width // GROUP):
        cols = slice(grp * GROUP, (grp + 1) * GROUP)
        rhs = jnp.concatenate(
            [gn[c * CHUNK:(c + 1) * CHUNK, cols] for c in range(n_chunks)], axis=1)
        mixed = jnp.dot(ws_ref[grp], rhs, preferred_element_type=F32) + bs_ref[grp]
        for c in range(n_chunks):
            rows = slice(c * CHUNK, (c + 1) * CHUNK)
            sgu_ref[rows, cols] = (
                gu[rows, cols] * mixed[:, c * CHUNK:(c + 1) * CHUNK]).astype(BF16)


def _inproj(x, mod, mod_row, norm1_w, w_in, qw, kw, bd, rope_tabs, sgw, ws, bs,
            *, tm, state_seq):
    m, d = x.shape
    n_in = w_in.shape[1]
    width = n_in // 5
    rope = rope_tabs is not None
    const = lambda shape: pl.BlockSpec(shape, lambda i: (0,) * len(shape),
                                       pipeline_mode=pl.Buffered(1))
    mod_spec = lambda which: pl.BlockSpec(
        (None, None, 1, d), lambda i: (mod_row(i), which, 0, 0))
    in_specs = [
        pl.BlockSpec((tm, d), lambda i: (i, 0)),
        mod_spec(0), mod_spec(1), const((1, d)),
        const((d, n_in)),
        const((1, width)), const((1, width)),
        const((MXU_DIM, MXU_DIM)),
    ]
    args = [x, mod, mod, norm1_w, w_in, qw, kw, bd]
    if rope:
        seq_tiles = rope_tabs[0].shape[0] // tm
        tab_spec = pl.BlockSpec((tm, LANES), lambda i: (i % seq_tiles, 0))
        in_specs += [tab_spec] * 3
        args += list(rope_tabs)
    in_specs += [const((1, width)), const(ws.shape), const(bs.shape)]
    args += [sgw, ws, bs]
    out_spec = pl.BlockSpec((tm, width), lambda i: (i, 0))
    out_shape = [jax.ShapeDtypeStruct((m, width), BF16)] * 4
    out_specs = [out_spec] * 4
    emit_state = state_seq is not None
    if emit_state:
        out_shape += [jax.ShapeDtypeStruct((m // state_seq, width, state_seq), F32),
                      jax.ShapeDtypeStruct((m, width), F32)]
        out_specs += [pl.BlockSpec((tm // state_seq, width, state_seq), lambda i: (i, 0, 0)),
                      out_spec]
    return pl.pallas_call(
        functools.partial(_inproj_kernel, rope=rope, emit_state=emit_state),
        grid=(m // tm,),
        in_specs=in_specs,
        out_specs=out_specs,
        out_shape=out_shape,
        compiler_params=_cparams(1),
        name="inproj_rope" if rope else "inproj_ctx",
    )(*args)


def _lambda(lq1_ref, lk1_ref, lq2_ref, lk2_ref, lam_init):
    a = jnp.sum(lq1_ref[...] * lk1_ref[...], axis=-1, keepdims=True)
    b = jnp.sum(lq2_ref[...] * lk2_ref[...], axis=-1, keepdims=True)
    return jnp.exp(a) - jnp.exp(b) + lam_init


def _lane_fold(op, acc, tile):
    for c in range(0, tile.shape[1], LANES):
        blk = tile[:, c:c + LANES]
        acc = blk if acc is None else op(acc, blk)
    return acc


class _ScoreTiles:
    def __init__(self, q, keys):
        lane = lax.broadcasted_iota(jnp.int32, q.shape, 1)
        zero = jnp.zeros_like(q)
        self.q2 = jnp.concatenate([jnp.where(lane < HEAD_DIM, q, zero),
                                   jnp.where(lane >= HEAD_DIM, q, zero)], axis=0)
        self.slices = []
        for k, feature_major in keys:
            n_keys = k.shape[1] if feature_major else k.shape[0]
            self.slices += [(k, feature_major, lo) for lo in range(0, n_keys, MXU_DIM)]
        self.tiles = []
        self.m_lanes = None

    def __len__(self):
        return len(self.slices)

    def compute(self, t):
        k, feature_major, lo = self.slices[t]
        if feature_major:
            st = jnp.dot(self.q2, k[:, lo:lo + MXU_DIM], preferred_element_type=F32)
        else:
            st = lax.dot_general(self.q2, k[lo:lo + MXU_DIM], (((1,), (1,)), ((), ())),
                                 preferred_element_type=F32)
        self.tiles.append(st)
        self.m_lanes = _lane_fold(jnp.maximum, self.m_lanes, st)

    def row_max(self):
        return jnp.max(self.m_lanes, axis=-1, keepdims=True)


def _attn_units(q_ref, o_ref, sub, lockstep, load_keys, load_values, lam, subln_w, lam_init,
                fillers):
    units = [(slice(r * sub, (r + 1) * sub), slice(h * V_DIM, (h + 1) * V_DIM))
             for r in range(q_ref.shape[0] // sub) for h in range(q_ref.shape[1] // V_DIM)]
    groups = [units[i:i + lockstep] for i in range(0, len(units), lockstep)]
    n_slots = 3 * len(groups)
    pending = list(fillers)

    def fill(slot):
        done = len(fillers) - len(pending)
        for _ in range(len(fillers) * (slot + 1) // n_slots - done):
            pending.pop(0)()

    def scores(group):
        out = []
        for rows, cols in group:
            tiles = _ScoreTiles(q_ref[rows, cols], load_keys(cols))
            for t in range(len(tiles)):
                tiles.compute(t)
            out.append(tiles)
        return out

    nxt = scores(groups[0])
    for i, group in enumerate(groups):
        cur, nxt = nxt, (scores(groups[i + 1]) if i + 1 < len(groups) else None)
        fill(3 * i)
        ms = [tiles.row_max() for tiles in cur]
        es = [jnp.concatenate([jnp.exp2(st - m).astype(BF16) for st in tiles.tiles], axis=1)
              for tiles, m in zip(cur, ms)]
        fill(3 * i + 1)
        ols = []
        for e, (rows, cols) in zip(es, group):
            v = jnp.concatenate(load_values(cols), axis=0)
            v1 = jnp.concatenate([v, jnp.ones_like(v)], axis=1)
            ols.append(jnp.dot(e, v1, preferred_element_type=F32))
        fill(3 * i + 2)
        outs = []
        for ol in ols:
            l1, l2 = ol[:sub, V_DIM:V_DIM + 1], ol[sub:, V_DIM:V_DIM + 1]
            outs.append(ol[:sub, :V_DIM] * (1.0 / l1) - ol[sub:, :V_DIM] * (lam / l2))
        sq = [jnp.mean(o * o, axis=-1, keepdims=True) for o in outs]
        for o, s2, (rows, cols) in zip(outs, sq, group):
            o_ref[rows, cols] = (o * lax.rsqrt(s2 + EPS) * subln_w
                                 * (1.0 - lam_init)).astype(o_ref.dtype)


def _cast_riders(riders, n_steps, step_of):
    in_specs, out_specs, out_shape = [], [], []
    for w, axis in riders:
        blk = tuple(n // n_steps if a == axis else n for a, n in enumerate(w.shape))
        idx = lambda *g, axis=axis, nd=w.ndim: tuple(
            step_of(*g) if a == axis else 0 for a in range(nd))
        in_specs.append(pl.BlockSpec(blk, idx))
        out_specs.append(pl.BlockSpec(blk, idx))
        out_shape.append(jax.ShapeDtypeStruct(w.shape, BF16))
    return in_specs, out_specs, out_shape


def _cast_rider_fillers(refs, n, n_chunks):
    def chunk(src, dst, rows):
        def cast():
            dst[rows, :] = src[rows, :].astype(dst.dtype)
        return cast

    fillers = []
    for src, dst in zip(refs[:n], refs[n:]):
        k = min(n_chunks, src.shape[0] // BF16_ROWS)
        step = src.shape[0] // k
        fillers += [chunk(src, dst, slice(c * step, (c + 1) * step)) for c in range(k)]
    return fillers


def _attn_cached_kernel(q_ref, kn_ref, vn_ref, kct_ref, vc_ref, lq1, lk1, lq2, lk2, sw_ref,
                        *rest, lam_init, sub, n_riders):
    o_ref = rest[n_riders]
    n_units = (q_ref.shape[0] // sub) * (q_ref.shape[1] // V_DIM)
    fillers = _cast_rider_fillers(rest[:n_riders] + rest[n_riders + 1:], n_riders, n_units)
    lam = _lambda(lq1, lk1, lq2, lk2, lam_init)
    _attn_units(
        q_ref, o_ref, sub, 1,
        lambda cols: [(kct_ref[cols, :].astype(BF16), True), (kn_ref[:, cols], False)],
        lambda cols: [vc_ref[:, cols].astype(BF16), vn_ref[:, cols]],
        lam, sw_ref[...], lam_init, fillers)


def _attn_cached(q, k, v, cache_kt, cache_v, lams, subln_w, riders, *, n_batch, lam_init, tq,
                 sub, heads):
    m, width = q.shape
    seq = m // n_batch
    past = cache_kt.shape[2]
    nq = seq // tq
    w = heads * V_DIM
    n_groups = width // w
    vec = lambda n: pl.BlockSpec((1, n), lambda b, g, t: (0, 0))
    r_in, r_out, r_shape = _cast_riders(
        riders, n_batch * n_groups * nq, lambda b, g, t: (b * n_groups + g) * nq + t)
    return pl.pallas_call(
        functools.partial(_attn_cached_kernel, lam_init=lam_init, sub=sub,
                          n_riders=len(riders)),
        grid=(n_batch, n_groups, nq),
        in_specs=[pl.BlockSpec((tq, w), lambda b, g, t: (b * nq + t, g)),
                  pl.BlockSpec((seq, w), lambda b, g, t: (b, g)),
                  pl.BlockSpec((seq, w), lambda b, g, t: (b, g)),
                  pl.BlockSpec((None, w, past), lambda b, g, t: (b, g, 0)),
                  pl.BlockSpec((past, w), lambda b, g, t: (b, g)),
                  vec(HEAD_DIM), vec(HEAD_DIM), vec(HEAD_DIM), vec(HEAD_DIM), vec(V_DIM)] + r_in,
        out_specs=[pl.BlockSpec((tq, w), lambda b, g, t: (b * nq + t, g))] + r_out,
        out_shape=[jax.ShapeDtypeStruct((m, width), BF16)] + r_shape,
        compiler_params=_cparams(3),
        name="attn_latent",
    )(q, k, v, cache_kt, cache_v, *lams, subln_w, *[w for w, _ in riders])


def _attn_self_kernel(q_ref, k_ref, v_ref, lq1, lk1, lq2, lk2, sw_ref, c_ref, wa_ref, ba_ref,
                      *rest, lam_init, n_riders):
    o_ref, mod_ref = rest[n_riders], rest[n_riders + 1]
    fillers = _cast_rider_fillers(rest[:n_riders] + rest[n_riders + 2:], n_riders,
                                  CTX_RIDER_CHUNKS)
    _adaln_slab(c_ref, wa_ref, ba_ref, mod_ref)
    lam = _lambda(lq1, lk1, lq2, lk2, lam_init)
    _attn_units(q_ref, o_ref, q_ref.shape[0], q_ref.shape[1] // V_DIM,
                lambda cols: [(k_ref[:, cols], False)],
                lambda cols: [v_ref[:, cols]],
                lam, sw_ref[...], lam_init, fillers)


def _attn_self(q, k, v, lams, subln_w, c_rows, w_ada, b_ada, first_col, riders, *, n_batch,
               lam_init):
    m, width = q.shape
    seq = m // n_batch
    d, n_mod = w_ada.shape
    tn = (n_mod - first_col) // n_batch
    blk = pl.BlockSpec((seq, width), lambda b: (b, 0))
    vec = lambda w: pl.BlockSpec((1, w), lambda b: (0, 0))
    a_in, a_out = _adaln_specs(d, tn, first_col // tn)
    r_in, r_out, r_shape = _cast_riders(riders, n_batch, lambda b: b)
    return pl.pallas_call(
        functools.partial(_attn_self_kernel, lam_init=lam_init, n_riders=len(riders)),
        grid=(n_batch,),
        in_specs=[blk, blk, blk,
                  vec(HEAD_DIM), vec(HEAD_DIM), vec(HEAD_DIM), vec(HEAD_DIM), vec(V_DIM)]
                 + a_in + r_in,
        out_specs=[blk, a_out] + r_out,
        out_shape=[jax.ShapeDtypeStruct((m, width), BF16),
                   jax.ShapeDtypeStruct((MOD_ROWS, n_mod - first_col), F32)] + r_shape,
        compiler_params=_cparams(1),
        name="attn_ctx",
    )(q, k, v, *lams, subln_w, c_rows, w_ada, b_ada, *[w for w, _ in riders])


def _outproj_kernel(att_ref, sgu_ref, wo_ref, x_ref, g1_ref, sh2_ref, sc2_ref, n2_ref,
                    y_ref, h_ref, *, sub):
    half = att_ref.shape[1]
    tm = x_ref.shape[0]
    n_sub = tm // sub

    def mix(r):
        rows = slice(r * sub, (r + 1) * sub)
        return (jnp.dot(att_ref[rows, :], wo_ref[:half], preferred_element_type=F32)
                + jnp.dot(sgu_ref[rows, :], wo_ref[half:], preferred_element_type=F32))

    nxt = mix(0)
    w2 = n2_ref[...] * (1.0 + sc2_ref[...])
    for r in range(n_sub):
        cur, nxt = nxt, (mix(r + 1) if r + 1 < n_sub else None)
        rows = slice(r * sub, (r + 1) * sub)
        y = x_ref[rows, :] + g1_ref[...] * cur
        y_ref[rows, :] = y
        ms = jnp.mean(y * y, axis=-1, keepdims=True)
        h_ref[rows, :] = (y * lax.rsqrt(ms + EPS) * w2 + sh2_ref[...]).astype(BF16)


def _outproj(att, sgu, w_o, x, mod, mod_row, norm2_w, *, tm, sub):
    m, d = x.shape
    half = att.shape[1]
    mod_spec = lambda which: pl.BlockSpec(
        (None, None, 1, d), lambda i: (mod_row(i), which, 0, 0))
    return pl.pallas_call(
        functools.partial(_outproj_kernel, sub=sub),
        grid=(m // tm,),
        in_specs=[pl.BlockSpec((tm, half), lambda i: (i, 0)),
                  pl.BlockSpec((tm, half), lambda i: (i, 0)),
                  pl.BlockSpec(w_o.shape, lambda i: (0, 0)),
                  pl.BlockSpec((tm, d), lambda i: (i, 0)),
                  mod_spec(0), mod_spec(1), mod_spec(2),
                  pl.BlockSpec((1, d), lambda i: (0, 0))],
        out_specs=[pl.BlockSpec((tm, d), lambda i: (i, 0))] * 2,
        out_shape=[jax.ShapeDtypeStruct((m, d), F32), jax.ShapeDtypeStruct((m, d), BF16)],
        compiler_params=_cparams(1),
        name="outproj",
    )(att, sgu, w_o, x, mod, mod, mod, norm2_w)


def _mlp_kernel(h_ref, w1_hbm, w2_hbm, y1_hbm, g2_ref, o_ref, w1_buf, w2_buf, y1_buf, w_sem,
                y1_sem, *, tf):
    i = pl.program_id(0)
    n_tiles = pl.num_programs(0)
    tm = o_ref.shape[0]
    nf = w1_hbm.shape[1] // tf

    def w_copies(f, slot):
        cols = pl.ds(pl.multiple_of(f * tf, tf), tf)
        return (pltpu.make_async_copy(w1_hbm.at[:, cols], w1_buf.at[slot], w_sem.at[0, slot]),
                pltpu.make_async_copy(w2_hbm.at[cols, :], w2_buf.at[slot], w_sem.at[1, slot]))

    def start(f, slot):
        for cp in w_copies(f, slot):
            cp.start()

    def wait(f, slot):
        for cp in w_copies(f, slot):
            cp.wait()

    def y1_copy():
        rows = pl.ds(pl.multiple_of(i * tm, tm), tm)
        return pltpu.make_async_copy(y1_hbm.at[rows, :], y1_buf, y1_sem)

    def part(slot):
        hid = jnp.dot(h_ref[...], w1_buf[slot], preferred_element_type=F32)
        hid = jnp.square(jnp.maximum(hid, 0.0)).astype(BF16)
        return jnp.dot(hid, w2_buf[slot], preferred_element_type=F32)

    @pl.when(i == 0)
    def _():
        start(0, 0)

    y1_copy().start()
    start(1, 1)
    wait(0, 0)
    o_ref[...] = part(0)

    def middle(f, carry):
        slot = f % 2
        start(f + 1, 1 - slot)
        wait(f, slot)
        o_ref[...] += part(slot)
        return carry

    lax.fori_loop(1, nf - 1, middle, 0)

    @pl.when(i + 1 < n_tiles)
    def _():
        start(0, 0)

    wait(nf - 1, 1)
    y1_copy().wait()
    o_ref[...] = y1_buf[...] + g2_ref[...] * (o_ref[...] + part(1))


def _mlp(h, w1, w2, y1, mod, mod_row, *, tm, tf):
    m, d = h.shape
    d_ff = w1.shape[1]
    nf = d_ff // tf
    assert nf >= 2 and nf % 2 == 0
    return pl.pallas_call(
        functools.partial(_mlp_kernel, tf=tf),
        grid=(m // tm,),
        in_specs=[pl.BlockSpec((tm, d), lambda i: (i, 0)),
                  pl.BlockSpec(memory_space=pl.ANY),
                  pl.BlockSpec(memory_space=pl.ANY),
                  pl.BlockSpec(memory_space=pl.ANY),
                  pl.BlockSpec((None, None, 1, d), lambda i: (mod_row(i), 3, 0, 0))],
        out_specs=pl.BlockSpec((tm, d), lambda i: (i, 0)),
        out_shape=jax.ShapeDtypeStruct((m, d), F32),
        scratch_shapes=[pltpu.VMEM((2, d, tf), BF16), pltpu.VMEM((2, tf, d), BF16),
                        pltpu.VMEM((tm, d), F32),
                        pltpu.SemaphoreType.DMA((2, 2)), pltpu.SemaphoreType.DMA(())],
        compiler_params=_cparams(1),
        name="mlp",
    )(h, w1, w2, y1, mod)


def _rope_tables(n):
    rows = n // GRID_W
    r, col = np.meshgrid(np.arange(rows), np.arange(GRID_W), indexing="ij")
    r = r.reshape(-1).astype(np.float32)
    col = col.reshape(-1).astype(np.float32)
    n_freq = HEAD_DIM // 4
    freqs = np.float32(ROPE_BASE) ** (-np.arange(n_freq, dtype=np.float32) / np.float32(n_freq))
    ang_r = r[:, None] * freqs
    ang_c = col[:, None] * freqs
    ang = np.concatenate([ang_r, ang_r, ang_c, ang_c], axis=-1).astype(np.float32)
    cos = np.tile(np.cos(ang), (1, LANES // HEAD_DIM)).astype(np.float32)
    sin = np.tile(np.sin(ang), (1, LANES // HEAD_DIM)).astype(np.float32)
    first = (np.arange(LANES) % (2 * n_freq)) < n_freq
    zero = np.float32(0.0)
    return (jnp.asarray(cos), jnp.asarray(np.where(first, -sin, zero)),
            jnp.asarray(np.where(first, zero, sin)))


def _block_diag_ones(n, blk):
    idx = np.arange(n) // blk
    return jnp.asarray(idx[:, None] == idx[None, :], dtype=BF16)


def kernel(x_prompt, x_sample, cache_k, cache_v, c, c_ctx, w_ada, b_ada, norm1_w, norm2_w,
           w_in, q_norm_w, k_norm_w, lambda_q1, lambda_k1, lambda_q2, lambda_k2, subln_w,
           sgu_norm_w, w_s, b_s, w_o, w_ff1, w_ff2):
    n_ctx, ctx_len, d = x_prompt.shape
    n_lat, lat_len, _ = x_sample.shape
    depth = w_in.shape[0]
    past = cache_k.shape[2]
    width = w_o.shape[1] // 2
    n_heads = width // V_DIM
    tiles = _TILES
    assert n_lat + 1 <= MOD_ROWS and ctx_len % CHUNK == 0
    assert lat_len % tiles.tm == 0 and lat_len % tiles.tm_mlp == 0

    y_p = x_prompt.reshape(n_ctx * ctx_len, d)
    y_s = x_sample.reshape(n_lat * lat_len, d)
    rope_tabs = _rope_tables(lat_len)
    bd = _block_diag_ones(MXU_DIM, HEAD_DIM)
    c_rows = jnp.concatenate(
        [c_ctx[None, :], c, jnp.zeros((MOD_ROWS - 1 - n_lat, d), F32)], axis=0)
    ctx_row = lambda tm: (lambda i: 0)
    lat_row = lambda tm: (lambda i: 1 + i // (lat_len // tm))

    state_k, state_v = [], []
    for l in range(depth):
        lam_init = 0.8 - 0.6 * math.exp(-0.3 * l)
        w_in_l = w_in[l].astype(BF16)
        ws_l = w_s[l].astype(BF16)
        bs_l = b_s[l][:, :, None]
        row = lambda a: a[l].reshape(1, -1)
        qw = jnp.tile(row(q_norm_w), (1, width // HEAD_DIM))
        kw = jnp.tile(row(k_norm_w), (1, width // HEAD_DIM))
        lams = (row(lambda_q1), row(lambda_k1), row(lambda_q2), row(lambda_k2))

        b_ada_l = b_ada[l].reshape(1, -1)
        mod_in = _adaln(c_rows, w_ada[l], b_ada_l, N_MOD_IN * d).reshape(
            MOD_ROWS, N_MOD_IN, 1, d)

        def mixers(x, mod_row, tabs, state_seq):
            return _inproj(x, mod_in, mod_row(tiles.tm), row(norm1_w), w_in_l, qw, kw, bd, tabs,
                           row(sgu_norm_w), ws_l, bs_l, tm=tiles.tm, state_seq=state_seq)

        def finish(x, att, sgu, mod_row):
            y1, h2 = _outproj(att, sgu, w_o_l, x, mod_out, mod_row(tiles.tm), row(norm2_w),
                              tm=tiles.tm, sub=tiles.tm_sub)
            return _mlp(h2, w1_l, w2_l, y1, mod_out, mod_row(tiles.tm_mlp),
                        tm=tiles.tm_mlp, tf=tiles.tf)

        q_p, k_p, v_p, sgu_p, sk, sv = mixers(y_p, ctx_row, None, ctx_len)
        sk = sk.reshape(n_ctx, n_heads, 2, HEAD_DIM, ctx_len)
        state_k.append(jnp.transpose(sk, (0, 4, 1, 2, 3)))
        state_v.append(sv.reshape(n_ctx, ctx_len, n_heads, V_DIM))
        q_s, k_s, v_s, sgu_s = mixers(y_s, lat_row, rope_tabs, None)

        att_p, mod_out, w2_l = _attn_self(
            q_p, k_p, v_p, lams, row(subln_w), c_rows, w_ada[l], b_ada_l, N_MOD_IN * d,
            [(w_ff2[l], 0)], n_batch=n_ctx, lam_init=lam_init)
        mod_out = mod_out.reshape(MOD_ROWS, N_MOD_OUT, 1, d)
        cache_kt = jnp.transpose(cache_k[:, l], (0, 2, 3, 4, 1)).reshape(n_lat, width, past)
        att_s, w1_l, w_o_l = _attn_cached(
            q_s, k_s, v_s, cache_kt, cache_v[:, l].reshape(n_lat * past, width),
            lams, row(subln_w), [(w_ff1[l], 1), (w_o[l], 0)], n_batch=n_lat, lam_init=lam_init,
            tq=tiles.tq, sub=tiles.tq_sub, heads=n_heads)

        y_p = finish(y_p, att_p, sgu_p, ctx_row)
        y_s = finish(y_s, att_s, sgu_s, lat_row)

    return (y_p.reshape(n_ctx, ctx_len, d), y_s.reshape(n_lat, lat_len, d),
            jnp.stack(state_k, axis=1), jnp.stack(state_v, axis=1))
```

```python
import functools
import math
from typing import NamedTuple

import jax
import jax.numpy as jnp
import numpy as np
from jax import lax
from jax.experimental import pallas as pl
from jax.experimental.pallas import tpu as pltpu

F32 = jnp.float32
BF16 = jnp.bfloat16

HEAD_DIM = 64
V_DIM = 128
CHUNK = 128
GROUP = 128
GRID_W = 64
ROPE_BASE = 10000.0
EPS = 1e-6
N_MOD_IN = 2
N_MOD_OUT = 4
MOD_ROWS = 8
LANES = 128
MXU_DIM = 256
BF16_ROWS = 16
VMEM_BYTES = 64 * 1024 * 1024
VMEM_LIMIT = VMEM_BYTES - 4 * 1024 * 1024
CTX_RIDER_CHUNKS = 4
Q_SCALE = HEAD_DIM ** -0.5 * math.log2(math.e)


class _Tiles(NamedTuple):
    tm: int
    tm_sub: int
    tm_mlp: int
    tf: int
    tq: int
    tq_sub: int


_TILES = _Tiles(tm=512, tm_sub=256, tm_mlp=1024, tf=1024, tq=512, tq_sub=256)


def _cparams(n_axes):
    return pltpu.CompilerParams(
        dimension_semantics=("arbitrary",) * n_axes,
        vmem_limit_bytes=VMEM_LIMIT)


def _adaln_slab(c_ref, w_ref, b_ref, o_ref):
    c = c_ref[...]
    s = c * (1.0 / (1.0 + jnp.exp(-c)))
    o_ref[...] = jnp.dot(s.astype(BF16), w_ref[...].astype(BF16),
                         preferred_element_type=F32) + b_ref[...]


def _adaln_specs(d, tn, first_block):
    return ([pl.BlockSpec((MOD_ROWS, d), lambda *g: (0, 0)),
             pl.BlockSpec((d, tn), lambda *g: (0, first_block + g[-1])),
             pl.BlockSpec((1, tn), lambda *g: (0, first_block + g[-1]))],
            pl.BlockSpec((MOD_ROWS, tn), lambda *g: (0, g[-1])))


def _adaln(c_rows, w_ada, b_ada, n_cols, tn=1024):
    d = w_ada.shape[0]
    in_specs, out_spec = _adaln_specs(d, tn, 0)
    return pl.pallas_call(
        _adaln_slab,
        grid=(n_cols // tn,),
        in_specs=in_specs,
        out_specs=out_spec,
        out_shape=jax.ShapeDtypeStruct((MOD_ROWS, n_cols), F32),
        compiler_params=_cparams(1),
        name="adaln",
    )(c_rows, w_ada, b_ada)


def _gelu(x):
    return jax.nn.gelu(x, approximate=True)


def _head_norm(z, w_ref, bd_ref):
    outs = []
    for c in range(z.shape[1] // MXU_DIM):
        zc = z[:, c * MXU_DIM:(c + 1) * MXU_DIM]
        ss = jnp.dot((zc * zc).astype(BF16), bd_ref[...], preferred_element_type=F32)
        r = lax.rsqrt(ss * (1.0 / HEAD_DIM) + EPS)
        outs.append(zc * r * w_ref[:, c * MXU_DIM:(c + 1) * MXU_DIM])
    return outs


def _rope128(x, cos, sin_lo, sin_hi):
    quarter = HEAD_DIM // 4
    return (x * cos + pltpu.roll(x, LANES - quarter, 1) * sin_lo
            + pltpu.roll(x, quarter, 1) * sin_hi)


def _inproj_kernel(*refs, rope, emit_state):
    it = iter(refs)
    x_ref, sh_ref, sc_ref, n1_ref, w_ref, qw_ref, kw_ref, bd_ref = (next(it) for _ in range(8))
    if rope:
        cos_ref, slo_ref, shi_ref = (next(it) for _ in range(3))
    sgw_ref, ws_ref, bs_ref = (next(it) for _ in range(3))
    q_ref, k_ref, v_ref, sgu_ref = (next(it) for _ in range(4))
    if emit_state:
        sk_ref, sv_ref = (next(it) for _ in range(2))
    tm = x_ref.shape[0]
    width = q_ref.shape[1]

    x = x_ref[...]
    ms = jnp.mean(x * x, axis=-1, keepdims=True)
    h = (x * lax.rsqrt(ms + EPS) * (n1_ref[...] * (1.0 + sc_ref[...])) + sh_ref[...]).astype(BF16)

    def proj(j):
        return jnp.dot(h, w_ref[:, j * width:(j + 1) * width], preferred_element_type=F32)

    def qk_epilogue(z, w_norm_ref, out_ref, scale, state_ref):
        blocks = _head_norm(z, w_norm_ref, bd_ref)
        for c, n in enumerate(blocks):
            if state_ref is not None:
                seq = state_ref.shape[2]
                for s in range(tm // seq):
                    state_ref[s, c * MXU_DIM:(c + 1) * MXU_DIM, :] = n[s * seq:(s + 1) * seq].T
            for hh in range(MXU_DIM // LANES):
                xh = n[:, hh * LANES:(hh + 1) * LANES]
                if rope:
                    xh = _rope128(xh, cos_ref[...], slo_ref[...], shi_ref[...])
                if scale != 1.0:
                    xh = xh * scale
                lo = c * MXU_DIM + hh * LANES
                out_ref[:, lo:lo + LANES] = xh.astype(BF16)

    zq = proj(0)
    zk = proj(1)
    qk_epilogue(zq, qw_ref, q_ref, Q_SCALE, None)
    zv = proj(2)
    qk_epilogue(zk, kw_ref, k_ref, 1.0, sk_ref if emit_state else None)
    zu = proj(3)
    v_ref[...] = zv.astype(BF16)
    if emit_state:
        sv_ref[...] = zv
    zg = proj(4)
    gu = _gelu(zu)
    g = _gelu(zg)
    mu = jnp.mean(g, axis=-1, keepdims=True)
    gc = g - mu
    var = jnp.mean(gc * gc, axis=-1, keepdims=True)
    gn = (gc * lax.rsqrt(var + EPS) * sgw_ref[...]).astype(BF16)
    n_chunks = tm // CHUNK
    for grp in range(width // GROUP):
        cols = slice(grp * GROUP, (grp + 1) * GROUP)
        rhs = jnp.concatenate(
            [gn[c * CHUNK:(c + 1) * CHUNK, cols] for c in range(n_chunks)], axis=1)
        mixed = jnp.dot(ws_ref[grp], rhs, preferred_element_type=F32) + bs_ref[grp]
        for c in range(n_chunks):
            rows = slice(c * CHUNK, (c + 1) * CHUNK)
            sgu_ref[rows, cols] = (
                gu[rows, cols] * mixed[:, c * CHUNK:(c + 1) * CHUNK]).astype(BF16)


def _inproj(x, mod, mod_row, norm1_w, w_in, qw, kw, bd, rope_tabs, sgw, ws, bs,
            *, tm, state_seq):
    m, d = x.shape
    n_in = w_in.shape[1]
    width = n_in // 5
    rope = rope_tabs is not None
    const = lambda shape: pl.BlockSpec(shape, lambda i: (0,) * len(shape),
                                       pipeline_mode=pl.Buffered(1))
    mod_spec = lambda which: pl.BlockSpec(
        (None, None, 1, d), lambda i: (mod_row(i), which, 0, 0))
    in_specs = [
        pl.BlockSpec((tm, d), lambda i: (i, 0)),
        mod_spec(0), mod_spec(1), const((1, d)),
        const((d, n_in)),
        const((1, width)), const((1, width)),
        const((MXU_DIM, MXU_DIM)),
    ]
    args = [x, mod, mod, norm1_w, w_in, qw, kw, bd]
    if rope:
        seq_tiles = rope_tabs[0].shape[0] // tm
        tab_spec = pl.BlockSpec((tm, LANES), lambda i: (i % seq_tiles, 0))
        in_specs += [tab_spec] * 3
        args += list(rope_tabs)
    in_specs += [const((1, width)), const(ws.shape), const(bs.shape)]
    args += [sgw, ws, bs]
    out_spec = pl.BlockSpec((tm, width), lambda i: (i, 0))
    out_shape = [jax.ShapeDtypeStruct((m, width), BF16)] * 4
    out_specs = [out_spec] * 4
    emit_state = state_seq is not None
    if emit_state:
        out_shape += [jax.ShapeDtypeStruct((m // state_seq, width, state_seq), F32),
                      jax.ShapeDtypeStruct((m, width), F32)]
        out_specs += [pl.BlockSpec((tm // state_seq, width, state_seq), lambda i: (i, 0, 0)),
                      out_spec]
    return pl.pallas_call(
        functools.partial(_inproj_kernel, rope=rope, emit_state=emit_state),
        grid=(m // tm,),
        in_specs=in_specs,
        out_specs=out_specs,
        out_shape=out_shape,
        compiler_params=_cparams(1),
        name="inproj_rope" if rope else "inproj_ctx",
    )(*args)


def _lambda(lq1_ref, lk1_ref, lq2_ref, lk2_ref, lam_init):
    a = jnp.sum(lq1_ref[...] * lk1_ref[...], axis=-1, keepdims=True)
    b = jnp.sum(lq2_ref[...] * lk2_ref[...], axis=-1, keepdims=True)
    return jnp.exp(a) - jnp.exp(b) + lam_init


def _lane_fold(op, acc, tile):
    for c in range(0, tile.shape[1], LANES):
        blk = tile[:, c:c + LANES]
        acc = blk if acc is None else op(acc, blk)
    return acc


class _ScoreTiles:
    def __init__(self, q, keys):
        lane = lax.broadcasted_iota(jnp.int32, q.shape, 1)
        zero = jnp.zeros_like(q)
        self.q2 = jnp.concatenate([jnp.where(lane < HEAD_DIM, q, zero),
                                   jnp.where(lane >= HEAD_DIM, q, zero)], axis=0)
        self.slices = []
        for k, feature_major in keys:
            n_keys = k.shape[1] if feature_major else k.shape[0]
            self.slices += [(k, feature_major, lo) for lo in range(0, n_keys, MXU_DIM)]
        self.tiles = []
        self.m_lanes = None

    def __len__(self):
        return len(self.slices)

    def compute(self, t):
        k, feature_major, lo = self.slices[t]
        if feature_major:
            st = jnp.dot(self.q2, k[:, lo:lo + MXU_DIM], preferred_element_type=F32)
        else:
            st = lax.dot_general(self.q2, k[lo:lo + MXU_DIM], (((1,), (1,)), ((), ())),
                                 preferred_element_type=F32)
        self.tiles.append(st)
        self.m_lanes = _lane_fold(jnp.maximum, self.m_lanes, st)

    def row_max(self):
        return jnp.max(self.m_lanes, axis=-1, keepdims=True)


def _attn_units(q_ref, o_ref, sub, lockstep, load_keys, load_values, lam, subln_w, lam_init,
                fillers):
    units = [(slice(r * sub, (r + 1) * sub), slice(h * V_DIM, (h + 1) * V_DIM))
             for r in range(q_ref.shape[0] // sub) for h in range(q_ref.shape[1] // V_DIM)]
    groups = [units[i:i + lockstep] for i in range(0, len(units), lockstep)]
    n_slots = 3 * len(groups)
    pending = list(fillers)

    def fill(slot):
        done = len(fillers) - len(pending)
        for _ in range(len(fillers) * (slot + 1) // n_slots - done):
            pending.pop(0)()

    def scores(group):
        out = []
        for rows, cols in group:
            tiles = _ScoreTiles(q_ref[rows, cols], load_keys(cols))
            for t in range(len(tiles)):
                tiles.compute(t)
            out.append(tiles)
        return out

    nxt = scores(groups[0])
    for i, group in enumerate(groups):
        cur, nxt = nxt, (scores(groups[i + 1]) if i + 1 < len(groups) else None)
        fill(3 * i)
        ms = [tiles.row_max() for tiles in cur]
        es = [jnp.concatenate([jnp.exp2(st - m).astype(BF16) for st in tiles.tiles], axis=1)
              for tiles, m in zip(cur, ms)]
        fill(3 * i + 1)
        ols = []
        for e, (rows, cols) in zip(es, group):
            v = jnp.concatenate(load_values(cols), axis=0)
            v1 = jnp.concatenate([v, jnp.ones_like(v)], axis=1)
            ols.append(jnp.dot(e, v1, preferred_element_type=F32))
        fill(3 * i + 2)
        outs = []
        for ol in ols:
            l1, l2 = ol[:sub, V_DIM:V_DIM + 1], ol[sub:, V_DIM:V_DIM + 1]
            outs.append(ol[:sub, :V_DIM] * (1.0 / l1) - ol[sub:, :V_DIM] * (lam / l2))
        sq = [jnp.mean(o * o, axis=-1, keepdims=True) for o in outs]
        for o, s2, (rows, cols) in zip(outs, sq, group):
            o_ref[rows, cols] = (o * lax.rsqrt(s2 + EPS) * subln_w
                                 * (1.0 - lam_init)).astype(o_ref.dtype)


def _cast_riders(riders, n_steps, step_of):
    in_specs, out_specs, out_shape = [], [], []
    for w, axis in riders:
        blk = tuple(n // n_steps if a == axis else n for a, n in enumerate(w.shape))
        idx = lambda *g, axis=axis, nd=w.ndim: tuple(
            step_of(*g) if a == axis else 0 for a in range(nd))
        in_specs.append(pl.BlockSpec(blk, idx))
        out_specs.append(pl.BlockSpec(blk, idx))
        out_shape.append(jax.ShapeDtypeStruct(w.shape, BF16))
    return in_specs, out_specs, out_shape


def _cast_rider_fillers(refs, n, n_chunks):
    def chunk(src, dst, rows):
        def cast():
            dst[rows, :] = src[rows, :].astype(dst.dtype)
        return cast

    fillers = []
    for src, dst in zip(refs[:n], refs[n:]):
        k = min(n_chunks, src.shape[0] // BF16_ROWS)
        step = src.shape[0] // k
        fillers += [chunk(src, dst, slice(c * step, (c + 1) * step)) for c in range(k)]
    return fillers


def _attn_cached_kernel(q_ref, kn_ref, vn_ref, kct_ref, vc_ref, lq1, lk1, lq2, lk2, sw_ref,
                        *rest, lam_init, sub, n_riders):
    o_ref = rest[n_riders]
    n_units = (q_ref.shape[0] // sub) * (q_ref.shape[1] // V_DIM)
    fillers = _cast_rider_fillers(rest[:n_riders] + rest[n_riders + 1:], n_riders, n_units)
    lam = _lambda(lq1, lk1, lq2, lk2, lam_init)
    _attn_units(
        q_ref, o_ref, sub, 1,
        lambda cols: [(kct_ref[cols, :].astype(BF16), True), (kn_ref[:, cols], False)],
        lambda cols: [vc_ref[:, cols].astype(BF16), vn_ref[:, cols]],
        lam, sw_ref[...], lam_init, fillers)


def _attn_cached(q, k, v, cache_kt, cache_v, lams, subln_w, riders, *, n_batch, lam_init, tq,
                 sub, heads):
    m, width = q.shape
    seq = m // n_batch
    past = cache_kt.shape[2]
    nq = seq // tq
    w = heads * V_DIM
    n_groups = width // w
    vec = lambda n: pl.BlockSpec((1, n), lambda b, g, t: (0, 0))
    r_in, r_out, r_shape = _cast_riders(
        riders, n_batch * n_groups * nq, lambda b, g, t: (b * n_groups + g) * nq + t)
    return pl.pallas_call(
        functools.partial(_attn_cached_kernel, lam_init=lam_init, sub=sub,
                          n_riders=len(riders)),
        grid=(n_batch, n_groups, nq),
        in_specs=[pl.BlockSpec((tq, w), lambda b, g, t: (b * nq + t, g)),
                  pl.BlockSpec((seq, w), lambda b, g, t: (b, g)),
                  pl.BlockSpec((seq, w), lambda b, g, t: (b, g)),
                  pl.BlockSpec((None, w, past), lambda b, g, t: (b, g, 0)),
                  pl.BlockSpec((past, w), lambda b, g, t: (b, g)),
                  vec(HEAD_DIM), vec(HEAD_DIM), vec(HEAD_DIM), vec(HEAD_DIM), vec(V_DIM)] + r_in,
        out_specs=[pl.BlockSpec((tq, w), lambda b, g, t: (b * nq + t, g))] + r_out,
        out_shape=[jax.ShapeDtypeStruct((m, width), BF16)] + r_shape,
        compiler_params=_cparams(3),
        name="attn_latent",
    )(q, k, v, cache_kt, cache_v, *lams, subln_w, *[w for w, _ in riders])


def _attn_self_kernel(q_ref, k_ref, v_ref, lq1, lk1, lq2, lk2, sw_ref, c_ref, wa_ref, ba_ref,
                      *rest, lam_init, n_riders):
    o_ref, mod_ref = rest[n_riders], rest[n_riders + 1]
    fillers = _cast_rider_fillers(rest[:n_riders] + rest[n_riders + 2:], n_riders,
                                  CTX_RIDER_CHUNKS)
    _adaln_slab(c_ref, wa_ref, ba_ref, mod_ref)
    lam = _lambda(lq1, lk1, lq2, lk2, lam_init)
    _attn_units(q_ref, o_ref, q_ref.shape[0], q_ref.shape[1] // V_DIM,
                lambda cols: [(k_ref[:, cols], False)],
                lambda cols: [v_ref[:, cols]],
                lam, sw_ref[...], lam_init, fillers)


def _attn_self(q, k, v, lams, subln_w, c_rows, w_ada, b_ada, first_col, riders, *, n_batch,
               lam_init):
    m, width = q.shape
    seq = m // n_batch
    d, n_mod = w_ada.shape
    tn = (n_mod - first_col) // n_batch
    blk = pl.BlockSpec((seq, width), lambda b: (b, 0))
    vec = lambda w: pl.BlockSpec((1, w), lambda b: (0, 0))
    a_in, a_out = _adaln_specs(d, tn, first_col // tn)
    r_in, r_out, r_shape = _cast_riders(riders, n_batch, lambda b: b)
    return pl.pallas_call(
        functools.partial(_attn_self_kernel, lam_init=lam_init, n_riders=len(riders)),
        grid=(n_batch,),
        in_specs=[blk, blk, blk,
                  vec(HEAD_DIM), vec(HEAD_DIM), vec(HEAD_DIM), vec(HEAD_DIM), vec(V_DIM)]
                 + a_in + r_in,
        out_specs=[blk, a_out] + r_out,
        out_shape=[jax.ShapeDtypeStruct((m, width), BF16),
                   jax.ShapeDtypeStruct((MOD_ROWS, n_mod - first_col), F32)] + r_shape,
        compiler_params=_cparams(1),
        name="attn_ctx",
    )(q, k, v, *lams, subln_w, c_rows, w_ada, b_ada, *[w for w, _ in riders])


def _outproj_kernel(att_ref, sgu_ref, wo_ref, x_ref, g1_ref, sh2_ref, sc2_ref, n2_ref,
                    y_ref, h_ref, *, sub):
    half = att_ref.shape[1]
    tm = x_ref.shape[0]
    n_sub = tm // sub

    def mix(r):
        rows = slice(r * sub, (r + 1) * sub)
        return (jnp.dot(att_ref[rows, :], wo_ref[:half], preferred_element_type=F32)
                + jnp.dot(sgu_ref[rows, :], wo_ref[half:], preferred_element_type=F32))

    nxt = mix(0)
    w2 = n2_ref[...] * (1.0 + sc2_ref[...])
    for r in range(n_sub):
        cur, nxt = nxt, (mix(r + 1) if r + 1 < n_sub else None)
        rows = slice(r * sub, (r + 1) * sub)
        y = x_ref[rows, :] + g1_ref[...] * cur
        y_ref[rows, :] = y
        ms = jnp.mean(y * y, axis=-1, keepdims=True)
        h_ref[rows, :] = (y * lax.rsqrt(ms + EPS) * w2 + sh2_ref[...]).astype(BF16)


def _outproj(att, sgu, w_o, x, mod, mod_row, norm2_w, *, tm, sub):
    m, d = x.shape
    half = att.shape[1]
    mod_spec = lambda which: pl.BlockSpec(
        (None, None, 1, d), lambda i: (mod_row(i), which, 0, 0))
    return pl.pallas_call(
        functools.partial(_outproj_kernel, sub=sub),
        grid=(m // tm,),
        in_specs=[pl.BlockSpec((tm, half), lambda i: (i, 0)),
                  pl.BlockSpec((tm, half), lambda i: (i, 0)),
                  pl.BlockSpec(w_o.shape, lambda i: (0, 0)),
                  pl.BlockSpec((tm, d), lambda i: (i, 0)),
                  mod_spec(0), mod_spec(1), mod_spec(2),
                  pl.BlockSpec((1, d), lambda i: (0, 0))],
        out_specs=[pl.BlockSpec((tm, d), lambda i: (i, 0))] * 2,
        out_shape=[jax.ShapeDtypeStruct((m, d), F32), jax.ShapeDtypeStruct((m, d), BF16)],
        compiler_params=_cparams(1),
        name="outproj",
    )(att, sgu, w_o, x, mod, mod, mod, norm2_w)


def _mlp_kernel(h_ref, w1_hbm, w2_hbm, y1_hbm, g2_ref, o_ref, w1_buf, w2_buf, y1_buf, w_sem,
                y1_sem, *, tf):
    i = pl.program_id(0)
    n_tiles = pl.num_programs(0)
    tm = o_ref.shape[0]
    nf = w1_hbm.shape[1] // tf

    def w_copies(f, slot):
        cols = pl.ds(pl.multiple_of(f * tf, tf), tf)
        return (pltpu.make_async_copy(w1_hbm.at[:, cols], w1_buf.at[slot], w_sem.at[0, slot]),
                pltpu.make_async_copy(w2_hbm.at[cols, :], w2_buf.at[slot], w_sem.at[1, slot]))

    def start(f, slot):
        for prio, cp in enumerate(w_copies(f, slot)):
            cp.start(priority=prio)

    def wait(f, slot):
        for cp in w_copies(f, slot):
            cp.wait()

    def y1_copy():
        rows = pl.ds(pl.multiple_of(i * tm, tm), tm)
        return pltpu.make_async_copy(y1_hbm.at[rows, :], y1_buf, y1_sem)

    def part(slot):
        hid = jnp.dot(h_ref[...], w1_buf[slot], preferred_element_type=F32)
        hid = jnp.square(jnp.maximum(hid, 0.0)).astype(BF16)
        return jnp.dot(hid, w2_buf[slot], preferred_element_type=F32)

    @pl.when(i == 0)
    def _():
        start(0, 0)

    y1_copy().start()
    start(1, 1)
    wait(0, 0)
    o_ref[...] = part(0)

    def middle(f, carry):
        slot = f % 2
        start(f + 1, 1 - slot)
        wait(f, slot)
        o_ref[...] += part(slot)
        return carry

    lax.fori_loop(1, nf - 1, middle, 0)

    @pl.when(i + 1 < n_tiles)
    def _():
        start(0, 0)

    wait(nf - 1, 1)
    y1_copy().wait()
    o_ref[...] = y1_buf[...] + g2_ref[...] * (o_ref[...] + part(1))


def _mlp(h, w1, w2, y1, mod, mod_row, *, tm, tf):
    m, d = h.shape
    d_ff = w1.shape[1]
    nf = d_ff // tf
    assert nf >= 2 and nf % 2 == 0
    return pl.pallas_call(
        functools.partial(_mlp_kernel, tf=tf),
        grid=(m // tm,),
        in_specs=[pl.BlockSpec((tm, d), lambda i: (i, 0)),
                  pl.BlockSpec(memory_space=pl.ANY),
                  pl.BlockSpec(memory_space=pl.ANY),
                  pl.BlockSpec(memory_space=pl.ANY),
                  pl.BlockSpec((None, None, 1, d), lambda i: (mod_row(i), 3, 0, 0))],
        out_specs=pl.BlockSpec((tm, d), lambda i: (i, 0)),
        out_shape=jax.ShapeDtypeStruct((m, d), F32),
        scratch_shapes=[pltpu.VMEM((2, d, tf), BF16), pltpu.VMEM((2, tf, d), BF16),
                        pltpu.VMEM((tm, d), F32),
                        pltpu.SemaphoreType.DMA((2, 2)), pltpu.SemaphoreType.DMA(())],
        compiler_params=_cparams(1),
        name="mlp",
    )(h, w1, w2, y1, mod)


def _rope_tables(n):
    rows = n // GRID_W
    r, col = np.meshgrid(np.arange(rows), np.arange(GRID_W), indexing="ij")
    r = r.reshape(-1).astype(np.float32)
    col = col.reshape(-1).astype(np.float32)
    n_freq = HEAD_DIM // 4
    freqs = np.float32(ROPE_BASE) ** (-np.arange(n_freq, dtype=np.float32) / np.float32(n_freq))
    ang_r = r[:, None] * freqs
    ang_c = col[:, None] * freqs
    ang = np.concatenate([ang_r, ang_r, ang_c, ang_c], axis=-1).astype(np.float32)
    cos = np.tile(np.cos(ang), (1, LANES // HEAD_DIM)).astype(np.float32)
    sin = np.tile(np.sin(ang), (1, LANES // HEAD_DIM)).astype(np.float32)
    first = (np.arange(LANES) % (2 * n_freq)) < n_freq
    zero = np.float32(0.0)
    return (jnp.asarray(cos), jnp.asarray(np.where(first, -sin, zero)),
            jnp.asarray(np.where(first, zero, sin)))


def _block_diag_ones(n, blk):
    idx = np.arange(n) // blk
    return jnp.asarray(idx[:, None] == idx[None, :], dtype=BF16)


def kernel(x_prompt, x_sample, cache_k, cache_v, c, c_ctx, w_ada, b_ada, norm1_w, norm2_w,
           w_in, q_norm_w, k_norm_w, lambda_q1, lambda_k1, lambda_q2, lambda_k2, subln_w,
           sgu_norm_w, w_s, b_s, w_o, w_ff1, w_ff2):
    n_ctx, ctx_len, d = x_prompt.shape
    n_lat, lat_len, _ = x_sample.shape
    depth = w_in.shape[0]
    past = cache_k.shape[2]
    width = w_o.shape[1] // 2
    n_heads = width // V_DIM
    tiles = _TILES
    assert n_lat + 1 <= MOD_ROWS and ctx_len % CHUNK == 0
    assert lat_len % tiles.tm == 0 and lat_len % tiles.tm_mlp == 0

    y_p = x_prompt.reshape(n_ctx * ctx_len, d)
    y_s = x_sample.reshape(n_lat * lat_len, d)
    rope_tabs = _rope_tables(lat_len)
    bd = _block_diag_ones(MXU_DIM, HEAD_DIM)
    c_rows = jnp.concatenate(
        [c_ctx[None, :], c, jnp.zeros((MOD_ROWS - 1 - n_lat, d), F32)], axis=0)
    ctx_row = lambda tm: (lambda i: 0)
    lat_row = lambda tm: (lambda i: 1 + i // (lat_len // tm))

    state_k, state_v = [], []
    for l in range(depth):
        lam_init = 0.8 - 0.6 * math.exp(-0.3 * l)
        w_in_l = w_in[l].astype(BF16)
        ws_l = w_s[l].astype(BF16)
        bs_l = b_s[l][:, :, None]
        row = lambda a: a[l].reshape(1, -1)
        qw = jnp.tile(row(q_norm_w), (1, width // HEAD_DIM))
        kw = jnp.tile(row(k_norm_w), (1, width // HEAD_DIM))
        lams = (row(lambda_q1), row(lambda_k1), row(lambda_q2), row(lambda_k2))

        b_ada_l = b_ada[l].reshape(1, -1)
        mod_in = _adaln(c_rows, w_ada[l], b_ada_l, N_MOD_IN * d).reshape(
            MOD_ROWS, N_MOD_IN, 1, d)

        def mixers(x, mod_row, tabs, state_seq):
            return _inproj(x, mod_in, mod_row(tiles.tm), row(norm1_w), w_in_l, qw, kw, bd, tabs,
                           row(sgu_norm_w), ws_l, bs_l, tm=tiles.tm, state_seq=state_seq)

        def finish(x, att, sgu, mod_row):
            y1, h2 = _outproj(att, sgu, w_o_l, x, mod_out, mod_row(tiles.tm), row(norm2_w),
                              tm=tiles.tm, sub=tiles.tm_sub)
            return _mlp(h2, w1_l, w2_l, y1, mod_out, mod_row(tiles.tm_mlp),
                        tm=tiles.tm_mlp, tf=tiles.tf)

        q_p, k_p, v_p, sgu_p, sk, sv = mixers(y_p, ctx_row, None, ctx_len)
        sk = sk.reshape(n_ctx, n_heads, 2, HEAD_DIM, ctx_len)
        state_k.append(jnp.transpose(sk, (0, 4, 1, 2, 3)))
        state_v.append(sv.reshape(n_ctx, ctx_len, n_heads, V_DIM))
        q_s, k_s, v_s, sgu_s = mixers(y_s, lat_row, rope_tabs, None)

        att_p, mod_out, w2_l = _attn_self(
            q_p, k_p, v_p, lams, row(subln_w), c_rows, w_ada[l], b_ada_l, N_MOD_IN * d,
            [(w_ff2[l], 0)], n_batch=n_ctx, lam_init=lam_init)
        mod_out = mod_out.reshape(MOD_ROWS, N_MOD_OUT, 1, d)
        cache_kt = jnp.transpose(cache_k[:, l], (0, 2, 3, 4, 1)).reshape(n_lat, width, past)
        att_s, w1_l, w_o_l = _attn_cached(
            q_s, k_s, v_s, cache_kt, cache_v[:, l].reshape(n_lat * past, width),
            lams, row(subln_w), [(w_ff1[l], 1), (w_o[l], 0)], n_batch=n_lat, lam_init=lam_init,
            tq=tiles.tq, sub=tiles.tq_sub, heads=n_heads)

        y_p = finish(y_p, att_p, sgu_p, ctx_row)
        y_s = finish(y_s, att_s, sgu_s, lat_row)

    return (y_p.reshape(n_ctx, ctx_len, d), y_s.reshape(n_lat, lat_len, d),
            jnp.stack(state_k, axis=1), jnp.stack(state_v, axis=1))
```
